```python
import jax
import jax.numpy as jnp
from jax import lax
import numpy as np


D_MODEL = 2048
BATCH = 1
SEQ = 8192
DEPTH = 2

N_MIXERS = 2
GRID_W = 64
PLE_DIM = 256
NORM_EPS = 1e-6
D_FF = 4 * D_MODEL

GLA_HEADS = 4
GLA_DK = D_MODEL // 2
GLA_DV = D_MODEL
GLA_HEAD_K = GLA_DK // GLA_HEADS
GLA_HEAD_V = GLA_DV // GLA_HEADS
GLA_GATE_RANK = 16
GLA_GATE_TAU = 16.0
GLA_CHUNK = 64
GLA_IN = 2 * GLA_DK + 2 * GLA_DV + 2 * GLA_GATE_RANK

ATTN_HEAD_DIM = 128
ATTN_Q_HEADS = D_MODEL // ATTN_HEAD_DIM
ATTN_KV_HEADS = 4
ATTN_GROUP = ATTN_Q_HEADS // ATTN_KV_HEADS
ATTN_Q_BLOCK = 128
ATTN_IN = (ATTN_Q_HEADS + 2 * ATTN_KV_HEADS) * ATTN_HEAD_DIM
ROPE_THETA = 10000.0
ROPE_AXIS_DIM = ATTN_HEAD_DIM // 2

kernel_name = 'hybrid_gla_axialgqa_encoder'


def rms_norm(x, gain):
    xf = x.astype(jnp.float32)
    y = xf * lax.rsqrt(jnp.mean(xf * xf, axis=-1, keepdims=True) + NORM_EPS)
    return (y * gain.astype(jnp.float32)).astype(x.dtype)


def gla_chunked(q, k, v, log_a, strict):
    bsz, nh, L, dk = q.shape
    dv = v.shape[-1]
    c = GLA_CHUNK
    n = L // c
    q = q.reshape(bsz, nh, n, c, dk)
    k = k.reshape(bsz, nh, n, c, dk)
    v = v.reshape(bsz, nh, n, c, dv)
    cum = jnp.cumsum(log_a.reshape(bsz, nh, n, c, dk), axis=3)
    last = cum[:, :, :, -1:, :]
    ref = cum[:, :, :, c // 2 - 1:c // 2, :]
    scores = jnp.einsum('bhntd,bhnsd->bhnts', q * jnp.exp(cum - ref), k * jnp.exp(ref - cum))
    mask = jnp.tril(jnp.ones((c, c), dtype=bool), k=-1 if strict else 0)
    o_intra = jnp.einsum('bhnts,bhnse->bhnte', jnp.where(mask, scores, 0.0), v)

    q_dec = q * jnp.exp(cum)
    k_to_end = k * jnp.exp(last - cum)
    chunk_decay = jnp.exp(last[:, :, :, 0, :])

    def step(state, xs):
        qd, kt, vc, dl = xs
        o = jnp.einsum('bhtd,bhde->bhte', qd, state)
        state = dl[..., None] * state + jnp.einsum('bhsd,bhse->bhde', kt, vc)
        return state, o

    xs = (jnp.moveaxis(q_dec, 2, 0), jnp.moveaxis(k_to_end, 2, 0),
          jnp.moveaxis(v, 2, 0), jnp.moveaxis(chunk_decay, 2, 0))
    state0 = jnp.zeros((bsz, nh, dk, dv), jnp.float32)
    _, o_inter = lax.scan(step, state0, xs)
    o = o_intra + jnp.moveaxis(o_inter, 0, 2)
    return o.reshape(bsz, nh, L, dv)


def gla_mixer(u, w_in, w_gk_f, b_gk_f, w_gk_b, b_gk_b, g_head, w_out):
    bsz, L, _ = u.shape
    proj = (u @ w_in).astype(jnp.float32)
    o1 = GLA_DK
    o2 = 2 * GLA_DK
    o3 = o2 + GLA_DV
    o4 = o3 + GLA_DV
    o5 = o4 + GLA_GATE_RANK
    q, k, v, og, lr_f, lr_b = jnp.split(proj, [o1, o2, o3, o4, o5], axis=-1)

    def heads(t, d):
        return t.reshape(bsz, L, GLA_HEADS, d).transpose(0, 2, 1, 3)

    q = heads(q, GLA_HEAD_K) * (GLA_HEAD_K ** -0.5)
    k = heads(k, GLA_HEAD_K)
    v = heads(v, GLA_HEAD_V)
    la_f = heads(jax.nn.log_sigmoid(lr_f @ w_gk_f.astype(jnp.float32) + b_gk_f.astype(jnp.float32)), GLA_HEAD_K) / GLA_GATE_TAU
    la_b = heads(jax.nn.log_sigmoid(lr_b @ w_gk_b.astype(jnp.float32) + b_gk_b.astype(jnp.float32)), GLA_HEAD_K) / GLA_GATE_TAU

    o_f = gla_chunked(q, k, v, la_f, strict=False)
    rev = lambda t: jnp.flip(t, axis=2)
    o_b = rev(gla_chunked(rev(q), rev(k), rev(v), rev(la_b), strict=True))
    o = rms_norm(o_f + o_b, g_head)
    o = o.transpose(0, 2, 1, 3).reshape(bsz, L, GLA_DV)
    o = o * jax.nn.silu(og)
    return o.astype(u.dtype) @ w_out


def axial_rope_tables(L):
    rows = L // GRID_W
    t_row = jnp.repeat(jnp.arange(rows, dtype=jnp.int32), GRID_W).astype(jnp.float32)
    t_col = jnp.tile(jnp.arange(GRID_W, dtype=jnp.int32), rows).astype(jnp.float32)
    inv_freq = 1.0 / (ROPE_THETA ** (jnp.arange(0, ROPE_AXIS_DIM, 2, dtype=jnp.float32) / ROPE_AXIS_DIM))
    ang = jnp.stack([t_row[:, None] * inv_freq, t_col[:, None] * inv_freq], axis=1)
    return jnp.cos(ang), jnp.sin(ang)


def apply_axial_rope(x, cos, sin):
    xr = x.astype(jnp.float32).reshape(*x.shape[:-1], 2, 2, ROPE_AXIS_DIM // 2)
    x1 = xr[..., 0, :]
    x2 = xr[..., 1, :]
    out = jnp.stack([x1 * cos - x2 * sin, x2 * cos + x1 * sin], axis=-2)
    return out.reshape(x.shape).astype(x.dtype)


def gqa_mixer(u, w_in, g_q, g_k, w_out):
    bsz, L, _ = u.shape
    hd = ATTN_HEAD_DIM
    proj = u @ w_in
    q, k, v = jnp.split(proj, [ATTN_Q_HEADS * hd, (ATTN_Q_HEADS + ATTN_KV_HEADS) * hd], axis=-1)
    q = q.reshape(bsz, L, ATTN_Q_HEADS, hd).transpose(0, 2, 1, 3)
    k = k.reshape(bsz, L, ATTN_KV_HEADS, hd).transpose(0, 2, 1, 3)
    v = v.reshape(bsz, L, ATTN_KV_HEADS, hd).transpose(0, 2, 1, 3)
    cos, sin = axial_rope_tables(L)
    q = apply_axial_rope(rms_norm(q, g_q), cos, sin)
    k = apply_axial_rope(rms_norm(k, g_k), cos, sin)
    nblk = L // ATTN_Q_BLOCK
    qb = q.reshape(bsz, ATTN_KV_HEADS, ATTN_GROUP, nblk, ATTN_Q_BLOCK, hd)
    qb = jnp.moveaxis(qb, 3, 0)
    scale = hd ** -0.5

    def attend(q_blk):
        s = jnp.einsum('bkgqd,bksd->bkgqs', q_blk, k).astype(jnp.float32) * scale
        pr = jax.nn.softmax(s, axis=-1)
        return jnp.einsum('bkgqs,bksd->bkgqd', pr.astype(v.dtype), v)

    o = lax.map(attend, qb)
    o = o.transpose(1, 4, 0, 2, 3, 5)
    o = o.reshape(bsz, L, ATTN_Q_HEADS * hd)
    return o @ w_out


def squared_relu_mlp(u, w_up, w_down):
    hid = jax.nn.relu(u @ w_up)
    return (hid * hid) @ w_down


def setup_inputs(seed: int = 0) -> dict:
    key = jax.random.key(seed)
    ks = jax.random.split(key, 22)
    n_a = (DEPTH + N_MIXERS - 1) // N_MIXERS
    n_b = DEPTH // N_MIXERS

    def nrm(k, shape, scale):
        return scale * jax.random.normal(k, shape, jnp.float32)

    def gain(k, shape):
        return 1.0 + nrm(k, shape, 0.05)

    return {
        'x': nrm(ks[0], (BATCH, SEQ, D_MODEL), 1.0),
        'p': nrm(ks[1], (DEPTH, BATCH, SEQ, PLE_DIM), 1.0),
        'g_pre_mix': gain(ks[2], (DEPTH, D_MODEL)),
        'g_post_mix': gain(ks[3], (DEPTH, D_MODEL)),
        'g_pre_mlp': gain(ks[4], (DEPTH, D_MODEL)),
        'g_post_mlp': gain(ks[5], (DEPTH, D_MODEL)),
        'g_ple': gain(ks[6], (DEPTH, D_MODEL)),
        'w_mlp_up': nrm(ks[7], (DEPTH, D_MODEL, D_FF), D_MODEL ** -0.5),
        'w_mlp_down': nrm(ks[8], (DEPTH, D_FF, D_MODEL), D_FF ** -0.5),
        'w_ple_proj': nrm(ks[9], (DEPTH, PLE_DIM, D_MODEL), PLE_DIM ** -0.5),
        'w_ple_gate': nrm(ks[10], (DEPTH, D_MODEL, D_MODEL), D_MODEL ** -0.5),
        'gla_w_in': nrm(ks[11], (n_a, D_MODEL, GLA_IN), D_MODEL ** -0.5),
        'gla_w_gk_fwd': nrm(ks[12], (n_a, GLA_GATE_RANK, GLA_DK), GLA_GATE_RANK ** -0.5),
        'gla_b_gk_fwd': nrm(ks[13], (n_a, GLA_DK), 0.1),
        'gla_w_gk_bwd': nrm(ks[14], (n_a, GLA_GATE_RANK, GLA_DK), GLA_GATE_RANK ** -0.5),
        'gla_b_gk_bwd': nrm(ks[15], (n_a, GLA_DK), 0.1),
        'gla_g_head': gain(ks[16], (n_a, GLA_HEAD_V)),
        'gla_w_out': nrm(ks[17], (n_a, GLA_DV, D_MODEL), GLA_DV ** -0.5),
        'attn_w_in': nrm(ks[18], (n_b, D_MODEL, ATTN_IN), D_MODEL ** -0.5),
        'attn_g_q': gain(ks[19], (n_b, ATTN_HEAD_DIM)),
        'attn_g_k': gain(ks[20], (n_b, ATTN_HEAD_DIM)),
        'attn_w_out': nrm(ks[21], (n_b, ATTN_Q_HEADS * ATTN_HEAD_DIM, D_MODEL), D_MODEL ** -0.5),
    }


def reference(x, p, g_pre_mix, g_post_mix, g_pre_mlp, g_post_mlp, g_ple, w_mlp_up, w_mlp_down,
              w_ple_proj, w_ple_gate, gla_w_in, gla_w_gk_fwd, gla_b_gk_fwd, gla_w_gk_bwd,
              gla_b_gk_bwd, gla_g_head, gla_w_out, attn_w_in, attn_g_q, attn_g_k, attn_w_out):
    h = x
    for i in range(DEPTH):
        j = i // N_MIXERS
        u = rms_norm(h, g_pre_mix[i])
        if i % N_MIXERS == 0:
            mix = gla_mixer(u, gla_w_in[j], gla_w_gk_fwd[j], gla_b_gk_fwd[j], gla_w_gk_bwd[j],
                            gla_b_gk_bwd[j], gla_g_head[j], gla_w_out[j])
        else:
            mix = gqa_mixer(u, attn_w_in[j], attn_g_q[j], attn_g_k[j], attn_w_out[j])
        h = h + rms_norm(mix, g_post_mix[i])
        m = squared_relu_mlp(rms_norm(h, g_pre_mlp[i]), w_mlp_up[i], w_mlp_down[i])
        h = h + rms_norm(m, g_post_mlp[i])
        gate = jax.nn.sigmoid((h @ w_ple_gate[i]).astype(jnp.float32)).astype(h.dtype)
        e = p[i] @ w_ple_proj[i]
        h = h + rms_norm(gate * e, g_ple[i])
    return h
```

```python
import functools

import numpy as np
import jax
import jax.numpy as jnp
from jax import lax
from jax.experimental import pallas as pl
from jax.experimental.pallas import tpu as pltpu

F32 = jnp.float32
BF16 = jnp.bfloat16

D_MODEL = 2048
SEQ = 8192
DEPTH = 2
GRID_W = 64
PLE_DIM = 256
NORM_EPS = 1e-6
D_FF = 4 * D_MODEL

GLA_HEADS = 4
GLA_DK = D_MODEL // 2
GLA_DV = D_MODEL
GLA_HEAD_K = GLA_DK // GLA_HEADS
GLA_HEAD_V = GLA_DV // GLA_HEADS
GLA_GATE_RANK = 16
GLA_GATE_TAU = 16.0
GLA_CHUNK = 64
GLA_MAIN = 2 * GLA_DK + 2 * GLA_DV
GLA_BLOCK = 256

ATTN_HEAD_DIM = 128
ATTN_Q_HEADS = D_MODEL // ATTN_HEAD_DIM
ATTN_KV_HEADS = 4
ATTN_GROUP = ATTN_Q_HEADS // ATTN_KV_HEADS
ATTN_Q_BLOCK = 128
ATTN_IN = (ATTN_Q_HEADS + 2 * ATTN_KV_HEADS) * ATTN_HEAD_DIM
ROPE_THETA = 10000.0
ROPE_AXIS_DIM = ATTN_HEAD_DIM // 2

LANES = 128
MIB = 1024 * 1024
LOG2E = 1.4426950408889634


def _params(semantics, vmem_mib):
    return pltpu.CompilerParams(dimension_semantics=semantics, vmem_limit_bytes=vmem_mib * MIB)


def _rms(x, gain):
    return x * lax.rsqrt(jnp.mean(x * x, axis=-1, keepdims=True) + NORM_EPS) * gain


def _dot(a, b):
    return jnp.dot(a, b, preferred_element_type=F32)


def _dot_nt(a, b):
    return lax.dot_general(a, b, (((1,), (1,)), ((), ())), preferred_element_type=F32)


def _dot_tn(a, b):
    return lax.dot_general(a, b, (((0,), (0,)), ((), ())), preferred_element_type=F32)


def _norm_matmul_kernel(h_ref, g_ref, w_ref, o_ref, u_ref):
    @pl.when(pl.program_id(1) == 0)
    def _():
        u_ref[...] = _rms(h_ref[...], g_ref[...]).astype(BF16)

    o_ref[...] = _dot(u_ref[...], w_ref[...]).astype(o_ref.dtype)


def norm_matmul(h, g, w, out_dtype, tm=1024, tn=1024):
    m, k = h.shape
    n = w.shape[1]
    tn = min(tn, n)
    return pl.pallas_call(
        _norm_matmul_kernel,
        grid=(m // tm, n // tn),
        in_specs=[
            pl.BlockSpec((tm, k), lambda i, j: (i, 0)),
            pl.BlockSpec((1, k), lambda i, j: (0, 0)),
            pl.BlockSpec((k, tn), lambda i, j: (0, j)),
        ],
        out_specs=pl.BlockSpec((tm, tn), lambda i, j: (i, j)),
        out_shape=jax.ShapeDtypeStruct((m, n), out_dtype),
        scratch_shapes=[pltpu.VMEM((tm, k), BF16)],
        compiler_params=_params(("parallel", "arbitrary"), 48),
        name="norm_matmul",
    )(h, g.reshape(1, k), w)


def _gla_kernel(q_ref, k_ref, v_ref, lr_ref, w_ref, b_ref, tri_ref, o_ref, s_ref, *, reverse):
    c = GLA_CHUNK
    nchunk = GLA_BLOCK // c

    @pl.when(pl.program_id(1) == 0)
    def _():
        s_ref[...] = jnp.zeros_like(s_ref)

    z = _dot(lr_ref[...], w_ref[...]) + b_ref[...]
    la = (jnp.minimum(z, 0.0) - jnp.log1p(jnp.exp(-jnp.abs(z)))) * (1.0 / GLA_GATE_TAU)
    la_hi = la.astype(BF16)
    la_lo = (la - la_hi.astype(F32)).astype(BF16)
    tri = tri_ref[...]
    cum = _dot(tri, la_hi) + _dot(tri, la_lo)

    row = lax.broadcasted_iota(jnp.int32, (c, c), 0)
    col = lax.broadcasted_iota(jnp.int32, (c, c), 1)
    mask = (col > row) if reverse else (col <= row)
    ref_row = c // 2 if reverse else c // 2 - 1
    last_row = 0 if reverse else c - 1

    order = range(nchunk - 1, -1, -1) if reverse else range(nchunk)
    for j in order:
        sl = slice(j * c, (j + 1) * c)
        cumc = cum[sl]
        refc = cumc[ref_row:ref_row + 1]
        lastc = cumc[last_row:last_row + 1]
        qc = q_ref[sl, :] * (GLA_HEAD_K ** -0.5)
        kc = k_ref[sl, :]
        vc = v_ref[sl, :].astype(BF16)
        qt = (qc * jnp.exp(cumc - refc)).astype(BF16)
        kt = (kc * jnp.exp(refc - cumc)).astype(BF16)
        qd = (qc * jnp.exp(cumc)).astype(BF16)
        ke = (kc * jnp.exp(lastc - cumc)).astype(BF16)
        scores = jnp.where(mask, _dot_nt(qt, kt), 0.0).astype(BF16)
        state = s_ref[...]
        o_ref[sl, :] = _dot(scores, vc) + _dot_nt(qd, state.astype(BF16))
        s_ref[...] = state * jnp.exp(lastc) + _dot_tn(vc, ke)


def gla_scan(proj, lr, w_pad, bias, tri, reverse):
    seq = proj.shape[0]
    t = GLA_BLOCK
    nb = seq // t
    blk = (lambda i: nb - 1 - i) if reverse else (lambda i: i)
    kq = GLA_DK // GLA_HEAD_K
    kv = 2 * GLA_DK // GLA_HEAD_V
    return pl.pallas_call(
        functools.partial(_gla_kernel, reverse=reverse),
        grid=(GLA_HEADS, nb),
        in_specs=[
            pl.BlockSpec((t, GLA_HEAD_K), lambda h, i: (blk(i), h)),
            pl.BlockSpec((t, GLA_HEAD_K), lambda h, i: (blk(i), kq + h)),
            pl.BlockSpec((t, GLA_HEAD_V), lambda h, i: (blk(i), kv + h)),
            pl.BlockSpec((t, LANES), lambda h, i: (blk(i), 0)),
            pl.BlockSpec((LANES, GLA_HEAD_K), lambda h, i: (0, h)),
            pl.BlockSpec((1, GLA_HEAD_K), lambda h, i: (0, h)),
            pl.BlockSpec((t, t), lambda h, i: (0, 0)),
        ],
        out_specs=pl.BlockSpec((t, GLA_HEAD_V), lambda h, i: (blk(i), h)),
        out_shape=jax.ShapeDtypeStruct((seq, GLA_DV), F32),
        scratch_shapes=[pltpu.VMEM((GLA_HEAD_V, GLA_HEAD_K), F32)],
        compiler_params=_params(("parallel", "arbitrary"), 32),
        name="gla_bwd" if reverse else "gla_fwd",
    )(proj, proj, proj, lr, w_pad, bias.reshape(1, GLA_DK), tri)


def _gla_post_kernel(of_ref, ob_ref, og_ref, gh_ref, w_ref, h_ref, gp_ref, o_ref, y_ref):
    for hd in range(GLA_HEADS):
        sl = slice(hd * GLA_HEAD_V, (hd + 1) * GLA_HEAD_V)
        o = _rms(of_ref[:, sl] + ob_ref[:, sl], gh_ref[...])
        og = og_ref[:, sl]
        y_ref[:, sl] = (o * (og / (1.0 + jnp.exp(-og)))).astype(BF16)
    mix = _dot(y_ref[...], w_ref[...])
    o_ref[...] = h_ref[...] + _rms(mix, gp_ref[...])


def gla_post(o_f, o_b, proj, g_head, w_out, h, g_post, tm=256):
    m, d = h.shape
    og_blk = (2 * GLA_DK + GLA_DV) // GLA_DV
    row = lambda i: (i, 0)
    const = lambda i: (0, 0)
    return pl.pallas_call(
        _gla_post_kernel,
        grid=(m // tm,),
        in_specs=[
            pl.BlockSpec((tm, GLA_DV), row),
            pl.BlockSpec((tm, GLA_DV), row),
            pl.BlockSpec((tm, GLA_DV), lambda i: (i, og_blk)),
            pl.BlockSpec((1, GLA_HEAD_V), const),
            pl.BlockSpec((GLA_DV, d), const),
            pl.BlockSpec((tm, d), row),
            pl.BlockSpec((1, d), const),
        ],
        out_specs=pl.BlockSpec((tm, d), row),
        out_shape=jax.ShapeDtypeStruct((m, d), F32),
        scratch_shapes=[pltpu.VMEM((tm, GLA_DV), BF16)],
        compiler_params=_params(("parallel",), 48),
        name="gla_post",
    )(o_f, o_b, proj, g_head.reshape(1, GLA_HEAD_V), w_out, h, g_post.reshape(1, d))


def _matmul_norm_res_kernel(a_ref, w_ref, h_ref, g_ref, o_ref):
    o_ref[...] = h_ref[...] + _rms(_dot(a_ref[...], w_ref[...]), g_ref[...])


def matmul_norm_res(a, w, h, g, tm=512):
    m, d = h.shape
    k = a.shape[1]
    row = lambda i: (i, 0)
    const = lambda i: (0, 0)
    return pl.pallas_call(
        _matmul_norm_res_kernel,
        grid=(m // tm,),
        in_specs=[
            pl.BlockSpec((tm, k), row),
            pl.BlockSpec((k, d), const),
            pl.BlockSpec((tm, d), row),
            pl.BlockSpec((1, d), const),
        ],
        out_specs=pl.BlockSpec((tm, d), row),
        out_shape=jax.ShapeDtypeStruct((m, d), F32),
        compiler_params=_params(("parallel",), 48),
        name="matmul_norm_res",
    )(a, w, h, g.reshape(1, d))


def _mlp_kernel(h_ref, g1_ref, wu_ref, wd_ref, g2_ref, o_ref, u_ref, acc_ref):
    f = pl.program_id(1)

    @pl.when(f == 0)
    def _():
        u_ref[...] = _rms(h_ref[...], g1_ref[...]).astype(BF16)

    hid = jnp.maximum(_dot(u_ref[...], wu_ref[...]), 0.0)
    part = _dot((hid * hid).astype(BF16), wd_ref[...])

    @pl.when(f == 0)
    def _():
        acc_ref[...] = part

    @pl.when(f > 0)
    def _():
        acc_ref[...] += part

    @pl.when(f == pl.num_programs(1) - 1)
    def _():
        o_ref[...] = h_ref[...] + _rms(acc_ref[...], g2_ref[...])


def mlp_block(h, g_pre, w_up, w_down, g_post, tm=512, tf=512):
    m, d = h.shape
    ff = w_up.shape[1]
    return pl.pallas_call(
        _mlp_kernel,
        grid=(m // tm, ff // tf),
        in_specs=[
            pl.BlockSpec((tm, d), lambda i, f: (i, 0)),
            pl.BlockSpec((1, d), lambda i, f: (0, 0)),
            pl.BlockSpec((d, tf), lambda i, f: (0, f)),
            pl.BlockSpec((tf, d), lambda i, f: (f, 0)),
            pl.BlockSpec((1, d), lambda i, f: (0, 0)),
        ],
        out_specs=pl.BlockSpec((tm, d), lambda i, f: (i, 0)),
        out_shape=jax.ShapeDtypeStruct((m, d), F32),
        scratch_shapes=[pltpu.VMEM((tm, d), BF16), pltpu.VMEM((tm, d), F32)],
        compiler_params=_params(("parallel", "arbitrary"), 48),
        name="mlp_block",
    )(h, g_pre.reshape(1, d), w_up, w_down, g_post.reshape(1, d))


def _ple_kernel(h_ref, p_ref, wg_ref, wp_ref, g_ref, o_ref):
    x = h_ref[...]
    gate = 1.0 / (1.0 + jnp.exp(-_dot(x.astype(BF16), wg_ref[...])))
    e = _dot(p_ref[...].astype(BF16), wp_ref[...])
    o_ref[...] = x + _rms(gate * e, g_ref[...])


def ple_block(h, p, w_gate, w_proj, g, tm=512):
    m, d = h.shape
    pd = p.shape[1]
    row = lambda i: (i, 0)
    const = lambda i: (0, 0)
    return pl.pallas_call(
        _ple_kernel,
        grid=(m // tm,),
        in_specs=[
            pl.BlockSpec((tm, d), row),
            pl.BlockSpec((tm, pd), row),
            pl.BlockSpec((d, d), const),
            pl.BlockSpec((pd, d), const),
            pl.BlockSpec((1, d), const),
        ],
        out_specs=pl.BlockSpec((tm, d), row),
        out_shape=jax.ShapeDtypeStruct((m, d), F32),
        compiler_params=_params(("parallel",), 48),
        name="ple_block",
    )(h, p, w_gate, w_proj, g.reshape(1, d))


def _rope_tables(seq):
    rows = seq // GRID_W
    t_row = jnp.repeat(jnp.arange(rows, dtype=jnp.int32), GRID_W).astype(F32)
    t_col = jnp.tile(jnp.arange(GRID_W, dtype=jnp.int32), rows).astype(F32)
    inv_freq = 1.0 / (ROPE_THETA ** (jnp.arange(0, ROPE_AXIS_DIM, 2, dtype=F32) / ROPE_AXIS_DIM))
    a_row = t_row[:, None] * inv_freq
    a_col = t_col[:, None] * inv_freq
    ang = jnp.concatenate([a_row, a_row, a_col, a_col], axis=1)
    half = ROPE_AXIS_DIM // 2
    sign = np.where((np.arange(ATTN_HEAD_DIM) % ROPE_AXIS_DIM) < half, -1.0, 1.0).astype(np.float32)
    return jnp.cos(ang), jnp.sin(ang) * sign


def _qk_prep_kernel(x_ref, cos_ref, sin_ref, gq_ref, gk_ref, q_ref, k_ref, v_ref):
    cos = cos_ref[...]
    sin = sin_ref[...]
    lane = lax.broadcasted_iota(jnp.int32, cos.shape, 1)
    first_half = (lane % ROPE_AXIS_DIM) < (ROPE_AXIS_DIM // 2)
    hd = ATTN_HEAD_DIM

    def norm_rope(x, gain):
        y = _rms(x, gain)
        partner = jnp.where(first_half, pltpu.roll(y, hd - ROPE_AXIS_DIM // 2, 1),
                            pltpu.roll(y, ROPE_AXIS_DIM // 2, 1))
        return y * cos + partner * sin

    q_scale = (hd ** -0.5) * LOG2E
    for i in range(ATTN_Q_HEADS):
        sl = slice(i * hd, (i + 1) * hd)
        q_ref[:, sl] = (norm_rope(x_ref[:, sl], gq_ref[...]) * q_scale).astype(BF16)
    for i in range(ATTN_KV_HEADS):
        src = slice((ATTN_Q_HEADS + i) * hd, (ATTN_Q_HEADS + i + 1) * hd)
        k_ref[:, i * hd:(i + 1) * hd] = norm_rope(x_ref[:, src], gk_ref[...]).astype(BF16)
    v0 = (ATTN_Q_HEADS + ATTN_KV_HEADS) * hd
    v_ref[...] = x_ref[:, v0:v0 + ATTN_KV_HEADS * hd].astype(BF16)


def qk_prep(proj, g_q, g_k, tm=512):
    seq = proj.shape[0]
    hd = ATTN_HEAD_DIM
    cos, sin = _rope_tables(seq)
    row = lambda i: (i, 0)
    const = lambda i: (0, 0)
    return pl.pallas_call(
        _qk_prep_kernel,
        grid=(seq // tm,),
        in_specs=[
            pl.BlockSpec((tm, ATTN_IN), row),
            pl.BlockSpec((tm, hd), row),
            pl.BlockSpec((tm, hd), row),
            pl.BlockSpec((1, hd), const),
            pl.BlockSpec((1, hd), const),
        ],
        out_specs=[
            pl.BlockSpec((tm, ATTN_Q_HEADS * hd), row),
            pl.BlockSpec((tm, ATTN_KV_HEADS * hd), row),
            pl.BlockSpec((tm, ATTN_KV_HEADS * hd), row),
        ],
        out_shape=[
            jax.ShapeDtypeStruct((seq, ATTN_Q_HEADS * hd), BF16),
            jax.ShapeDtypeStruct((seq, ATTN_KV_HEADS * hd), BF16),
            jax.ShapeDtypeStruct((seq, ATTN_KV_HEADS * hd), BF16),
        ],
        compiler_params=_params(("parallel",), 48),
        name="qk_prep",
    )(proj, cos, sin, g_q.reshape(1, hd), g_k.reshape(1, hd))


def _flash_kernel(q_ref, k_ref, v_ref, o_ref, q4_ref, m_ref, l_ref, acc_ref, *, tq, tk):
    hd = ATTN_HEAD_DIM
    for i in range(ATTN_GROUP):
        q4_ref[i * tq:(i + 1) * tq, :] = q_ref[:, i * hd:(i + 1) * hd]
    m_ref[...] = jnp.full_like(m_ref, -jnp.inf)
    l_ref[...] = jnp.zeros_like(l_ref)
    acc_ref[...] = jnp.zeros_like(acc_ref)

    def body(j, carry):
        start = pl.multiple_of(j * tk, tk)
        kb = k_ref[pl.ds(start, tk), :]
        vb = v_ref[pl.ds(start, tk), :]
        s = _dot_nt(q4_ref[...], kb)
        m_prev = m_ref[...]
        m_next = jnp.maximum(m_prev, jnp.max(s, axis=1, keepdims=True))
        alpha = jnp.exp2(m_prev - m_next)
        p = jnp.exp2(s - pltpu.repeat(m_next, tk // LANES, axis=1))
        l_ref[...] = alpha * l_ref[...] + jnp.sum(p, axis=1, keepdims=True)
        acc_ref[...] = alpha * acc_ref[...] + _dot(p.astype(BF16), vb)
        m_ref[...] = m_next
        return carry

    lax.fori_loop(0, k_ref.shape[0] // tk, body, 0)
    o = acc_ref[...] / l_ref[...]
    for i in range(ATTN_GROUP):
        o_ref[:, i * hd:(i + 1) * hd] = o[i * tq:(i + 1) * tq].astype(o_ref.dtype)


def flash_attention(q, k, v, tq=256, tk=512):
    seq = q.shape[0]
    hd = ATTN_HEAD_DIM
    gw = ATTN_GROUP * hd
    return pl.pallas_call(
        functools.partial(_flash_kernel, tq=tq, tk=tk),
        grid=(ATTN_KV_HEADS, seq // tq),
        in_specs=[
            pl.BlockSpec((tq, gw), lambda g, i: (i, g)),
            pl.BlockSpec((seq, hd), lambda g, i: (0, g)),
            pl.BlockSpec((seq, hd), lambda g, i: (0, g)),
        ],
        out_specs=pl.BlockSpec((tq, gw), lambda g, i: (i, g)),
        out_shape=jax.ShapeDtypeStruct((seq, ATTN_Q_HEADS * hd), BF16),
        scratch_shapes=[
            pltpu.VMEM((ATTN_GROUP * tq, hd), BF16),
            pltpu.VMEM((ATTN_GROUP * tq, LANES), F32),
            pltpu.VMEM((ATTN_GROUP * tq, LANES), F32),
            pltpu.VMEM((ATTN_GROUP * tq, hd), F32),
        ],
        compiler_params=_params(("parallel", "arbitrary"), 48),
        name="flash_attention",
    )(q, k, v)


def _chunk_tri(reverse):
    idx = np.arange(GLA_BLOCK)
    same = (idx[:, None] // GLA_CHUNK) == (idx[None, :] // GLA_CHUNK)
    tri = (idx[None, :] >= idx[:, None]) if reverse else (idx[None, :] <= idx[:, None])
    return jnp.asarray((same & tri).astype(np.float32), dtype=BF16)


def _gate_weights(w_gk, first_row):
    w = jnp.zeros((LANES, GLA_DK), F32).at[first_row:first_row + GLA_GATE_RANK].set(w_gk)
    return w.astype(BF16)


def gla_layer(h, g_pre, w_in, w_gk_f, b_gk_f, w_gk_b, b_gk_b, g_head, w_out, g_post):
    w_main = w_in[:, :GLA_MAIN].astype(BF16)
    w_lr = jnp.pad(w_in[:, GLA_MAIN:], ((0, 0), (0, LANES - 2 * GLA_GATE_RANK))).astype(BF16)
    proj = norm_matmul(h, g_pre, w_main, F32)
    lr = norm_matmul(h, g_pre, w_lr, BF16)
    o_f = gla_scan(proj, lr, _gate_weights(w_gk_f, 0), b_gk_f, _chunk_tri(False), reverse=False)
    o_b = gla_scan(proj, lr, _gate_weights(w_gk_b, GLA_GATE_RANK), b_gk_b, _chunk_tri(True), reverse=True)
    return gla_post(o_f, o_b, proj, g_head, w_out.astype(BF16), h, g_post)


def gqa_layer(h, g_pre, w_in, g_q, g_k, w_out, g_post):
    proj = norm_matmul(h, g_pre, w_in.astype(BF16), F32)
    q, k, v = qk_prep(proj, g_q, g_k)
    o = flash_attention(q, k, v)
    seq, d = o.shape
    nblk = seq // ATTN_Q_BLOCK
    o = o.reshape(nblk, ATTN_Q_BLOCK, d).transpose(1, 0, 2).reshape(seq, d)
    return matmul_norm_res(o, w_out.astype(BF16), h, g_post)


def kernel(x, p, g_pre_mix, g_post_mix, g_pre_mlp, g_post_mlp, g_ple, w_mlp_up, w_mlp_down,
           w_ple_proj, w_ple_gate, gla_w_in, gla_w_gk_fwd, gla_b_gk_fwd, gla_w_gk_bwd,
           gla_b_gk_bwd, gla_g_head, gla_w_out, attn_w_in, attn_g_q, attn_g_k, attn_w_out):
    assert x.shape == (1, SEQ, D_MODEL)
    h = x[0]
    for i in range(DEPTH):
        j = i // 2
        if i % 2 == 0:
            h = gla_layer(h, g_pre_mix[i], gla_w_in[j], gla_w_gk_fwd[j], gla_b_gk_fwd[j],
                          gla_w_gk_bwd[j], gla_b_gk_bwd[j], gla_g_head[j], gla_w_out[j], g_post_mix[i])
        else:
            h = gqa_layer(h, g_pre_mix[i], attn_w_in[j], attn_g_q[j], attn_g_k[j], attn_w_out[j],
                          g_post_mix[i])
        h = mlp_block(h, g_pre_mlp[i], w_mlp_up[i].astype(BF16), w_mlp_down[i].astype(BF16), g_post_mlp[i])
        h = ple_block(h, p[i, 0], w_ple_gate[i].astype(BF16), w_ple_proj[i].astype(BF16), g_ple[i])
    return h[None]
```

```python
import functools

import numpy as np
import jax
import jax.numpy as jnp
from jax import lax
from jax.experimental import pallas as pl
from jax.experimental.pallas import tpu as pltpu

F32 = jnp.float32
BF16 = jnp.bfloat16

D_MODEL = 2048
SEQ = 8192
DEPTH = 2
GRID_W = 64
PLE_DIM = 256
NORM_EPS = 1e-6
D_FF = 4 * D_MODEL

GLA_HEADS = 4
GLA_DK = D_MODEL // 2
GLA_DV = D_MODEL
GLA_HEAD_K = GLA_DK // GLA_HEADS
GLA_HEAD_V = GLA_DV // GLA_HEADS
GLA_GATE_RANK = 16
GLA_GATE_TAU = 16.0
GLA_CHUNK = 64
GLA_MAIN = 2 * GLA_DK + 2 * GLA_DV
GLA_BLOCK = 256

ATTN_HEAD_DIM = 128
ATTN_Q_HEADS = D_MODEL // ATTN_HEAD_DIM
ATTN_KV_HEADS = 4
ATTN_GROUP = ATTN_Q_HEADS // ATTN_KV_HEADS
ATTN_Q_BLOCK = 128
ATTN_KV_TILE = 1024
ATTN_V_ROWS = ATTN_HEAD_DIM + 16
ATTN_QK = (ATTN_Q_HEADS + ATTN_KV_HEADS) * ATTN_HEAD_DIM
ROPE_THETA = 10000.0
ROPE_AXIS_DIM = ATTN_HEAD_DIM // 2

LANES = 128
MIB = 1024 * 1024
LOG2E = 1.4426950408889634


def _params(semantics, vmem_mib):
    return pltpu.CompilerParams(dimension_semantics=semantics, vmem_limit_bytes=vmem_mib * MIB)


def _rms(x, gain):
    return x * lax.rsqrt(jnp.mean(x * x, axis=-1, keepdims=True) + NORM_EPS) * gain


def _dot(a, b):
    return jnp.dot(a, b, preferred_element_type=F32)


def _dot_nt(a, b):
    return lax.dot_general(a, b, (((1,), (1,)), ((), ())), preferred_element_type=F32)


def _dot_tn(a, b):
    return lax.dot_general(a, b, (((0,), (0,)), ((), ())), preferred_element_type=F32)


def _norm_matmul_kernel(h_ref, g_ref, w_ref, wx_ref, o_ref, ox_ref, u_ref):
    @pl.when(pl.program_id(1) == 0)
    def _():
        u_ref[...] = _rms(h_ref[...], g_ref[...]).astype(BF16)
        ox_ref[...] = _dot(u_ref[...], wx_ref[...]).astype(ox_ref.dtype)

    o_ref[...] = _dot(u_ref[...], w_ref[...]).astype(o_ref.dtype)


def norm_matmul(h, g, w, w_extra, tm=1024, tn=1024):
    m, k = h.shape
    n = w.shape[1]
    nx = w_extra.shape[1]
    return pl.pallas_call(
        _norm_matmul_kernel,
        grid=(m // tm, n // tn),
        in_specs=[
            pl.BlockSpec((tm, k), lambda i, j: (i, 0)),
            pl.BlockSpec((1, k), lambda i, j: (0, 0)),
            pl.BlockSpec((k, tn), lambda i, j: (0, j)),
            pl.BlockSpec((k, nx), lambda i, j: (0, 0)),
        ],
        out_specs=[
            pl.BlockSpec((tm, tn), lambda i, j: (i, j)),
            pl.BlockSpec((tm, nx), lambda i, j: (i, 0)),
        ],
        out_shape=[jax.ShapeDtypeStruct((m, n), F32), jax.ShapeDtypeStruct((m, nx), BF16)],
        scratch_shapes=[pltpu.VMEM((tm, k), BF16)],
        compiler_params=_params(("parallel", "arbitrary"), 48),
        name="norm_matmul",
    )(h, g.reshape(1, k), w, w_extra)


def _gla_kernel(qf_ref, kf_ref, vf_ref, lrf_ref, qb_ref, kb_ref, vb_ref, lrb_ref, wf_ref, bf_ref, wb_ref, bb_ref,
                trif_ref, trib_ref, of_ref, ob_ref, s_ref, cum_ref):
    c = GLA_CHUNK
    nchunk = GLA_BLOCK // c

    @pl.when(pl.program_id(0) == 0)
    def _():
        s_ref[...] = jnp.zeros_like(s_ref)

    def cumulative(d, lr_ref, w_ref, b_ref, tri_ref):
        z = _dot(lr_ref[...], w_ref[...]) + b_ref[...]
        la = (jnp.minimum(z, 0.0) - jnp.log1p(jnp.exp(-jnp.abs(z)))) * (1.0 / GLA_GATE_TAU)
        la_hi = la.astype(BF16)
        la_lo = (la - la_hi.astype(F32)).astype(BF16)
        tri = tri_ref[...]
        cum_ref[d] = _dot(tri, la_hi) + _dot(tri, la_lo)

    cumulative(0, lrf_ref, wf_ref, bf_ref, trif_ref)
    cumulative(1, lrb_ref, wb_ref, bb_ref, trib_ref)

    row = lax.broadcasted_iota(jnp.int32, (c, c), 0)
    col = lax.broadcasted_iota(jnp.int32, (c, c), 1)
    directions = (
        (qf_ref, kf_ref, vf_ref, of_ref, col <= row, c // 2 - 1, c - 1),
        (qb_ref, kb_ref, vb_ref, ob_ref, col > row, c // 2, 0),
    )
    for step in range(nchunk):
        for d, (q_ref, k_ref, v_ref, o_ref, mask, ref_row, last_row) in enumerate(directions):
            j = nchunk - 1 - step if d else step
            sl = slice(j * c, (j + 1) * c)
            for h in range(GLA_HEADS):
                hk = slice(h * GLA_HEAD_K, (h + 1) * GLA_HEAD_K)
                hv = slice(h * GLA_HEAD_V, (h + 1) * GLA_HEAD_V)
                cumc = cum_ref[d, sl, hk]
                refc = cumc[ref_row:ref_row + 1]
                lastc = cumc[last_row:last_row + 1]
                qc = q_ref[sl, hk] * (GLA_HEAD_K ** -0.5)
                kc = k_ref[sl, hk]
                vc = v_ref[sl, hv].astype(BF16)
                qt = (qc * jnp.exp(cumc - refc)).astype(BF16)
                kt = (kc * jnp.exp(refc - cumc)).astype(BF16)
                qd = (qc * jnp.exp(cumc)).astype(BF16)
                ke = (kc * jnp.exp(lastc - cumc)).astype(BF16)
                scores = jnp.where(mask, _dot_nt(qt, kt), 0.0).astype(BF16)
                state = s_ref[d * GLA_HEADS + h]
                o_ref[sl, hv] = _dot(scores, vc) + _dot_nt(qd, state.astype(BF16))
                s_ref[d * GLA_HEADS + h] = state * jnp.exp(lastc) + _dot_tn(vc, ke)


def gla_scan(proj, lr, w_f, b_f, w_b, b_b):
    seq = proj.shape[0]
    t = GLA_BLOCK
    nb = seq // t
    fwd = lambda i: i
    bwd = lambda i: nb - 1 - i
    const = lambda i: (0, 0)
    v_blk = 2 * GLA_DK // GLA_DV

    def operands(blk):
        return [
            pl.BlockSpec((t, GLA_DK), lambda i: (blk(i), 0)),
            pl.BlockSpec((t, GLA_DK), lambda i: (blk(i), 1)),
            pl.BlockSpec((t, GLA_DV), lambda i: (blk(i), v_blk)),
            pl.BlockSpec((t, LANES), lambda i: (blk(i), 0)),
        ]

    return pl.pallas_call(
        _gla_kernel,
        grid=(nb,),
        in_specs=operands(fwd) + operands(bwd) + [
            pl.BlockSpec((LANES, GLA_DK), const),
            pl.BlockSpec((1, GLA_DK), const),
            pl.BlockSpec((LANES, GLA_DK), const),
            pl.BlockSpec((1, GLA_DK), const),
            pl.BlockSpec((t, t), const),
            pl.BlockSpec((t, t), const),
        ],
        out_specs=[
            pl.BlockSpec((t, GLA_DV), lambda i: (fwd(i), 0)),
            pl.BlockSpec((t, GLA_DV), lambda i: (bwd(i), 0)),
        ],
        out_shape=[jax.ShapeDtypeStruct((seq, GLA_DV), F32)] * 2,
        scratch_shapes=[
            pltpu.VMEM((2 * GLA_HEADS, GLA_HEAD_V, GLA_HEAD_K), F32),
            pltpu.VMEM((2, t, GLA_DK), F32),
        ],
        compiler_params=_params(("arbitrary",), 48),
        name="gla_scan",
    )(proj, proj, proj, lr, proj, proj, proj, lr,
      _gate_weights(w_f, 0), b_f.reshape(1, GLA_DK), _gate_weights(w_b, GLA_GATE_RANK), b_b.reshape(1, GLA_DK),
      _chunk_tri(False), _chunk_tri(True))


def _gla_post_kernel(of_ref, ob_ref, og_ref, gh_ref, w_ref, h_ref, gp_ref, o_ref, y_ref):
    for hd in range(GLA_HEADS):
        sl = slice(hd * GLA_HEAD_V, (hd + 1) * GLA_HEAD_V)
        o = _rms(of_ref[:, sl] + ob_ref[:, sl], gh_ref[...])
        og = og_ref[:, sl]
        y_ref[:, sl] = (o * (og / (1.0 + jnp.exp(-og)))).astype(BF16)
    mix = _dot(y_ref[...], w_ref[...])
    o_ref[...] = h_ref[...] + _rms(mix, gp_ref[...])


def gla_post(o_f, o_b, proj, g_head, w_out, h, g_post, tm=256):
    m, d = h.shape
    og_blk = (2 * GLA_DK + GLA_DV) // GLA_DV
    row = lambda i: (i, 0)
    const = lambda i: (0, 0)
    return pl.pallas_call(
        _gla_post_kernel,
        grid=(m // tm,),
        in_specs=[
            pl.BlockSpec((tm, GLA_DV), row),
            pl.BlockSpec((tm, GLA_DV), row),
            pl.BlockSpec((tm, GLA_DV), lambda i: (i, og_blk)),
            pl.BlockSpec((1, GLA_HEAD_V), const),
            pl.BlockSpec((GLA_DV, d), const),
            pl.BlockSpec((tm, d), row),
            pl.BlockSpec((1, d), const),
        ],
        out_specs=pl.BlockSpec((tm, d), row),
        out_shape=jax.ShapeDtypeStruct((m, d), F32),
        scratch_shapes=[pltpu.VMEM((tm, GLA_DV), BF16)],
        compiler_params=_params(("parallel",), 48),
        name="gla_post",
    )(o_f, o_b, proj, g_head.reshape(1, GLA_HEAD_V), w_out, h, g_post.reshape(1, d))


def _matmul_norm_res_kernel(a_ref, w_ref, h_ref, g_ref, o_ref):
    o_ref[...] = h_ref[...] + _rms(_dot(a_ref[...], w_ref[...]), g_ref[...])


def matmul_norm_res(a, w, h, g, tm=512):
    m, d = h.shape
    k = a.shape[1]
    row = lambda i: (i, 0)
    const = lambda i: (0, 0)
    return pl.pallas_call(
        _matmul_norm_res_kernel,
        grid=(m // tm,),
        in_specs=[
            pl.BlockSpec((tm, k), row),
            pl.BlockSpec((k, d), const),
            pl.BlockSpec((tm, d), row),
            pl.BlockSpec((1, d), const),
        ],
        out_specs=pl.BlockSpec((tm, d), row),
        out_shape=jax.ShapeDtypeStruct((m, d), F32),
        compiler_params=_params(("parallel",), 48),
        name="matmul_norm_res",
    )(a, w, h, g.reshape(1, d))


def _mlp_kernel(h_ref, g1_ref, wu_ref, wd_ref, g2_ref, o_ref, u_ref, acc_ref):
    f = pl.program_id(1)

    @pl.when(f == 0)
    def _():
        u_ref[...] = _rms(h_ref[...], g1_ref[...]).astype(BF16)

    hid = jnp.maximum(_dot(u_ref[...], wu_ref[...]), 0.0)
    part = _dot((hid * hid).astype(BF16), wd_ref[...])

    @pl.when(f == 0)
    def _():
        acc_ref[...] = part

    @pl.when(f > 0)
    def _():
        acc_ref[...] += part

    @pl.when(f == pl.num_programs(1) - 1)
    def _():
        o_ref[...] = h_ref[...] + _rms(acc_ref[...], g2_ref[...])


def mlp_block(h, g_pre, w_up, w_down, g_post, layer, tm=512, tf=512):
    m, d = h.shape
    ff = w_up.shape[2]
    return pl.pallas_call(
        _mlp_kernel,
        grid=(m // tm, ff // tf),
        in_specs=[
            pl.BlockSpec((tm, d), lambda i, f: (i, 0)),
            pl.BlockSpec((1, d), lambda i, f: (0, 0)),
            pl.BlockSpec((None, d, tf), lambda i, f: (layer, 0, f)),
            pl.BlockSpec((None, tf, d), lambda i, f: (layer, f, 0)),
            pl.BlockSpec((1, d), lambda i, f: (0, 0)),
        ],
        out_specs=pl.BlockSpec((tm, d), lambda i, f: (i, 0)),
        out_shape=jax.ShapeDtypeStruct((m, d), F32),
        scratch_shapes=[pltpu.VMEM((tm, d), BF16), pltpu.VMEM((tm, d), F32)],
        compiler_params=_params(("parallel", "arbitrary"), 48),
        name="mlp_block",
    )(h, g_pre.reshape(1, d), w_up, w_down, g_post.reshape(1, d))


def _ple_kernel(h_ref, p_ref, wg_ref, wp_ref, g_ref, o_ref):
    x = h_ref[...]
    gate = 1.0 / (1.0 + jnp.exp(-_dot(x.astype(BF16), wg_ref[...])))
    e = _dot(p_ref[...].astype(BF16), wp_ref[...])
    o_ref[...] = x + _rms(gate * e, g_ref[...])


def ple_block(h, p, w_gate, w_proj, g, layer, tm=512):
    m, d = h.shape
    pd = p.shape[-1]
    row = lambda i: (i, 0)
    const = lambda i: (0, 0)
    return pl.pallas_call(
        _ple_kernel,
        grid=(m // tm,),
        in_specs=[
            pl.BlockSpec((tm, d), row),
            pl.BlockSpec((None, None, tm, pd), lambda i: (layer, 0, i, 0)),
            pl.BlockSpec((None, d, d), lambda i: (layer, 0, 0)),
            pl.BlockSpec((None, pd, d), lambda i: (layer, 0, 0)),
            pl.BlockSpec((1, d), const),
        ],
        out_specs=pl.BlockSpec((tm, d), row),
        out_shape=jax.ShapeDtypeStruct((m, d), F32),
        compiler_params=_params(("parallel",), 48),
        name="ple_block",
    )(h, p, w_gate, w_proj, g.reshape(1, d))


def _rope_tables(seq):
    rows = seq // GRID_W
    t_row = np.repeat(np.arange(rows), GRID_W).astype(np.float64)
    t_col = np.tile(np.arange(GRID_W), rows).astype(np.float64)
    inv_freq = 1.0 / (ROPE_THETA ** (np.arange(0, ROPE_AXIS_DIM, 2, dtype=np.float64) / ROPE_AXIS_DIM))
    a_row = t_row[:, None] * inv_freq
    a_col = t_col[:, None] * inv_freq
    ang = np.concatenate([a_row, a_row, a_col, a_col], axis=1)
    half = ROPE_AXIS_DIM // 2
    sign = np.where((np.arange(ATTN_HEAD_DIM) % ROPE_AXIS_DIM) < half, -1.0, 1.0)
    return jnp.asarray(np.cos(ang), F32), jnp.asarray(np.sin(ang) * sign, F32)


def _qk_prep_kernel(x_ref, v_ref, cos_ref, sin_ref, gq_ref, gk_ref, qt_ref, k_ref, vt_ref):
    cos = cos_ref[...]
    sin = sin_ref[...]
    lane = lax.broadcasted_iota(jnp.int32, cos.shape, 1)
    first_half = (lane % ROPE_AXIS_DIM) < (ROPE_AXIS_DIM // 2)
    hd = ATTN_HEAD_DIM

    def norm_rope(x, gain):
        y = _rms(x, gain)
        partner = jnp.where(first_half, pltpu.roll(y, hd - ROPE_AXIS_DIM // 2, 1),
                            pltpu.roll(y, ROPE_AXIS_DIM // 2, 1))
        return y * cos + partner * sin

    q_scale = (hd ** -0.5) * LOG2E
    for i in range(ATTN_Q_HEADS):
        sl = slice(i * hd, (i + 1) * hd)
        qt_ref[sl, :] = (norm_rope(x_ref[:, sl], gq_ref[...]) * q_scale).T.astype(BF16)
    for i in range(ATTN_KV_HEADS):
        src = slice((ATTN_Q_HEADS + i) * hd, (ATTN_Q_HEADS + i + 1) * hd)
        k_ref[:, i * hd:(i + 1) * hd] = norm_rope(x_ref[:, src], gk_ref[...]).astype(BF16)
        vt_ref[i, :hd, :] = v_ref[:, i * hd:(i + 1) * hd].astype(F32).T.astype(BF16)
        vt_ref[i, hd:, :] = jnp.ones((ATTN_V_ROWS - hd, x_ref.shape[0]), BF16)


def qk_prep(proj_qk, v, g_q, g_k, tm=512):
    seq = proj_qk.shape[0]
    hd = ATTN_HEAD_DIM
    cos, sin = _rope_tables(seq)
    row = lambda i: (i, 0)
    const = lambda i: (0, 0)
    return pl.pallas_call(
        _qk_prep_kernel,
        grid=(seq // tm,),
        in_specs=[
            pl.BlockSpec((tm, ATTN_QK), row),
            pl.BlockSpec((tm, ATTN_KV_HEADS * hd), row),
            pl.BlockSpec((tm, hd), row),
            pl.BlockSpec((tm, hd), row),
            pl.BlockSpec((1, hd), const),
            pl.BlockSpec((1, hd), const),
        ],
        out_specs=[
            pl.BlockSpec((ATTN_Q_HEADS * hd, tm), lambda i: (0, i)),
            pl.BlockSpec((tm, ATTN_KV_HEADS * hd), row),
            pl.BlockSpec((ATTN_KV_HEADS, ATTN_V_ROWS, tm), lambda i: (0, 0, i)),
        ],
        out_shape=[
            jax.ShapeDtypeStruct((ATTN_Q_HEADS * hd, seq), BF16),
            jax.ShapeDtypeStruct((seq, ATTN_KV_HEADS * hd), BF16),
            jax.ShapeDtypeStruct((ATTN_KV_HEADS, ATTN_V_ROWS, seq), BF16),
        ],
        compiler_params=_params(("parallel",), 48),
        name="qk_prep",
    )(proj_qk, v, cos, sin, g_q.reshape(1, hd), g_k.reshape(1, hd))


def _flash_kernel(qt_ref, k_ref, vt_ref, o_ref, st_ref, m_ref, acc_ref):
    hd = ATTN_HEAD_DIM
    tk = ATTN_KV_TILE
    qt = jnp.concatenate([qt_ref[i * hd:(i + 1) * hd, :] for i in range(ATTN_GROUP)], axis=1)
    m_ref[...] = jnp.full_like(m_ref, -jnp.inf)
    acc_ref[...] = jnp.zeros_like(acc_ref)

    def scores(j, slot):
        start = pl.multiple_of(j * tk, tk)
        st_ref[slot] = _dot(k_ref[pl.ds(start, tk), :], qt)

    def consume(j, slot):
        start = pl.multiple_of(j * tk, tk)
        st = st_ref[slot]
        m_prev = m_ref[...]
        m_next = jnp.maximum(m_prev, jnp.max(st, axis=0, keepdims=True))
        alpha = jnp.exp2(m_prev - m_next)
        pt = jnp.exp2(st - m_next)
        acc_ref[...] = alpha * acc_ref[...] + _dot(vt_ref[:, pl.ds(start, tk)], pt.astype(BF16))
        m_ref[...] = m_next

    def pair(j, last):
        scores(j + 1, 1)
        consume(j, 0)
        if not last:
            scores(j + 2, 0)
        consume(j + 1, 1)

    npair = k_ref.shape[0] // (2 * tk)
    scores(0, 0)

    def body(jj, carry):
        pair(2 * jj, last=False)
        return carry

    lax.fori_loop(0, npair - 1, body, 0)
    pair(2 * (npair - 1), last=True)
    ot = acc_ref[:hd, :] / acc_ref[hd:hd + 1, :]
    tq = o_ref.shape[0]
    for i in range(ATTN_GROUP):
        o_ref[:, i * hd:(i + 1) * hd] = ot[:, i * tq:(i + 1) * tq].T.astype(o_ref.dtype)


def flash_attention(qt, k, vt, tq=256):
    seq = k.shape[0]
    hd = ATTN_HEAD_DIM
    gw = ATTN_GROUP * hd
    return pl.pallas_call(
        _flash_kernel,
        grid=(ATTN_KV_HEADS, seq // tq),
        in_specs=[
            pl.BlockSpec((gw, tq), lambda g, i: (g, i)),
            pl.BlockSpec((seq, hd), lambda g, i: (0, g)),
            pl.BlockSpec((None, ATTN_V_ROWS, seq), lambda g, i: (g, 0, 0)),
        ],
        out_specs=pl.BlockSpec((tq, gw), lambda g, i: (i, g)),
        out_shape=jax.ShapeDtypeStruct((seq, ATTN_Q_HEADS * hd), BF16),
        scratch_shapes=[
            pltpu.VMEM((2, ATTN_KV_TILE, ATTN_GROUP * tq), F32),
            pltpu.VMEM((1, ATTN_GROUP * tq), F32),
            pltpu.VMEM((ATTN_V_ROWS, ATTN_GROUP * tq), F32),
        ],
        compiler_params=_params(("parallel", "arbitrary"), 48),
        name="flash_attention",
    )(qt, k, vt)


def _chunk_tri(reverse):
    idx = np.arange(GLA_BLOCK)
    same = (idx[:, None] // GLA_CHUNK) == (idx[None, :] // GLA_CHUNK)
    tri = (idx[None, :] >= idx[:, None]) if reverse else (idx[None, :] <= idx[:, None])
    return jnp.asarray((same & tri).astype(np.float32), dtype=BF16)


def _gate_weights(w_gk, first_row):
    w = jnp.zeros((LANES, GLA_DK), F32).at[first_row:first_row + GLA_GATE_RANK].set(w_gk)
    return w.astype(BF16)


def gla_layer(h, g_pre, w_in, w_gk_f, b_gk_f, w_gk_b, b_gk_b, g_head, w_out, g_post):
    w_main = w_in[:, :GLA_MAIN].astype(BF16)
    w_lr = jnp.pad(w_in[:, GLA_MAIN:], ((0, 0), (0, LANES - 2 * GLA_GATE_RANK))).astype(BF16)
    proj, lr = norm_matmul(h, g_pre, w_main, w_lr)
    o_f, o_b = gla_scan(proj, lr, w_gk_f, b_gk_f, w_gk_b, b_gk_b)
    return gla_post(o_f, o_b, proj, g_head, w_out.astype(BF16), h, g_post)


def gqa_layer(h, g_pre, w_in, g_q, g_k, w_out, g_post):
    w_in = w_in.astype(BF16)
    proj_qk, v = norm_matmul(h, g_pre, w_in[:, :ATTN_QK], w_in[:, ATTN_QK:], tm=512, tn=ATTN_QK // 2)
    qt, k, vt = qk_prep(proj_qk, v, g_q, g_k)
    o = flash_attention(qt, k, vt)
    seq, d = o.shape
    nblk = seq // ATTN_Q_BLOCK
    o = o.reshape(nblk, ATTN_Q_BLOCK, d).transpose(1, 0, 2).reshape(seq, d)
    return matmul_norm_res(o, w_out.astype(BF16), h, g_post)


def kernel(x, p, g_pre_mix, g_post_mix, g_pre_mlp, g_post_mlp, g_ple, w_mlp_up, w_mlp_down,
           w_ple_proj, w_ple_gate, gla_w_in, gla_w_gk_fwd, gla_b_gk_fwd, gla_w_gk_bwd,
           gla_b_gk_bwd, gla_g_head, gla_w_out, attn_w_in, attn_g_q, attn_g_k, attn_w_out):
    assert x.shape == (1, SEQ, D_MODEL)
    h = x[0]
    w_mlp_up, w_mlp_down = w_mlp_up.astype(BF16), w_mlp_down.astype(BF16)
    w_ple_gate, w_ple_proj = w_ple_gate.astype(BF16), w_ple_proj.astype(BF16)
    for i in range(DEPTH):
        j = i // 2
        if i % 2 == 0:
            h = gla_layer(h, g_pre_mix[i], gla_w_in[j], gla_w_gk_fwd[j], gla_b_gk_fwd[j],
                          gla_w_gk_bwd[j], gla_b_gk_bwd[j], gla_g_head[j], gla_w_out[j], g_post_mix[i])
        else:
            h = gqa_layer(h, g_pre_mix[i], attn_w_in[j], attn_g_q[j], attn_g_k[j], attn_w_out[j],
                          g_post_mix[i])
        h = mlp_block(h, g_pre_mlp[i], w_mlp_up, w_mlp_down, g_post_mlp[i], i)
        h = ple_block(h, p, w_ple_gate, w_ple_proj, g_ple[i], i)
    return h[None]
```

```python
import functools

import numpy as np
import jax
import jax.numpy as jnp
from jax import lax
from jax.experimental import pallas as pl
from jax.experimental.pallas import tpu as pltpu

F32 = jnp.float32
BF16 = jnp.bfloat16

D_MODEL = 2048
SEQ = 8192
DEPTH = 2
GRID_W = 64
PLE_DIM = 256
NORM_EPS = 1e-6
D_FF = 4 * D_MODEL

GLA_HEADS = 4
GLA_DK = D_MODEL // 2
GLA_DV = D_MODEL
GLA_HEAD_K = GLA_DK // GLA_HEADS
GLA_HEAD_V = GLA_DV // GLA_HEADS
GLA_GATE_RANK = 16
GLA_GATE_TAU = 16.0
GLA_CHUNK = 64
GLA_MAIN = 2 * GLA_DK + 2 * GLA_DV
GLA_BLOCK = 256

ATTN_HEAD_DIM = 128
ATTN_Q_HEADS = D_MODEL // ATTN_HEAD_DIM
ATTN_KV_HEADS = 4
ATTN_GROUP = ATTN_Q_HEADS // ATTN_KV_HEADS
ATTN_Q_BLOCK = 128
ATTN_KV_TILE = 1024
ATTN_V_ROWS = ATTN_HEAD_DIM + 16
ATTN_QK = (ATTN_Q_HEADS + ATTN_KV_HEADS) * ATTN_HEAD_DIM
ROPE_THETA = 10000.0
ROPE_AXIS_DIM = ATTN_HEAD_DIM // 2

LANES = 128
MIB = 1024 * 1024
LOG2E = 1.4426950408889634


def _params(semantics, vmem_mib):
    return pltpu.CompilerParams(dimension_semantics=semantics, vmem_limit_bytes=vmem_mib * MIB)


def _rms(x, gain):
    return x * lax.rsqrt(jnp.mean(x * x, axis=-1, keepdims=True) + NORM_EPS) * gain


def _dot(a, b):
    return jnp.dot(a, b, preferred_element_type=F32)


def _dot_nt(a, b):
    return lax.dot_general(a, b, (((1,), (1,)), ((), ())), preferred_element_type=F32)


def _dot_tn(a, b):
    return lax.dot_general(a, b, (((0,), (0,)), ((), ())), preferred_element_type=F32)


def _norm_matmul_kernel(h_ref, g_ref, w_ref, wx_ref, o_ref, ox_ref, u_ref):
    @pl.when(pl.program_id(1) == 0)
    def _():
        u_ref[...] = _rms(h_ref[...], g_ref[...]).astype(BF16)
        ox_ref[...] = _dot(u_ref[...], wx_ref[...]).astype(ox_ref.dtype)

    o_ref[...] = _dot(u_ref[...], w_ref[...]).astype(o_ref.dtype)


def norm_matmul(h, g, w, w_extra, tm=1024, tn=1024):
    m, k = h.shape
    n = w.shape[1]
    nx = w_extra.shape[1]
    return pl.pallas_call(
        _norm_matmul_kernel,
        grid=(m // tm, n // tn),
        in_specs=[
            pl.BlockSpec((tm, k), lambda i, j: (i, 0)),
            pl.BlockSpec((1, k), lambda i, j: (0, 0)),
            pl.BlockSpec((k, tn), lambda i, j: (0, j)),
            pl.BlockSpec((k, nx), lambda i, j: (0, 0)),
        ],
        out_specs=[
            pl.BlockSpec((tm, tn), lambda i, j: (i, j)),
            pl.BlockSpec((tm, nx), lambda i, j: (i, 0)),
        ],
        out_shape=[jax.ShapeDtypeStruct((m, n), F32), jax.ShapeDtypeStruct((m, nx), BF16)],
        scratch_shapes=[pltpu.VMEM((tm, k), BF16)],
        compiler_params=_params(("parallel", "arbitrary"), 48),
        name="norm_matmul",
    )(h, g.reshape(1, k), w, w_extra)


def _gla_kernel(qf_ref, kf_ref, vf_ref, lrf_ref, qb_ref, kb_ref, vb_ref, lrb_ref, wf_ref, bf_ref, wb_ref, bb_ref,
                trif_ref, trib_ref, of_ref, ob_ref, s_ref, cum_ref):
    c = GLA_CHUNK
    nchunk = GLA_BLOCK // c

    @pl.when(pl.program_id(0) == 0)
    def _():
        s_ref[...] = jnp.zeros_like(s_ref)

    def cumulative(d, lr_ref, w_ref, b_ref, tri_ref):
        z = _dot(lr_ref[...], w_ref[...]) + b_ref[...]
        la = (jnp.minimum(z, 0.0) - jnp.log1p(jnp.exp(-jnp.abs(z)))) * (1.0 / GLA_GATE_TAU)
        la_hi = la.astype(BF16)
        la_lo = (la - la_hi.astype(F32)).astype(BF16)
        tri = tri_ref[...]
        cum_ref[d] = _dot(tri, la_hi) + _dot(tri, la_lo)

    cumulative(0, lrf_ref, wf_ref, bf_ref, trif_ref)
    cumulative(1, lrb_ref, wb_ref, bb_ref, trib_ref)

    row = lax.broadcasted_iota(jnp.int32, (c, c), 0)
    col = lax.broadcasted_iota(jnp.int32, (c, c), 1)
    directions = (
        (qf_ref, kf_ref, vf_ref, of_ref, col <= row, c // 2 - 1, c - 1),
        (qb_ref, kb_ref, vb_ref, ob_ref, col > row, c // 2, 0),
    )
    for step in range(nchunk):
        for d, (q_ref, k_ref, v_ref, o_ref, mask, ref_row, last_row) in enumerate(directions):
            j = nchunk - 1 - step if d else step
            sl = slice(j * c, (j + 1) * c)
            for h in range(GLA_HEADS):
                hk = slice(h * GLA_HEAD_K, (h + 1) * GLA_HEAD_K)
                hv = slice(h * GLA_HEAD_V, (h + 1) * GLA_HEAD_V)
                cumc = cum_ref[d, sl, hk]
                refc = cumc[ref_row:ref_row + 1]
                lastc = cumc[last_row:last_row + 1]
                qc = q_ref[sl, hk] * (GLA_HEAD_K ** -0.5)
                kc = k_ref[sl, hk]
                vc = v_ref[sl, hv].astype(BF16)
                qt = (qc * jnp.exp(cumc - refc)).astype(BF16)
                kt = (kc * jnp.exp(refc - cumc)).astype(BF16)
                qd = (qc * jnp.exp(cumc)).astype(BF16)
                ke = (kc * jnp.exp(lastc - cumc)).astype(BF16)
                scores = jnp.where(mask, _dot_nt(qt, kt), 0.0).astype(BF16)
                state = s_ref[d * GLA_HEADS + h]
                o_ref[sl, hv] = _dot(scores, vc) + _dot_nt(qd, state.astype(BF16))
                s_ref[d * GLA_HEADS + h] = state * jnp.exp(lastc) + _dot_tn(vc, ke)


def gla_scan(proj, lr, w_f, b_f, w_b, b_b):
    seq = proj.shape[0]
    t = GLA_BLOCK
    nb = seq // t
    fwd = lambda i: i
    bwd = lambda i: nb - 1 - i
    const = lambda i: (0, 0)
    v_blk = 2 * GLA_DK // GLA_DV

    def operands(blk):
        return [
            pl.BlockSpec((t, GLA_DK), lambda i: (blk(i), 0)),
            pl.BlockSpec((t, GLA_DK), lambda i: (blk(i), 1)),
            pl.BlockSpec((t, GLA_DV), lambda i: (blk(i), v_blk)),
            pl.BlockSpec((t, LANES), lambda i: (blk(i), 0)),
        ]

    return pl.pallas_call(
        _gla_kernel,
        grid=(nb,),
        in_specs=operands(fwd) + operands(bwd) + [
            pl.BlockSpec((LANES, GLA_DK), const),
            pl.BlockSpec((1, GLA_DK), const),
            pl.BlockSpec((LANES, GLA_DK), const),
            pl.BlockSpec((1, GLA_DK), const),
            pl.BlockSpec((t, t), const),
            pl.BlockSpec((t, t), const),
        ],
        out_specs=[
            pl.BlockSpec((t, GLA_DV), lambda i: (fwd(i), 0)),
            pl.BlockSpec((t, GLA_DV), lambda i: (bwd(i), 0)),
        ],
        out_shape=[jax.ShapeDtypeStruct((seq, GLA_DV), F32)] * 2,
        scratch_shapes=[
            pltpu.VMEM((2 * GLA_HEADS, GLA_HEAD_V, GLA_HEAD_K), F32),
            pltpu.VMEM((2, t, GLA_DK), F32),
        ],
        compiler_params=_params(("arbitrary",), 48),
        name="gla_scan",
    )(proj, proj, proj, lr, proj, proj, proj, lr,
      _gate_weights(w_f, 0), b_f.reshape(1, GLA_DK), _gate_weights(w_b, GLA_GATE_RANK), b_b.reshape(1, GLA_DK),
      _chunk_tri(False), _chunk_tri(True))


def _gla_post_kernel(of_ref, ob_ref, og_ref, gh_ref, w_ref, h_ref, gp_ref, o_ref, y_ref):
    for hd in range(GLA_HEADS):
        sl = slice(hd * GLA_HEAD_V, (hd + 1) * GLA_HEAD_V)
        o = _rms(of_ref[:, sl] + ob_ref[:, sl], gh_ref[...])
        og = og_ref[:, sl]
        y_ref[:, sl] = (o * (og / (1.0 + jnp.exp(-og)))).astype(BF16)
    mix = _dot(y_ref[...], w_ref[...])
    o_ref[...] = h_ref[...] + _rms(mix, gp_ref[...])


def gla_post(o_f, o_b, proj, g_head, w_out, h, g_post, tm=256):
    m, d = h.shape
    og_blk = (2 * GLA_DK + GLA_DV) // GLA_DV
    row = lambda i: (i, 0)
    const = lambda i: (0, 0)
    return pl.pallas_call(
        _gla_post_kernel,
        grid=(m // tm,),
        in_specs=[
            pl.BlockSpec((tm, GLA_DV), row),
            pl.BlockSpec((tm, GLA_DV), row),
            pl.BlockSpec((tm, GLA_DV), lambda i: (i, og_blk)),
            pl.BlockSpec((1, GLA_HEAD_V), const),
            pl.BlockSpec((GLA_DV, d), const),
            pl.BlockSpec((tm, d), row),
            pl.BlockSpec((1, d), const),
        ],
        out_specs=pl.BlockSpec((tm, d), row),
        out_shape=jax.ShapeDtypeStruct((m, d), F32),
        scratch_shapes=[pltpu.VMEM((tm, GLA_DV), BF16)],
        compiler_params=_params(("parallel",), 48),
        name="gla_post",
    )(o_f, o_b, proj, g_head.reshape(1, GLA_HEAD_V), w_out, h, g_post.reshape(1, d))


def _matmul_norm_res_kernel(a_ref, w_ref, h_ref, g_ref, o_ref):
    o_ref[...] = h_ref[...] + _rms(_dot(a_ref[...], w_ref[...]), g_ref[...])


def matmul_norm_res(a, w, h, g, tm=512):
    m, d = h.shape
    k = a.shape[1]
    row = lambda i: (i, 0)
    const = lambda i: (0, 0)
    return pl.pallas_call(
        _matmul_norm_res_kernel,
        grid=(m // tm,),
        in_specs=[
            pl.BlockSpec((tm, k), row),
            pl.BlockSpec((k, d), const),
            pl.BlockSpec((tm, d), row),
            pl.BlockSpec((1, d), const),
        ],
        out_specs=pl.BlockSpec((tm, d), row),
        out_shape=jax.ShapeDtypeStruct((m, d), F32),
        compiler_params=_params(("parallel",), 48),
        name="matmul_norm_res",
    )(a, w, h, g.reshape(1, d))


def _mlp_kernel(h_ref, g1_ref, wu_ref, wd_ref, g2_ref, o_ref, u_ref, acc_ref):
    f = pl.program_id(1)

    @pl.when(f == 0)
    def _():
        u_ref[...] = _rms(h_ref[...], g1_ref[...]).astype(BF16)
        acc_ref[...] = jnp.zeros_like(acc_ref)

    hid = jnp.maximum(_dot(u_ref[...], wu_ref[...]), 0.0)
    acc_ref[...] += _dot((hid * hid).astype(BF16), wd_ref[...])

    @pl.when(f == pl.num_programs(1) - 1)
    def _():
        o_ref[...] = h_ref[...] + _rms(acc_ref[...], g2_ref[...])


def mlp_block(h, g_pre, w_up, w_down, g_post, layer, tm=512, tf=1024):
    m, d = h.shape
    ff = w_up.shape[2]
    return pl.pallas_call(
        _mlp_kernel,
        grid=(m // tm, ff // tf),
        in_specs=[
            pl.BlockSpec((tm, d), lambda i, f: (i, 0)),
            pl.BlockSpec((1, d), lambda i, f: (0, 0)),
            pl.BlockSpec((None, d, tf), lambda i, f: (layer, 0, f)),
            pl.BlockSpec((None, tf, d), lambda i, f: (layer, f, 0)),
            pl.BlockSpec((1, d), lambda i, f: (0, 0)),
        ],
        out_specs=pl.BlockSpec((tm, d), lambda i, f: (i, 0)),
        out_shape=jax.ShapeDtypeStruct((m, d), F32),
        scratch_shapes=[pltpu.VMEM((tm, d), BF16), pltpu.VMEM((tm, d), F32)],
        compiler_params=_params(("parallel", "arbitrary"), 48),
        name="mlp_block",
    )(h, g_pre.reshape(1, d), w_up, w_down, g_post.reshape(1, d))


def _ple_kernel(h_ref, p_ref, wg_ref, wp_ref, g_ref, o_ref):
    x = h_ref[...]
    gate = 1.0 / (1.0 + jnp.exp(-_dot(x.astype(BF16), wg_ref[...])))
    e = _dot(p_ref[...].astype(BF16), wp_ref[...])
    o_ref[...] = x + _rms(gate * e, g_ref[...])


def ple_block(h, p, w_gate, w_proj, g, layer, tm=512):
    m, d = h.shape
    pd = p.shape[-1]
    row = lambda i: (i, 0)
    const = lambda i: (0, 0)
    return pl.pallas_call(
        _ple_kernel,
        grid=(m // tm,),
        in_specs=[
            pl.BlockSpec((tm, d), row),
            pl.BlockSpec((None, None, tm, pd), lambda i: (layer, 0, i, 0)),
            pl.BlockSpec((None, d, d), lambda i: (layer, 0, 0)),
            pl.BlockSpec((None, pd, d), lambda i: (layer, 0, 0)),
            pl.BlockSpec((1, d), const),
        ],
        out_specs=pl.BlockSpec((tm, d), row),
        out_shape=jax.ShapeDtypeStruct((m, d), F32),
        compiler_params=_params(("parallel",), 48),
        name="ple_block",
    )(h, p, w_gate, w_proj, g.reshape(1, d))


def _rope_tables(seq):
    rows = seq // GRID_W
    t_row = np.repeat(np.arange(rows), GRID_W).astype(np.float64)
    t_col = np.tile(np.arange(GRID_W), rows).astype(np.float64)
    inv_freq = 1.0 / (ROPE_THETA ** (np.arange(0, ROPE_AXIS_DIM, 2, dtype=np.float64) / ROPE_AXIS_DIM))
    a_row = t_row[:, None] * inv_freq
    a_col = t_col[:, None] * inv_freq
    ang = np.concatenate([a_row, a_row, a_col, a_col], axis=1)
    half = ROPE_AXIS_DIM // 2
    sign = np.where((np.arange(ATTN_HEAD_DIM) % ROPE_AXIS_DIM) < half, -1.0, 1.0)
    return jnp.asarray(np.cos(ang), F32), jnp.asarray(np.sin(ang) * sign, F32)


def _qk_prep_kernel(x_ref, v_ref, cos_ref, sin_ref, gq_ref, gk_ref, qt_ref, k_ref, vt_ref):
    cos = cos_ref[...]
    sin = sin_ref[...]
    lane = lax.broadcasted_iota(jnp.int32, cos.shape, 1)
    first_half = (lane % ROPE_AXIS_DIM) < (ROPE_AXIS_DIM // 2)
    hd = ATTN_HEAD_DIM

    def norm_rope(x, gain):
        y = _rms(x, gain)
        partner = jnp.where(first_half, pltpu.roll(y, hd - ROPE_AXIS_DIM // 2, 1),
                            pltpu.roll(y, ROPE_AXIS_DIM // 2, 1))
        return y * cos + partner * sin

    q_scale = (hd ** -0.5) * LOG2E
    for i in range(ATTN_Q_HEADS):
        sl = slice(i * hd, (i + 1) * hd)
        qt_ref[sl, :] = (norm_rope(x_ref[:, sl], gq_ref[...]) * q_scale).T.astype(BF16)
    for i in range(ATTN_KV_HEADS):
        src = slice((ATTN_Q_HEADS + i) * hd, (ATTN_Q_HEADS + i + 1) * hd)
        k_ref[:, i * hd:(i + 1) * hd] = norm_rope(x_ref[:, src], gk_ref[...]).astype(BF16)
        vt_ref[i, :hd, :] = v_ref[:, i * hd:(i + 1) * hd].astype(F32).T.astype(BF16)
        vt_ref[i, hd:, :] = jnp.ones((ATTN_V_ROWS - hd, x_ref.shape[0]), BF16)


def qk_prep(proj_qk, v, g_q, g_k, tm=512):
    seq = proj_qk.shape[0]
    hd = ATTN_HEAD_DIM
    cos, sin = _rope_tables(seq)
    row = lambda i: (i, 0)
    const = lambda i: (0, 0)
    return pl.pallas_call(
        _qk_prep_kernel,
        grid=(seq // tm,),
        in_specs=[
            pl.BlockSpec((tm, ATTN_QK), row),
            pl.BlockSpec((tm, ATTN_KV_HEADS * hd), row),
            pl.BlockSpec((tm, hd), row),
            pl.BlockSpec((tm, hd), row),
            pl.BlockSpec((1, hd), const),
            pl.BlockSpec((1, hd), const),
        ],
        out_specs=[
            pl.BlockSpec((ATTN_Q_HEADS * hd, tm), lambda i: (0, i)),
            pl.BlockSpec((tm, ATTN_KV_HEADS * hd), row),
            pl.BlockSpec((ATTN_KV_HEADS, ATTN_V_ROWS, tm), lambda i: (0, 0, i)),
        ],
        out_shape=[
            jax.ShapeDtypeStruct((ATTN_Q_HEADS * hd, seq), BF16),
            jax.ShapeDtypeStruct((seq, ATTN_KV_HEADS * hd), BF16),
            jax.ShapeDtypeStruct((ATTN_KV_HEADS, ATTN_V_ROWS, seq), BF16),
        ],
        compiler_params=_params(("parallel",), 48),
        name="qk_prep",
    )(proj_qk, v, cos, sin, g_q.reshape(1, hd), g_k.reshape(1, hd))


def _flash_kernel(qt_ref, k_ref, vt_ref, o_ref, st_ref, m_ref, acc_ref):
    hd = ATTN_HEAD_DIM
    tk = ATTN_KV_TILE
    qt = jnp.concatenate([qt_ref[i * hd:(i + 1) * hd, :] for i in range(ATTN_GROUP)], axis=1)
    m_ref[...] = jnp.full_like(m_ref, -jnp.inf)
    acc_ref[...] = jnp.zeros_like(acc_ref)

    def scores(j, slot):
        start = pl.multiple_of(j * tk, tk)
        st_ref[slot] = _dot(k_ref[pl.ds(start, tk), :], qt)

    def consume(j, slot):
        start = pl.multiple_of(j * tk, tk)
        st = st_ref[slot]
        m_prev = m_ref[...]
        m_next = jnp.maximum(m_prev, jnp.max(st, axis=0, keepdims=True))
        alpha = jnp.exp2(m_prev - m_next)
        pt = jnp.exp2(st - m_next)
        acc_ref[...] = alpha * acc_ref[...] + _dot(vt_ref[:, pl.ds(start, tk)], pt.astype(BF16))
        m_ref[...] = m_next

    def pair(j, last):
        scores(j + 1, 1)
        consume(j, 0)
        if not last:
            scores(j + 2, 0)
        consume(j + 1, 1)

    npair = k_ref.shape[0] // (2 * tk)
    scores(0, 0)

    def body(jj, carry):
        pair(2 * jj, last=False)
        return carry

    lax.fori_loop(0, npair - 1, body, 0)
    pair(2 * (npair - 1), last=True)
    ot = acc_ref[:hd, :] / acc_ref[hd:hd + 1, :]
    tq = o_ref.shape[0]
    for i in range(ATTN_GROUP):
        o_ref[:, i * hd:(i + 1) * hd] = ot[:, i * tq:(i + 1) * tq].T.astype(o_ref.dtype)


def flash_attention(qt, k, vt, tq=256):
    seq = k.shape[0]
    hd = ATTN_HEAD_DIM
    gw = ATTN_GROUP * hd
    return pl.pallas_call(
        _flash_kernel,
        grid=(ATTN_KV_HEADS, seq // tq),
        in_specs=[
            pl.BlockSpec((gw, tq), lambda g, i: (g, i)),
            pl.BlockSpec((seq, hd), lambda g, i: (0, g)),
            pl.BlockSpec((None, ATTN_V_ROWS, seq), lambda g, i: (g, 0, 0)),
        ],
        out_specs=pl.BlockSpec((tq, gw), lambda g, i: (i, g)),
        out_shape=jax.ShapeDtypeStruct((seq, ATTN_Q_HEADS * hd), BF16),
        scratch_shapes=[
            pltpu.VMEM((2, ATTN_KV_TILE, ATTN_GROUP * tq), F32),
            pltpu.VMEM((1, ATTN_GROUP * tq), F32),
            pltpu.VMEM((ATTN_V_ROWS, ATTN_GROUP * tq), F32),
        ],
        compiler_params=_params(("parallel", "arbitrary"), 48),
        name="flash_attention",
    )(qt, k, vt)


def _chunk_tri(reverse):
    idx = np.arange(GLA_BLOCK)
    same = (idx[:, None] // GLA_CHUNK) == (idx[None, :] // GLA_CHUNK)
    tri = (idx[None, :] >= idx[:, None]) if reverse else (idx[None, :] <= idx[:, None])
    return jnp.asarray((same & tri).astype(np.float32), dtype=BF16)


def _gate_weights(w_gk, first_row):
    w = jnp.zeros((LANES, GLA_DK), F32).at[first_row:first_row + GLA_GATE_RANK].set(w_gk)
    return w.astype(BF16)


def gla_layer(h, g_pre, w_in, w_gk_f, b_gk_f, w_gk_b, b_gk_b, g_head, w_out, g_post):
    w_main = w_in[:, :GLA_MAIN].astype(BF16)
    w_lr = jnp.pad(w_in[:, GLA_MAIN:], ((0, 0), (0, LANES - 2 * GLA_GATE_RANK))).astype(BF16)
    proj, lr = norm_matmul(h, g_pre, w_main, w_lr)
    o_f, o_b = gla_scan(proj, lr, w_gk_f, b_gk_f, w_gk_b, b_gk_b)
    return gla_post(o_f, o_b, proj, g_head, w_out.astype(BF16), h, g_post)


def gqa_layer(h, g_pre, w_in, g_q, g_k, w_out, g_post):
    w_in = w_in.astype(BF16)
    proj_qk, v = norm_matmul(h, g_pre, w_in[:, :ATTN_QK], w_in[:, ATTN_QK:], tm=512, tn=ATTN_QK // 2)
    qt, k, vt = qk_prep(proj_qk, v, g_q, g_k)
    o = flash_attention(qt, k, vt)
    seq, d = o.shape
    nblk = seq // ATTN_Q_BLOCK
    o = o.reshape(nblk, ATTN_Q_BLOCK, d).transpose(1, 0, 2).reshape(seq, d)
    return matmul_norm_res(o, w_out.astype(BF16), h, g_post)


def kernel(x, p, g_pre_mix, g_post_mix, g_pre_mlp, g_post_mlp, g_ple, w_mlp_up, w_mlp_down,
           w_ple_proj, w_ple_gate, gla_w_in, gla_w_gk_fwd, gla_b_gk_fwd, gla_w_gk_bwd,
           gla_b_gk_bwd, gla_g_head, gla_w_out, attn_w_in, attn_g_q, attn_g_k, attn_w_out):
    assert x.shape == (1, SEQ, D_MODEL)
    h = x[0]
    w_mlp_up, w_mlp_down = w_mlp_up.astype(BF16), w_mlp_down.astype(BF16)
    w_ple_gate, w_ple_proj = w_ple_gate.astype(BF16), w_ple_proj.astype(BF16)
    for i in range(DEPTH):
        j = i // 2
        if i % 2 == 0:
            h = gla_layer(h, g_pre_mix[i], gla_w_in[j], gla_w_gk_fwd[j], gla_b_gk_fwd[j],
                          gla_w_gk_bwd[j], gla_b_gk_bwd[j], gla_g_head[j], gla_w_out[j], g_post_mix[i])
        else:
            h = gqa_layer(h, g_pre_mix[i], attn_w_in[j], attn_g_q[j], attn_g_k[j], attn_w_out[j],
                          g_post_mix[i])
        h = mlp_block(h, g_pre_mlp[i], w_mlp_up, w_mlp_down, g_post_mlp[i], i)
        h = ple_block(h, p, w_ple_gate, w_ple_proj, g_ple[i], i)
    return h[None]
```

```python
import functools

import numpy as np
import jax
import jax.numpy as jnp
from jax import lax
from jax.experimental import pallas as pl
from jax.experimental.pallas import tpu as pltpu

F32 = jnp.float32
BF16 = jnp.bfloat16

D_MODEL = 2048
SEQ = 8192
DEPTH = 2
GRID_W = 64
PLE_DIM = 256
NORM_EPS = 1e-6
D_FF = 4 * D_MODEL

GLA_HEADS = 4
GLA_DK = D_MODEL // 2
GLA_DV = D_MODEL
GLA_HEAD_K = GLA_DK // GLA_HEADS
GLA_HEAD_V = GLA_DV // GLA_HEADS
GLA_GATE_RANK = 16
GLA_GATE_TAU = 16.0
GLA_CHUNK = 64
GLA_MAIN = 2 * GLA_DK + 2 * GLA_DV
GLA_BLOCK = 256

ATTN_HEAD_DIM = 128
ATTN_Q_HEADS = D_MODEL // ATTN_HEAD_DIM
ATTN_KV_HEADS = 4
ATTN_GROUP = ATTN_Q_HEADS // ATTN_KV_HEADS
ATTN_Q_BLOCK = 128
ATTN_KV_TILE = 1024
ATTN_V_ROWS = ATTN_HEAD_DIM + 16
ATTN_QK = (ATTN_Q_HEADS + ATTN_KV_HEADS) * ATTN_HEAD_DIM
ROPE_THETA = 10000.0
ROPE_AXIS_DIM = ATTN_HEAD_DIM // 2

LANES = 128
MIB = 1024 * 1024
LOG2E = 1.4426950408889634


def _params(semantics, vmem_mib):
    return pltpu.CompilerParams(dimension_semantics=semantics, vmem_limit_bytes=vmem_mib * MIB)


def _rms(x, gain):
    return x * lax.rsqrt(jnp.mean(x * x, axis=-1, keepdims=True) + NORM_EPS) * gain


def _dot(a, b):
    return jnp.dot(a, b, preferred_element_type=F32)


def _dot_nt(a, b):
    return lax.dot_general(a, b, (((1,), (1,)), ((), ())), preferred_element_type=F32)


def _dot_tn(a, b):
    return lax.dot_general(a, b, (((0,), (0,)), ((), ())), preferred_element_type=F32)


def _resident(shape):
    return pl.BlockSpec(shape, lambda *_: (0,) * len(shape), pipeline_mode=pl.Buffered(1))


def _norm_matmul_kernel(h_ref, g_ref, w_ref, wx_ref, o_ref, ox_ref, u_ref, *, tn):
    u_ref[...] = _rms(h_ref[...], g_ref[...]).astype(BF16)
    ox_ref[...] = _dot(u_ref[...], wx_ref[...]).astype(ox_ref.dtype)
    for c in range(w_ref.shape[1] // tn):
        sl = slice(c * tn, (c + 1) * tn)
        o_ref[:, sl] = _dot(u_ref[...], w_ref[:, sl]).astype(o_ref.dtype)


def norm_matmul(h, g, w, w_extra, tm, tn):
    m, k = h.shape
    n = w.shape[1]
    nx = w_extra.shape[1]
    return pl.pallas_call(
        functools.partial(_norm_matmul_kernel, tn=tn),
        grid=(m // tm,),
        in_specs=[
            pl.BlockSpec((tm, k), lambda i: (i, 0)),
            _resident((1, k)),
            _resident((k, n)),
            _resident((k, nx)),
        ],
        out_specs=[
            pl.BlockSpec((tm, n), lambda i: (i, 0)),
            pl.BlockSpec((tm, nx), lambda i: (i, 0)),
        ],
        out_shape=[jax.ShapeDtypeStruct((m, n), F32), jax.ShapeDtypeStruct((m, nx), BF16)],
        scratch_shapes=[pltpu.VMEM((tm, k), BF16)],
        compiler_params=_params(("parallel",), 56),
        name="norm_matmul",
    )(h, g.reshape(1, k), w, w_extra)


def _gla_kernel(qf_ref, kf_ref, vf_ref, lrf_ref, qb_ref, kb_ref, vb_ref, lrb_ref, wf_ref, bf_ref, wb_ref, bb_ref,
                trif_ref, trib_ref, of_ref, ob_ref, s_ref, cum_ref):
    c = GLA_CHUNK
    nchunk = GLA_BLOCK // c

    @pl.when(pl.program_id(0) == 0)
    def _():
        s_ref[...] = jnp.zeros_like(s_ref)

    def cumulative(d, lr_ref, w_ref, b_ref, tri_ref):
        z = _dot(lr_ref[...], w_ref[...]) + b_ref[...]
        la = (jnp.minimum(z, 0.0) - jnp.log1p(jnp.exp(-jnp.abs(z)))) * (1.0 / GLA_GATE_TAU)
        la_hi = la.astype(BF16)
        la_lo = (la - la_hi.astype(F32)).astype(BF16)
        tri = tri_ref[...]
        cum_ref[d] = _dot(tri, la_hi) + _dot(tri, la_lo)

    cumulative(0, lrf_ref, wf_ref, bf_ref, trif_ref)
    cumulative(1, lrb_ref, wb_ref, bb_ref, trib_ref)

    row = lax.broadcasted_iota(jnp.int32, (c, c), 0)
    col = lax.broadcasted_iota(jnp.int32, (c, c), 1)
    directions = (
        (qf_ref, kf_ref, vf_ref, of_ref, col <= row, c // 2 - 1, c - 1),
        (qb_ref, kb_ref, vb_ref, ob_ref, col > row, c // 2, 0),
    )
    for step in range(nchunk):
        for d, (q_ref, k_ref, v_ref, o_ref, mask, ref_row, last_row) in enumerate(directions):
            j = nchunk - 1 - step if d else step
            sl = slice(j * c, (j + 1) * c)
            for h in range(GLA_HEADS):
                hk = slice(h * GLA_HEAD_K, (h + 1) * GLA_HEAD_K)
                hv = slice(h * GLA_HEAD_V, (h + 1) * GLA_HEAD_V)
                cumc = cum_ref[d, sl, hk]
                refc = cumc[ref_row:ref_row + 1]
                lastc = cumc[last_row:last_row + 1]
                qc = q_ref[sl, hk] * (GLA_HEAD_K ** -0.5)
                kc = k_ref[sl, hk]
                vc = v_ref[sl, hv].astype(BF16)
                qt = (qc * jnp.exp(cumc - refc)).astype(BF16)
                kt = (kc * jnp.exp(refc - cumc)).astype(BF16)
                qd = (qc * jnp.exp(cumc)).astype(BF16)
                ke = (kc * jnp.exp(lastc - cumc)).astype(BF16)
                scores = jnp.where(mask, _dot_nt(qt, kt), 0.0).astype(BF16)
                state = s_ref[d * GLA_HEADS + h]
                o_ref[sl, hv] = _dot(scores, vc) + _dot_nt(qd, state.astype(BF16))
                s_ref[d * GLA_HEADS + h] = state * jnp.exp(lastc) + _dot_tn(vc, ke)


def gla_scan(proj, lr, w_f, b_f, w_b, b_b):
    seq = proj.shape[0]
    t = GLA_BLOCK
    nb = seq // t
    fwd = lambda i: i
    bwd = lambda i: nb - 1 - i
    const = lambda i: (0, 0)
    v_blk = 2 * GLA_DK // GLA_DV

    def operands(blk):
        return [
            pl.BlockSpec((t, GLA_DK), lambda i: (blk(i), 0)),
            pl.BlockSpec((t, GLA_DK), lambda i: (blk(i), 1)),
            pl.BlockSpec((t, GLA_DV), lambda i: (blk(i), v_blk)),
            pl.BlockSpec((t, LANES), lambda i: (blk(i), 0)),
        ]

    return pl.pallas_call(
        _gla_kernel,
        grid=(nb,),
        in_specs=operands(fwd) + operands(bwd) + [
            pl.BlockSpec((LANES, GLA_DK), const),
            pl.BlockSpec((1, GLA_DK), const),
            pl.BlockSpec((LANES, GLA_DK), const),
            pl.BlockSpec((1, GLA_DK), const),
            pl.BlockSpec((t, t), const),
            pl.BlockSpec((t, t), const),
        ],
        out_specs=[
            pl.BlockSpec((t, GLA_DV), lambda i: (fwd(i), 0)),
            pl.BlockSpec((t, GLA_DV), lambda i: (bwd(i), 0)),
        ],
        out_shape=[jax.ShapeDtypeStruct((seq, GLA_DV), F32)] * 2,
        scratch_shapes=[
            pltpu.VMEM((2 * GLA_HEADS, GLA_HEAD_V, GLA_HEAD_K), F32),
            pltpu.VMEM((2, t, GLA_DK), F32),
        ],
        compiler_params=_params(("arbitrary",), 48),
        name="gla_scan",
    )(proj, proj, proj, lr, proj, proj, proj, lr,
      _gate_weights(w_f, 0), b_f.reshape(1, GLA_DK), _gate_weights(w_b, GLA_GATE_RANK), b_b.reshape(1, GLA_DK),
      _chunk_tri(False), _chunk_tri(True))


def _gla_post_kernel(of_ref, ob_ref, og_ref, gh_ref, w_ref, h_ref, gp_ref, o_ref, y_ref):
    for hd in range(GLA_HEADS):
        sl = slice(hd * GLA_HEAD_V, (hd + 1) * GLA_HEAD_V)
        o = _rms(of_ref[:, sl] + ob_ref[:, sl], gh_ref[...])
        og = og_ref[:, sl]
        y_ref[:, sl] = (o * (og / (1.0 + jnp.exp(-og)))).astype(BF16)
    mix = _dot(y_ref[...], w_ref[...])
    o_ref[...] = h_ref[...] + _rms(mix, gp_ref[...])


def gla_post(o_f, o_b, proj, g_head, w_out, h, g_post, tm=512):
    m, d = h.shape
    og_blk = (2 * GLA_DK + GLA_DV) // GLA_DV
    row = lambda i: (i, 0)
    return pl.pallas_call(
        _gla_post_kernel,
        grid=(m // tm,),
        in_specs=[
            pl.BlockSpec((tm, GLA_DV), row),
            pl.BlockSpec((tm, GLA_DV), row),
            pl.BlockSpec((tm, GLA_DV), lambda i: (i, og_blk)),
            _resident((1, GLA_HEAD_V)),
            _resident((GLA_DV, d)),
            pl.BlockSpec((tm, d), row),
            _resident((1, d)),
        ],
        out_specs=pl.BlockSpec((tm, d), row),
        out_shape=jax.ShapeDtypeStruct((m, d), F32),
        scratch_shapes=[pltpu.VMEM((tm, GLA_DV), BF16)],
        compiler_params=_params(("parallel",), 56),
        name="gla_post",
    )(o_f, o_b, proj, g_head.reshape(1, GLA_HEAD_V), w_out, h, g_post.reshape(1, d))


def _matmul_norm_res_kernel(a_ref, w_ref, h_ref, g_ref, o_ref):
    o_ref[...] = h_ref[...] + _rms(_dot(a_ref[...], w_ref[...]), g_ref[...])


def matmul_norm_res(a, w, h, g, tm=512):
    m, d = h.shape
    k = a.shape[1]
    row = lambda i: (i, 0)
    return pl.pallas_call(
        _matmul_norm_res_kernel,
        grid=(m // tm,),
        in_specs=[
            pl.BlockSpec((tm, k), row),
            _resident((k, d)),
            pl.BlockSpec((tm, d), row),
            _resident((1, d)),
        ],
        out_specs=pl.BlockSpec((tm, d), row),
        out_shape=jax.ShapeDtypeStruct((m, d), F32),
        compiler_params=_params(("parallel",), 48),
        name="matmul_norm_res",
    )(a, w, h, g.reshape(1, d))


def _mlp_kernel(h_ref, g1_ref, wu_ref, wd_ref, g2_ref, o_ref, u_ref, acc_ref):
    f = pl.program_id(1)

    @pl.when(f == 0)
    def _():
        u_ref[...] = _rms(h_ref[...], g1_ref[...]).astype(BF16)
        acc_ref[...] = jnp.zeros_like(acc_ref)

    hid = jnp.maximum(_dot(u_ref[...], wu_ref[...]), 0.0)
    acc_ref[...] += _dot((hid * hid).astype(BF16), wd_ref[...])

    @pl.when(f == pl.num_programs(1) - 1)
    def _():
        o_ref[...] = h_ref[...] + _rms(acc_ref[...], g2_ref[...])


def mlp_block(h, g_pre, w_up, w_down, g_post, layer, tm=512, tf=1024):
    m, d = h.shape
    ff = w_up.shape[2]
    return pl.pallas_call(
        _mlp_kernel,
        grid=(m // tm, ff // tf),
        in_specs=[
            pl.BlockSpec((tm, d), lambda i, f: (i, 0)),
            pl.BlockSpec((1, d), lambda i, f: (0, 0)),
            pl.BlockSpec((None, d, tf), lambda i, f: (layer, 0, f)),
            pl.BlockSpec((None, tf, d), lambda i, f: (layer, f, 0)),
            pl.BlockSpec((1, d), lambda i, f: (0, 0)),
        ],
        out_specs=pl.BlockSpec((tm, d), lambda i, f: (i, 0)),
        out_shape=jax.ShapeDtypeStruct((m, d), F32),
        scratch_shapes=[pltpu.VMEM((tm, d), BF16), pltpu.VMEM((tm, d), F32)],
        compiler_params=_params(("parallel", "arbitrary"), 48),
        name="mlp_block",
    )(h, g_pre.reshape(1, d), w_up, w_down, g_post.reshape(1, d))


def _ple_kernel(h_ref, p_ref, wg_ref, wp_ref, g_ref, o_ref):
    x = h_ref[...]
    gate = 1.0 / (1.0 + jnp.exp(-_dot(x.astype(BF16), wg_ref[...])))
    e = _dot(p_ref[...].astype(BF16), wp_ref[...])
    o_ref[...] = x + _rms(gate * e, g_ref[...])


def ple_block(h, p, w_gate, w_proj, g, layer, tm=512):
    m, d = h.shape
    pd = p.shape[-1]
    row = lambda i: (i, 0)
    const = lambda i: (0, 0)
    return pl.pallas_call(
        _ple_kernel,
        grid=(m // tm,),
        in_specs=[
            pl.BlockSpec((tm, d), row),
            pl.BlockSpec((None, None, tm, pd), lambda i: (layer, 0, i, 0)),
            pl.BlockSpec((None, d, d), lambda i: (layer, 0, 0)),
            pl.BlockSpec((None, pd, d), lambda i: (layer, 0, 0)),
            pl.BlockSpec((1, d), const),
        ],
        out_specs=pl.BlockSpec((tm, d), row),
        out_shape=jax.ShapeDtypeStruct((m, d), F32),
        compiler_params=_params(("parallel",), 48),
        name="ple_block",
    )(h, p, w_gate, w_proj, g.reshape(1, d))


def _rope_tables(seq):
    rows = seq // GRID_W
    t_row = np.repeat(np.arange(rows), GRID_W).astype(np.float64)
    t_col = np.tile(np.arange(GRID_W), rows).astype(np.float64)
    inv_freq = 1.0 / (ROPE_THETA ** (np.arange(0, ROPE_AXIS_DIM, 2, dtype=np.float64) / ROPE_AXIS_DIM))
    a_row = t_row[:, None] * inv_freq
    a_col = t_col[:, None] * inv_freq
    ang = np.concatenate([a_row, a_row, a_col, a_col], axis=1)
    half = ROPE_AXIS_DIM // 2
    sign = np.where((np.arange(ATTN_HEAD_DIM) % ROPE_AXIS_DIM) < half, -1.0, 1.0)
    return np.cos(ang).astype(np.float32), (np.sin(ang) * sign).astype(np.float32)


def _qk_prep_kernel(x_ref, v_ref, cos_ref, sin_ref, cost_ref, sint_ref, gq_ref, gk_ref, qt_ref, k_ref, vt_ref):
    hd = ATTN_HEAD_DIM
    half = ROPE_AXIS_DIM // 2
    tm = x_ref.shape[0]

    cos_t = cost_ref[...]
    sin_t = sint_ref[...]
    gq = jnp.tile(gq_ref[...], (1, tm // LANES))
    for i in range(ATTN_Q_HEADS):
        sl = slice(i * hd, (i + 1) * hd)
        xt = x_ref[:, sl].T
        y = xt * lax.rsqrt(jnp.mean(xt * xt, axis=0, keepdims=True) + NORM_EPS) * gq
        partner = jnp.concatenate([y[half:2 * half], y[:half], y[3 * half:], y[2 * half:3 * half]], axis=0)
        qt_ref[sl, :] = (y * cos_t + partner * sin_t).astype(BF16)

    cos = cos_ref[...]
    sin = sin_ref[...]
    lane = lax.broadcasted_iota(jnp.int32, cos.shape, 1)
    first_half = (lane % ROPE_AXIS_DIM) < half
    for i in range(ATTN_KV_HEADS):
        src = slice((ATTN_Q_HEADS + i) * hd, (ATTN_Q_HEADS + i + 1) * hd)
        y = _rms(x_ref[:, src], gk_ref[...])
        partner = jnp.where(first_half, pltpu.roll(y, hd - half, 1), pltpu.roll(y, half, 1))
        k_ref[:, i * hd:(i + 1) * hd] = (y * cos + partner * sin).astype(BF16)
        vt_ref[i, :hd, :] = v_ref[:, i * hd:(i + 1) * hd].astype(F32).T.astype(BF16)
        vt_ref[i, hd:, :] = jnp.ones((ATTN_V_ROWS - hd, tm), BF16)


def qk_prep(proj_qk, v, g_q, g_k, tm=512):
    seq = proj_qk.shape[0]
    hd = ATTN_HEAD_DIM
    cos, sin = _rope_tables(seq)
    q_scale = (hd ** -0.5) * LOG2E
    cos_t, sin_t = jnp.asarray(cos.T * q_scale), jnp.asarray(sin.T * q_scale)
    cos, sin = jnp.asarray(cos), jnp.asarray(sin)
    gq_rows = jnp.broadcast_to(g_q.astype(F32)[:, None], (hd, LANES))
    row = lambda i: (i, 0)
    col = lambda i: (0, i)
    const = lambda i: (0, 0)
    return pl.pallas_call(
        _qk_prep_kernel,
        grid=(seq // tm,),
        in_specs=[
            pl.BlockSpec((tm, ATTN_QK), row),
            pl.BlockSpec((tm, ATTN_KV_HEADS * hd), row),
            pl.BlockSpec((tm, hd), row),
            pl.BlockSpec((tm, hd), row),
            pl.BlockSpec((hd, tm), col),
            pl.BlockSpec((hd, tm), col),
            pl.BlockSpec((hd, LANES), const),
            pl.BlockSpec((1, hd), const),
        ],
        out_specs=[
            pl.BlockSpec((ATTN_Q_HEADS * hd, tm), lambda i: (0, i)),
            pl.BlockSpec((tm, ATTN_KV_HEADS * hd), row),
            pl.BlockSpec((ATTN_KV_HEADS, ATTN_V_ROWS, tm), lambda i: (0, 0, i)),
        ],
        out_shape=[
            jax.ShapeDtypeStruct((ATTN_Q_HEADS * hd, seq), BF16),
            jax.ShapeDtypeStruct((seq, ATTN_KV_HEADS * hd), BF16),
            jax.ShapeDtypeStruct((ATTN_KV_HEADS, ATTN_V_ROWS, seq), BF16),
        ],
        compiler_params=_params(("parallel",), 48),
        name="qk_prep",
    )(proj_qk, v, cos, sin, cos_t, sin_t, gq_rows, g_k.reshape(1, hd))


def _flash_kernel(qt_ref, k_ref, vt_ref, o_ref, st_ref, m_ref, acc_ref):
    hd = ATTN_HEAD_DIM
    tk = ATTN_KV_TILE
    qt = jnp.concatenate([qt_ref[i * hd:(i + 1) * hd, :] for i in range(ATTN_GROUP)], axis=1)
    m_ref[...] = jnp.full_like(m_ref, -jnp.inf)
    acc_ref[...] = jnp.zeros_like(acc_ref)

    def scores(j, slot):
        start = pl.multiple_of(j * tk, tk)
        st_ref[slot] = _dot(k_ref[pl.ds(start, tk), :], qt)

    def consume(j, slot):
        start = pl.multiple_of(j * tk, tk)
        st = st_ref[slot]
        m_prev = m_ref[...]
        m_next = jnp.maximum(m_prev, jnp.max(st, axis=0, keepdims=True))
        alpha = jnp.exp2(m_prev - m_next)
        pt = jnp.exp2(st - m_next)
        acc_ref[...] = alpha * acc_ref[...] + _dot(vt_ref[:, pl.ds(start, tk)], pt.astype(BF16))
        m_ref[...] = m_next

    def pair(j, last):
        scores(j + 1, 1)
        consume(j, 0)
        if not last:
            scores(j + 2, 0)
        consume(j + 1, 1)

    npair = k_ref.shape[0] // (2 * tk)
    scores(0, 0)

    def body(jj, carry):
        pair(2 * jj, last=False)
        return carry

    lax.fori_loop(0, npair - 1, body, 0)
    pair(2 * (npair - 1), last=True)
    ot = acc_ref[:hd, :] / acc_ref[hd:hd + 1, :]
    tq = o_ref.shape[0]
    for i in range(ATTN_GROUP):
        o_ref[:, i * hd:(i + 1) * hd] = ot[:, i * tq:(i + 1) * tq].T.astype(o_ref.dtype)


def flash_attention(qt, k, vt, tq=256):
    seq = k.shape[0]
    hd = ATTN_HEAD_DIM
    gw = ATTN_GROUP * hd
    return pl.pallas_call(
        _flash_kernel,
        grid=(ATTN_KV_HEADS, seq // tq),
        in_specs=[
            pl.BlockSpec((gw, tq), lambda g, i: (g, i)),
            pl.BlockSpec((seq, hd), lambda g, i: (0, g)),
            pl.BlockSpec((None, ATTN_V_ROWS, seq), lambda g, i: (g, 0, 0)),
        ],
        out_specs=pl.BlockSpec((tq, gw), lambda g, i: (i, g)),
        out_shape=jax.ShapeDtypeStruct((seq, ATTN_Q_HEADS * hd), BF16),
        scratch_shapes=[
            pltpu.VMEM((2, ATTN_KV_TILE, ATTN_GROUP * tq), F32),
            pltpu.VMEM((1, ATTN_GROUP * tq), F32),
            pltpu.VMEM((ATTN_V_ROWS, ATTN_GROUP * tq), F32),
        ],
        compiler_params=_params(("parallel", "arbitrary"), 48),
        name="flash_attention",
    )(qt, k, vt)


def _chunk_tri(reverse):
    idx = np.arange(GLA_BLOCK)
    same = (idx[:, None] // GLA_CHUNK) == (idx[None, :] // GLA_CHUNK)
    tri = (idx[None, :] >= idx[:, None]) if reverse else (idx[None, :] <= idx[:, None])
    return jnp.asarray((same & tri).astype(np.float32), dtype=BF16)


def _gate_weights(w_gk, first_row):
    w = jnp.zeros((LANES, GLA_DK), F32).at[first_row:first_row + GLA_GATE_RANK].set(w_gk)
    return w.astype(BF16)


def gla_layer(h, g_pre, w_in, w_gk_f, b_gk_f, w_gk_b, b_gk_b, g_head, w_out, g_post):
    w_main = w_in[:, :GLA_MAIN].astype(BF16)
    w_lr = jnp.pad(w_in[:, GLA_MAIN:], ((0, 0), (0, LANES - 2 * GLA_GATE_RANK))).astype(BF16)
    proj, lr = norm_matmul(h, g_pre, w_main, w_lr, tm=256, tn=1024)
    o_f, o_b = gla_scan(proj, lr, w_gk_f, b_gk_f, w_gk_b, b_gk_b)
    return gla_post(o_f, o_b, proj, g_head, w_out.astype(BF16), h, g_post)


def gqa_layer(h, g_pre, w_in, g_q, g_k, w_out, g_post):
    w_in = w_in.astype(BF16)
    proj_qk, v = norm_matmul(h, g_pre, w_in[:, :ATTN_QK], w_in[:, ATTN_QK:], tm=512, tn=ATTN_QK // 2)
    qt, k, vt = qk_prep(proj_qk, v, g_q, g_k)
    o = flash_attention(qt, k, vt)
    seq, d = o.shape
    nblk = seq // ATTN_Q_BLOCK
    o = o.reshape(nblk, ATTN_Q_BLOCK, d).transpose(1, 0, 2).reshape(seq, d)
    return matmul_norm_res(o, w_out.astype(BF16), h, g_post)


def kernel(x, p, g_pre_mix, g_post_mix, g_pre_mlp, g_post_mlp, g_ple, w_mlp_up, w_mlp_down,
           w_ple_proj, w_ple_gate, gla_w_in, gla_w_gk_fwd, gla_b_gk_fwd, gla_w_gk_bwd,
           gla_b_gk_bwd, gla_g_head, gla_w_out, attn_w_in, attn_g_q, attn_g_k, attn_w_out):
    assert x.shape == (1, SEQ, D_MODEL)
    h = x[0]
    w_mlp_up, w_mlp_down = w_mlp_up.astype(BF16), w_mlp_down.astype(BF16)
    w_ple_gate, w_ple_proj = w_ple_gate.astype(BF16), w_ple_proj.astype(BF16)
    for i in range(DEPTH):
        j = i // 2
        if i % 2 == 0:
            h = gla_layer(h, g_pre_mix[i], gla_w_in[j], gla_w_gk_fwd[j], gla_b_gk_fwd[j],
                          gla_w_gk_bwd[j], gla_b_gk_bwd[j], gla_g_head[j], gla_w_out[j], g_post_mix[i])
        else:
            h = gqa_layer(h, g_pre_mix[i], attn_w_in[j], attn_g_q[j], attn_g_k[j], attn_w_out[j],
                          g_post_mix[i])
        h = mlp_block(h, g_pre_mlp[i], w_mlp_up, w_mlp_down, g_post_mlp[i], i)
        h = ple_block(h, p, w_ple_gate, w_ple_proj, g_ple[i], i)
    return h[None]
```

```python
import functools

import numpy as np
import jax
import jax.numpy as jnp
from jax import lax
from jax.experimental import pallas as pl
from jax.experimental.pallas import tpu as pltpu

F32 = jnp.float32
BF16 = jnp.bfloat16

D_MODEL = 2048
SEQ = 8192
DEPTH = 2
GRID_W = 64
PLE_DIM = 256
NORM_EPS = 1e-6
D_FF = 4 * D_MODEL

GLA_HEADS = 4
GLA_DK = D_MODEL // 2
GLA_DV = D_MODEL
GLA_HEAD_K = GLA_DK // GLA_HEADS
GLA_HEAD_V = GLA_DV // GLA_HEADS
GLA_GATE_RANK = 16
GLA_GATE_TAU = 16.0
GLA_CHUNK = 64
GLA_MAIN = 2 * GLA_DK + 2 * GLA_DV
GLA_BLOCK = 256

ATTN_HEAD_DIM = 128
ATTN_Q_HEADS = D_MODEL // ATTN_HEAD_DIM
ATTN_KV_HEADS = 4
ATTN_GROUP = ATTN_Q_HEADS // ATTN_KV_HEADS
ATTN_Q_BLOCK = 128
ATTN_KV_TILE = 1024
ATTN_V_ROWS = ATTN_HEAD_DIM + 16
ATTN_QK = (ATTN_Q_HEADS + ATTN_KV_HEADS) * ATTN_HEAD_DIM
ROPE_THETA = 10000.0
ROPE_AXIS_DIM = ATTN_HEAD_DIM // 2

LANES = 128
MIB = 1024 * 1024
LOG2E = 1.4426950408889634


def _params(semantics, vmem_mib):
    return pltpu.CompilerParams(dimension_semantics=semantics, vmem_limit_bytes=vmem_mib * MIB)


def _rms(x, gain):
    return x * lax.rsqrt(jnp.mean(x * x, axis=-1, keepdims=True) + NORM_EPS) * gain


def _dot(a, b):
    return jnp.dot(a, b, preferred_element_type=F32)


def _dot_nt(a, b):
    return lax.dot_general(a, b, (((1,), (1,)), ((), ())), preferred_element_type=F32)


def _dot_tn(a, b):
    return lax.dot_general(a, b, (((0,), (0,)), ((), ())), preferred_element_type=F32)


def _resident(shape):
    return pl.BlockSpec(shape, lambda *_: (0,) * len(shape), pipeline_mode=pl.Buffered(1))


def _norm_matmul_kernel(h_ref, g_ref, w_ref, o_ref, ox_ref, u_ref, *, tn):
    n = o_ref.shape[1]
    u_ref[...] = _rms(h_ref[...], g_ref[...]).astype(BF16)
    ox_ref[...] = _dot(u_ref[...], w_ref[:, n:]).astype(ox_ref.dtype)
    for c in range(n // tn):
        sl = slice(c * tn, (c + 1) * tn)
        o_ref[:, sl] = _dot(u_ref[...], w_ref[:, sl]).astype(o_ref.dtype)


def norm_matmul(h, g, w, n, tm, tn):
    m, k = h.shape
    nx = w.shape[1] - n
    return pl.pallas_call(
        functools.partial(_norm_matmul_kernel, tn=tn),
        grid=(m // tm,),
        in_specs=[
            pl.BlockSpec((tm, k), lambda i: (i, 0)),
            _resident((1, k)),
            _resident(w.shape),
        ],
        out_specs=[
            pl.BlockSpec((tm, n), lambda i: (i, 0)),
            pl.BlockSpec((tm, nx), lambda i: (i, 0)),
        ],
        out_shape=[jax.ShapeDtypeStruct((m, n), F32), jax.ShapeDtypeStruct((m, nx), BF16)],
        scratch_shapes=[pltpu.VMEM((tm, k), BF16)],
        compiler_params=_params(("parallel",), 56),
        name="norm_matmul",
    )(h, g.reshape(1, k), w)


def _gla_kernel(qf_ref, kf_ref, vf_ref, lrf_ref, qb_ref, kb_ref, vb_ref, lrb_ref, wf_ref, bf_ref, wb_ref, bb_ref,
                trif_ref, trib_ref, of_ref, ob_ref, s_ref, cum_ref):
    c = GLA_CHUNK
    nchunk = GLA_BLOCK // c

    @pl.when(pl.program_id(0) == 0)
    def _():
        s_ref[...] = jnp.zeros_like(s_ref)

    def cumulative(d, lr_ref, w_ref, b_ref, tri_ref):
        z = _dot(lr_ref[...], w_ref[...]) + b_ref[...]
        la = (jnp.minimum(z, 0.0) - jnp.log1p(jnp.exp(-jnp.abs(z)))) * (1.0 / GLA_GATE_TAU)
        la_hi = la.astype(BF16)
        la_lo = (la - la_hi.astype(F32)).astype(BF16)
        tri = tri_ref[...]
        cum_ref[d] = _dot(tri, la_hi) + _dot(tri, la_lo)

    cumulative(0, lrf_ref, wf_ref, bf_ref, trif_ref)
    cumulative(1, lrb_ref, wb_ref, bb_ref, trib_ref)

    row = lax.broadcasted_iota(jnp.int32, (c, c), 0)
    col = lax.broadcasted_iota(jnp.int32, (c, c), 1)
    directions = (
        (qf_ref, kf_ref, vf_ref, of_ref, col <= row, c // 2 - 1, c - 1),
        (qb_ref, kb_ref, vb_ref, ob_ref, col > row, c // 2, 0),
    )
    for step in range(nchunk):
        for d, (q_ref, k_ref, v_ref, o_ref, mask, ref_row, last_row) in enumerate(directions):
            j = nchunk - 1 - step if d else step
            sl = slice(j * c, (j + 1) * c)
            for h in range(GLA_HEADS):
                hk = slice(h * GLA_HEAD_K, (h + 1) * GLA_HEAD_K)
                hv = slice(h * GLA_HEAD_V, (h + 1) * GLA_HEAD_V)
                cumc = cum_ref[d, sl, hk]
                refc = cumc[ref_row:ref_row + 1]
                lastc = cumc[last_row:last_row + 1]
                qc = q_ref[sl, hk] * (GLA_HEAD_K ** -0.5)
                kc = k_ref[sl, hk]
                vc = v_ref[sl, hv].astype(BF16)
                qt = (qc * jnp.exp(cumc - refc)).astype(BF16)
                kt = (kc * jnp.exp(refc - cumc)).astype(BF16)
                qd = (qc * jnp.exp(cumc)).astype(BF16)
                ke = (kc * jnp.exp(lastc - cumc)).astype(BF16)
                scores = jnp.where(mask, _dot_nt(qt, kt), 0.0).astype(BF16)
                state = s_ref[d * GLA_HEADS + h]
                o_ref[sl, hv] = _dot(scores, vc) + _dot_nt(qd, state.astype(BF16))
                s_ref[d * GLA_HEADS + h] = state * jnp.exp(lastc) + _dot_tn(vc, ke)


def gla_scan(proj, lr, w_f, b_f, w_b, b_b):
    seq = proj.shape[0]
    t = GLA_BLOCK
    nb = seq // t
    fwd = lambda i: i
    bwd = lambda i: nb - 1 - i
    const = lambda i: (0, 0)
    v_blk = 2 * GLA_DK // GLA_DV

    def operands(blk):
        return [
            pl.BlockSpec((t, GLA_DK), lambda i: (blk(i), 0)),
            pl.BlockSpec((t, GLA_DK), lambda i: (blk(i), 1)),
            pl.BlockSpec((t, GLA_DV), lambda i: (blk(i), v_blk)),
            pl.BlockSpec((t, 2 * GLA_GATE_RANK), lambda i: (blk(i), 0)),
        ]

    return pl.pallas_call(
        _gla_kernel,
        grid=(nb,),
        in_specs=operands(fwd) + operands(bwd) + [
            pl.BlockSpec((2 * GLA_GATE_RANK, GLA_DK), const),
            pl.BlockSpec((1, GLA_DK), const),
            pl.BlockSpec((2 * GLA_GATE_RANK, GLA_DK), const),
            pl.BlockSpec((1, GLA_DK), const),
            pl.BlockSpec((t, t), const),
            pl.BlockSpec((t, t), const),
        ],
        out_specs=[
            pl.BlockSpec((t, GLA_DV), lambda i: (fwd(i), 0)),
            pl.BlockSpec((t, GLA_DV), lambda i: (bwd(i), 0)),
        ],
        out_shape=[jax.ShapeDtypeStruct((seq, GLA_DV), F32)] * 2,
        scratch_shapes=[
            pltpu.VMEM((2 * GLA_HEADS, GLA_HEAD_V, GLA_HEAD_K), F32),
            pltpu.VMEM((2, t, GLA_DK), F32),
        ],
        compiler_params=_params(("arbitrary",), 48),
        name="gla_scan",
    )(proj, proj, proj, lr, proj, proj, proj, lr,
      _gate_weights(w_f, 0), b_f.reshape(1, GLA_DK), _gate_weights(w_b, GLA_GATE_RANK), b_b.reshape(1, GLA_DK),
      _chunk_tri(False), _chunk_tri(True))


def _gla_post_kernel(of_ref, ob_ref, og_ref, gh_ref, w_ref, h_ref, gp_ref, o_ref, y_ref):
    for hd in range(GLA_HEADS):
        sl = slice(hd * GLA_HEAD_V, (hd + 1) * GLA_HEAD_V)
        o = _rms(of_ref[:, sl] + ob_ref[:, sl], gh_ref[...])
        og = og_ref[:, sl]
        y_ref[:, sl] = (o * (og / (1.0 + jnp.exp(-og)))).astype(BF16)
    mix = _dot(y_ref[...], w_ref[...])
    o_ref[...] = h_ref[...] + _rms(mix, gp_ref[...])


def gla_post(o_f, o_b, proj, g_head, w_out, h, g_post, tm=512):
    m, d = h.shape
    og_blk = (2 * GLA_DK + GLA_DV) // GLA_DV
    row = lambda i: (i, 0)
    return pl.pallas_call(
        _gla_post_kernel,
        grid=(m // tm,),
        in_specs=[
            pl.BlockSpec((tm, GLA_DV), row),
            pl.BlockSpec((tm, GLA_DV), row),
            pl.BlockSpec((tm, GLA_DV), lambda i: (i, og_blk)),
            _resident((1, GLA_HEAD_V)),
            _resident((GLA_DV, d)),
            pl.BlockSpec((tm, d), row),
            _resident((1, d)),
        ],
        out_specs=pl.BlockSpec((tm, d), row),
        out_shape=jax.ShapeDtypeStruct((m, d), F32),
        scratch_shapes=[pltpu.VMEM((tm, GLA_DV), BF16)],
        compiler_params=_params(("parallel",), 56),
        name="gla_post",
    )(o_f, o_b, proj, g_head.reshape(1, GLA_HEAD_V), w_out, h, g_post.reshape(1, d))


def _matmul_norm_res_kernel(a_ref, w_ref, h_ref, g_ref, o_ref):
    o_ref[...] = h_ref[...] + _rms(_dot(a_ref[...], w_ref[...]), g_ref[...])


def matmul_norm_res(a, w, h, g, tm=512):
    m, d = h.shape
    k = a.shape[1]
    row = lambda i: (i, 0)
    return pl.pallas_call(
        _matmul_norm_res_kernel,
        grid=(m // tm,),
        in_specs=[
            pl.BlockSpec((tm, k), row),
            _resident((k, d)),
            pl.BlockSpec((tm, d), row),
            _resident((1, d)),
        ],
        out_specs=pl.BlockSpec((tm, d), row),
        out_shape=jax.ShapeDtypeStruct((m, d), F32),
        compiler_params=_params(("parallel",), 48),
        name="matmul_norm_res",
    )(a, w, h, g.reshape(1, d))


def _mlp_kernel(h_ref, g1_ref, wu_ref, wd_ref, g2_ref, o_ref, u_ref):
    f = pl.program_id(1)

    @pl.when(f == 0)
    def _():
        u_ref[...] = _rms(h_ref[...], g1_ref[...]).astype(BF16)
        o_ref[...] = jnp.zeros_like(o_ref)

    hid = jnp.maximum(_dot(u_ref[...], wu_ref[...].astype(BF16)), 0.0)
    o_ref[...] += _dot((hid * hid).astype(BF16), wd_ref[...].astype(BF16))

    @pl.when(f == pl.num_programs(1) - 1)
    def _():
        o_ref[...] = h_ref[...] + _rms(o_ref[...], g2_ref[...])


def mlp_block(h, g_pre, w_up, w_down, g_post, layer, tm=1024, tf=512):
    m, d = h.shape
    ff = w_up.shape[2]
    return pl.pallas_call(
        _mlp_kernel,
        grid=(m // tm, ff // tf),
        in_specs=[
            pl.BlockSpec((tm, d), lambda i, f: (i, 0), pipeline_mode=pl.Buffered(1)),
            _resident((1, d)),
            pl.BlockSpec((None, d, tf), lambda i, f: (layer, 0, f)),
            pl.BlockSpec((None, tf, d), lambda i, f: (layer, f, 0)),
            _resident((1, d)),
        ],
        out_specs=pl.BlockSpec((tm, d), lambda i, f: (i, 0)),
        out_shape=jax.ShapeDtypeStruct((m, d), F32),
        scratch_shapes=[pltpu.VMEM((tm, d), BF16)],
        compiler_params=_params(("parallel", "arbitrary"), 58),
        name="mlp_block",
    )(h, g_pre.reshape(1, d), w_up, w_down, g_post.reshape(1, d))


def _ple_kernel(h_ref, p_ref, wg_ref, wp_ref, g_ref, o_ref):
    x = h_ref[...]
    gate = 1.0 / (1.0 + jnp.exp(-_dot(x.astype(BF16), wg_ref[...])))
    e = _dot(p_ref[...].astype(BF16), wp_ref[...])
    o_ref[...] = x + _rms(gate * e, g_ref[...])


def ple_block(h, p, w_gate, w_proj, g, layer, tm=512):
    m, d = h.shape
    pd = p.shape[-1]
    row = lambda i: (i, 0)
    const = lambda i: (0, 0)
    return pl.pallas_call(
        _ple_kernel,
        grid=(m // tm,),
        in_specs=[
            pl.BlockSpec((tm, d), row),
            pl.BlockSpec((None, None, tm, pd), lambda i: (layer, 0, i, 0)),
            pl.BlockSpec((None, d, d), lambda i: (layer, 0, 0)),
            pl.BlockSpec((None, pd, d), lambda i: (layer, 0, 0)),
            pl.BlockSpec((1, d), const),
        ],
        out_specs=pl.BlockSpec((tm, d), row),
        out_shape=jax.ShapeDtypeStruct((m, d), F32),
        compiler_params=_params(("parallel",), 48),
        name="ple_block",
    )(h, p, w_gate, w_proj, g.reshape(1, d))


def _rope_tables(seq):
    rows = seq // GRID_W
    t_row = np.repeat(np.arange(rows), GRID_W).astype(np.float64)
    t_col = np.tile(np.arange(GRID_W), rows).astype(np.float64)
    inv_freq = 1.0 / (ROPE_THETA ** (np.arange(0, ROPE_AXIS_DIM, 2, dtype=np.float64) / ROPE_AXIS_DIM))
    a_row = t_row[:, None] * inv_freq
    a_col = t_col[:, None] * inv_freq
    ang = np.concatenate([a_row, a_row, a_col, a_col], axis=1)
    half = ROPE_AXIS_DIM // 2
    sign = np.where((np.arange(ATTN_HEAD_DIM) % ROPE_AXIS_DIM) < half, -1.0, 1.0)
    return np.cos(ang).astype(np.float32), (np.sin(ang) * sign).astype(np.float32)


def _qk_prep_kernel(x_ref, v_ref, cos_ref, sin_ref, cost_ref, sint_ref, gq_ref, gk_ref, qt_ref, k_ref, vt_ref):
    hd = ATTN_HEAD_DIM
    half = ROPE_AXIS_DIM // 2
    tm = x_ref.shape[0]

    cos_t = cost_ref[...]
    sin_t = sint_ref[...]
    gq = jnp.tile(gq_ref[...], (1, tm // LANES))
    for i in range(ATTN_Q_HEADS):
        sl = slice(i * hd, (i + 1) * hd)
        xt = x_ref[:, sl].T
        y = xt * lax.rsqrt(jnp.mean(xt * xt, axis=0, keepdims=True) + NORM_EPS) * gq
        partner = jnp.concatenate([y[half:2 * half], y[:half], y[3 * half:], y[2 * half:3 * half]], axis=0)
        qt_ref[sl, :] = (y * cos_t + partner * sin_t).astype(BF16)

    cos = cos_ref[...]
    sin = sin_ref[...]
    lane = lax.broadcasted_iota(jnp.int32, cos.shape, 1)
    first_half = (lane % ROPE_AXIS_DIM) < half
    for i in range(ATTN_KV_HEADS):
        src = slice((ATTN_Q_HEADS + i) * hd, (ATTN_Q_HEADS + i + 1) * hd)
        y = _rms(x_ref[:, src], gk_ref[...])
        partner = jnp.where(first_half, pltpu.roll(y, hd - half, 1), pltpu.roll(y, half, 1))
        k_ref[:, i * hd:(i + 1) * hd] = (y * cos + partner * sin).astype(BF16)
        vt_ref[i, :hd, :] = v_ref[:, i * hd:(i + 1) * hd].astype(F32).T.astype(BF16)
        vt_ref[i, hd:, :] = jnp.ones((ATTN_V_ROWS - hd, tm), BF16)


def qk_prep(proj_qk, v, g_q, g_k, tm=512):
    seq = proj_qk.shape[0]
    hd = ATTN_HEAD_DIM
    cos, sin = _rope_tables(seq)
    q_scale = (hd ** -0.5) * LOG2E
    cos_t, sin_t = jnp.asarray(cos.T * q_scale), jnp.asarray(sin.T * q_scale)
    cos, sin = jnp.asarray(cos), jnp.asarray(sin)
    gq_rows = jnp.broadcast_to(g_q.astype(F32)[:, None], (hd, LANES))
    row = lambda i: (i, 0)
    col = lambda i: (0, i)
    const = lambda i: (0, 0)
    return pl.pallas_call(
        _qk_prep_kernel,
        grid=(seq // tm,),
        in_specs=[
            pl.BlockSpec((tm, ATTN_QK), row),
            pl.BlockSpec((tm, ATTN_KV_HEADS * hd), row),
            pl.BlockSpec((tm, hd), row),
            pl.BlockSpec((tm, hd), row),
            pl.BlockSpec((hd, tm), col),
            pl.BlockSpec((hd, tm), col),
            pl.BlockSpec((hd, LANES), const),
            pl.BlockSpec((1, hd), const),
        ],
        out_specs=[
            pl.BlockSpec((ATTN_Q_HEADS * hd, tm), lambda i: (0, i)),
            pl.BlockSpec((tm, ATTN_KV_HEADS * hd), row),
            pl.BlockSpec((ATTN_KV_HEADS, ATTN_V_ROWS, tm), lambda i: (0, 0, i)),
        ],
        out_shape=[
            jax.ShapeDtypeStruct((ATTN_Q_HEADS * hd, seq), BF16),
            jax.ShapeDtypeStruct((seq, ATTN_KV_HEADS * hd), BF16),
            jax.ShapeDtypeStruct((ATTN_KV_HEADS, ATTN_V_ROWS, seq), BF16),
        ],
        compiler_params=_params(("parallel",), 48),
        name="qk_prep",
    )(proj_qk, v, cos, sin, cos_t, sin_t, gq_rows, g_k.reshape(1, hd))


def _flash_kernel(qt_ref, k_ref, vt_ref, o_ref, st_ref, m_ref, acc_ref):
    hd = ATTN_HEAD_DIM
    tk = ATTN_KV_TILE
    qt = jnp.concatenate([qt_ref[i * hd:(i + 1) * hd, :] for i in range(ATTN_GROUP)], axis=1)
    m_ref[...] = jnp.full_like(m_ref, -jnp.inf)
    acc_ref[...] = jnp.zeros_like(acc_ref)

    def scores(j, slot):
        start = pl.multiple_of(j * tk, tk)
        st_ref[slot] = _dot(k_ref[pl.ds(start, tk), :], qt)

    def consume(j, slot):
        start = pl.multiple_of(j * tk, tk)
        st = st_ref[slot]
        m_prev = m_ref[...]
        m_next = jnp.maximum(m_prev, jnp.max(st, axis=0, keepdims=True))
        alpha = jnp.exp2(m_prev - m_next)
        pt = jnp.exp2(st - m_next)
        acc_ref[...] = alpha * acc_ref[...] + _dot(vt_ref[:, pl.ds(start, tk)], pt.astype(BF16))
        m_ref[...] = m_next

    def pair(j, last):
        scores(j + 1, 1)
        consume(j, 0)
        if not last:
            scores(j + 2, 0)
        consume(j + 1, 1)

    npair = k_ref.shape[0] // (2 * tk)
    scores(0, 0)

    def body(jj, carry):
        pair(2 * jj, last=False)
        return carry

    lax.fori_loop(0, npair - 1, body, 0)
    pair(2 * (npair - 1), last=True)
    ot = acc_ref[:hd, :] / acc_ref[hd:hd + 1, :]
    tq = o_ref.shape[0]
    for i in range(ATTN_GROUP):
        o_ref[:, i * hd:(i + 1) * hd] = ot[:, i * tq:(i + 1) * tq].T.astype(o_ref.dtype)


def flash_attention(qt, k, vt, tq=256):
    seq = k.shape[0]
    hd = ATTN_HEAD_DIM
    gw = ATTN_GROUP * hd
    return pl.pallas_call(
        _flash_kernel,
        grid=(ATTN_KV_HEADS, seq // tq),
        in_specs=[
            pl.BlockSpec((gw, tq), lambda g, i: (g, i)),
            pl.BlockSpec((seq, hd), lambda g, i: (0, g)),
            pl.BlockSpec((None, ATTN_V_ROWS, seq), lambda g, i: (g, 0, 0)),
        ],
        out_specs=pl.BlockSpec((tq, gw), lambda g, i: (i, g)),
        out_shape=jax.ShapeDtypeStruct((seq, ATTN_Q_HEADS * hd), BF16),
        scratch_shapes=[
            pltpu.VMEM((2, ATTN_KV_TILE, ATTN_GROUP * tq), F32),
            pltpu.VMEM((1, ATTN_GROUP * tq), F32),
            pltpu.VMEM((ATTN_V_ROWS, ATTN_GROUP * tq), F32),
        ],
        compiler_params=_params(("parallel", "arbitrary"), 48),
        name="flash_attention",
    )(qt, k, vt)


def _chunk_tri(reverse):
    idx = np.arange(GLA_BLOCK)
    same = (idx[:, None] // GLA_CHUNK) == (idx[None, :] // GLA_CHUNK)
    tri = (idx[None, :] >= idx[:, None]) if reverse else (idx[None, :] <= idx[:, None])
    return jnp.asarray((same & tri).astype(np.float32), dtype=BF16)


def _gate_weights(w_gk, first_row):
    w = jnp.zeros((2 * GLA_GATE_RANK, GLA_DK), F32).at[first_row:first_row + GLA_GATE_RANK].set(w_gk)
    return w.astype(BF16)


def gla_layer(h, g_pre, w_in, w_gk_f, b_gk_f, w_gk_b, b_gk_b, g_head, w_out, g_post):
    proj, lr = norm_matmul(h, g_pre, w_in.astype(BF16), GLA_MAIN, tm=256, tn=1024)
    o_f, o_b = gla_scan(proj, lr, w_gk_f, b_gk_f, w_gk_b, b_gk_b)
    return gla_post(o_f, o_b, proj, g_head, w_out.astype(BF16), h, g_post)


def gqa_layer(h, g_pre, w_in, g_q, g_k, w_out, g_post):
    proj_qk, v = norm_matmul(h, g_pre, w_in.astype(BF16), ATTN_QK, tm=512, tn=ATTN_QK // 2)
    qt, k, vt = qk_prep(proj_qk, v, g_q, g_k)
    o = flash_attention(qt, k, vt)
    seq, d = o.shape
    nblk = seq // ATTN_Q_BLOCK
    o = o.reshape(nblk, ATTN_Q_BLOCK, d).transpose(1, 0, 2).reshape(seq, d)
    return matmul_norm_res(o, w_out.astype(BF16), h, g_post)


def kernel(x, p, g_pre_mix, g_post_mix, g_pre_mlp, g_post_mlp, g_ple, w_mlp_up, w_mlp_down,
           w_ple_proj, w_ple_gate, gla_w_in, gla_w_gk_fwd, gla_b_gk_fwd, gla_w_gk_bwd,
           gla_b_gk_bwd, gla_g_head, gla_w_out, attn_w_in, attn_g_q, attn_g_k, attn_w_out):
    assert x.shape == (1, SEQ, D_MODEL)
    h = x[0]
    w_ple_gate, w_ple_proj = w_ple_gate.astype(BF16), w_ple_proj.astype(BF16)
    for i in range(DEPTH):
        j = i // 2
        if i % 2 == 0:
            h = gla_layer(h, g_pre_mix[i], gla_w_in[j], gla_w_gk_fwd[j], gla_b_gk_fwd[j],
                          gla_w_gk_bwd[j], gla_b_gk_bwd[j], gla_g_head[j], gla_w_out[j], g_post_mix[i])
        else:
            h = gqa_layer(h, g_pre_mix[i], attn_w_in[j], attn_g_q[j], attn_g_k[j], attn_w_out[j],
                          g_post_mix[i])
        h = mlp_block(h, g_pre_mlp[i], w_mlp_up, w_mlp_down, g_post_mlp[i], i)
        h = ple_block(h, p, w_ple_gate, w_ple_proj, g_ple[i], i)
    return h[None]
```

```python
import functools

import numpy as np
import jax
import jax.numpy as jnp
from jax import lax
from jax.experimental import pallas as pl
from jax.experimental.pallas import tpu as pltpu

F32 = jnp.float32
BF16 = jnp.bfloat16

D_MODEL = 2048
SEQ = 8192
DEPTH = 2
GRID_W = 64
PLE_DIM = 256
NORM_EPS = 1e-6
D_FF = 4 * D_MODEL

GLA_HEADS = 4
GLA_DK = D_MODEL // 2
GLA_DV = D_MODEL
GLA_HEAD_K = GLA_DK // GLA_HEADS
GLA_HEAD_V = GLA_DV // GLA_HEADS
GLA_GATE_RANK = 16
GLA_GATE_TAU = 16.0
GLA_CHUNK = 64
GLA_MAIN = 2 * GLA_DK + 2 * GLA_DV
GLA_BLOCK = 256

ATTN_HEAD_DIM = 128
ATTN_Q_HEADS = D_MODEL // ATTN_HEAD_DIM
ATTN_KV_HEADS = 4
ATTN_GROUP = ATTN_Q_HEADS // ATTN_KV_HEADS
ATTN_Q_BLOCK = 128
ATTN_KV_TILE = 1024
ATTN_V_ROWS = ATTN_HEAD_DIM + 16
ROPE_THETA = 10000.0
ROPE_AXIS_DIM = ATTN_HEAD_DIM // 2

LANES = 128
MIB = 1024 * 1024
LOG2E = 1.4426950408889634


def _params(semantics, vmem_mib):
    return pltpu.CompilerParams(dimension_semantics=semantics, vmem_limit_bytes=vmem_mib * MIB)


def _rms(x, gain):
    return x * lax.rsqrt(jnp.mean(x * x, axis=-1, keepdims=True) + NORM_EPS) * gain


def _dot(a, b):
    return jnp.dot(a, b, preferred_element_type=F32)


def _dot_nt(a, b):
    return lax.dot_general(a, b, (((1,), (1,)), ((), ())), preferred_element_type=F32)


def _dot_tn(a, b):
    return lax.dot_general(a, b, (((0,), (0,)), ((), ())), preferred_element_type=F32)


def _resident(shape):
    return pl.BlockSpec(shape, lambda *_: (0,) * len(shape), pipeline_mode=pl.Buffered(1))


def _norm_matmul_kernel(h_ref, g_ref, w_ref, o_ref, ox_ref, u_ref, *, tn):
    n = o_ref.shape[1]
    u_ref[...] = _rms(h_ref[...], g_ref[...]).astype(BF16)
    ox_ref[...] = _dot(u_ref[...], w_ref[:, n:]).astype(ox_ref.dtype)
    for c in range(n // tn):
        sl = slice(c * tn, (c + 1) * tn)
        o_ref[:, sl] = _dot(u_ref[...], w_ref[:, sl]).astype(o_ref.dtype)


def norm_matmul(h, g, w, n, tm, tn):
    m, k = h.shape
    nx = w.shape[1] - n
    return pl.pallas_call(
        functools.partial(_norm_matmul_kernel, tn=tn),
        grid=(m // tm,),
        in_specs=[
            pl.BlockSpec((tm, k), lambda i: (i, 0)),
            _resident((1, k)),
            _resident(w.shape),
        ],
        out_specs=[
            pl.BlockSpec((tm, n), lambda i: (i, 0)),
            pl.BlockSpec((tm, nx), lambda i: (i, 0)),
        ],
        out_shape=[jax.ShapeDtypeStruct((m, n), F32), jax.ShapeDtypeStruct((m, nx), BF16)],
        scratch_shapes=[pltpu.VMEM((tm, k), BF16)],
        compiler_params=_params(("parallel",), 56),
        name="norm_matmul",
    )(h, g.reshape(1, k), w)


def _gla_kernel(qf_ref, kf_ref, vf_ref, lrf_ref, qb_ref, kb_ref, vb_ref, lrb_ref, wf_ref, bf_ref, wb_ref, bb_ref,
                trif_ref, trib_ref, of_ref, ob_ref, s_ref, cum_ref):
    c = GLA_CHUNK
    nchunk = GLA_BLOCK // c

    @pl.when(pl.program_id(0) == 0)
    def _():
        s_ref[...] = jnp.zeros_like(s_ref)

    def cumulative(d, lr_ref, w_ref, b_ref, tri_ref):
        z = _dot(lr_ref[...], w_ref[...]) + b_ref[...]
        la = (jnp.minimum(z, 0.0) - jnp.log1p(jnp.exp(-jnp.abs(z)))) * (1.0 / GLA_GATE_TAU)
        la_hi = la.astype(BF16)
        la_lo = (la - la_hi.astype(F32)).astype(BF16)
        tri = tri_ref[...]
        cum_ref[d] = _dot(tri, la_hi) + _dot(tri, la_lo)

    cumulative(0, lrf_ref, wf_ref, bf_ref, trif_ref)
    cumulative(1, lrb_ref, wb_ref, bb_ref, trib_ref)

    row = lax.broadcasted_iota(jnp.int32, (c, c), 0)
    col = lax.broadcasted_iota(jnp.int32, (c, c), 1)
    directions = (
        (qf_ref, kf_ref, vf_ref, of_ref, col <= row, c // 2 - 1, c - 1),
        (qb_ref, kb_ref, vb_ref, ob_ref, col > row, c // 2, 0),
    )
    for step in range(nchunk):
        for d, (q_ref, k_ref, v_ref, o_ref, mask, ref_row, last_row) in enumerate(directions):
            j = nchunk - 1 - step if d else step
            sl = slice(j * c, (j + 1) * c)
            for h in range(GLA_HEADS):
                hk = slice(h * GLA_HEAD_K, (h + 1) * GLA_HEAD_K)
                hv = slice(h * GLA_HEAD_V, (h + 1) * GLA_HEAD_V)
                cumc = cum_ref[d, sl, hk]
                refc = cumc[ref_row:ref_row + 1]
                lastc = cumc[last_row:last_row + 1]
                qc = q_ref[sl, hk] * (GLA_HEAD_K ** -0.5)
                kc = k_ref[sl, hk]
                vc = v_ref[sl, hv].astype(BF16)
                qt = (qc * jnp.exp(cumc - refc)).astype(BF16)
                kt = (kc * jnp.exp(refc - cumc)).astype(BF16)
                qd = (qc * jnp.exp(cumc)).astype(BF16)
                ke = (kc * jnp.exp(lastc - cumc)).astype(BF16)
                scores = jnp.where(mask, _dot_nt(qt, kt), 0.0).astype(BF16)
                state = s_ref[d * GLA_HEADS + h]
                o_ref[sl, hv] = _dot(scores, vc) + _dot_nt(qd, state.astype(BF16))
                s_ref[d * GLA_HEADS + h] = state * jnp.exp(lastc) + _dot_tn(vc, ke)


def gla_scan(proj, lr, w_f, b_f, w_b, b_b):
    seq = proj.shape[0]
    t = GLA_BLOCK
    nb = seq // t
    fwd = lambda i: i
    bwd = lambda i: nb - 1 - i
    const = lambda i: (0, 0)
    v_blk = 2 * GLA_DK // GLA_DV

    def operands(blk):
        return [
            pl.BlockSpec((t, GLA_DK), lambda i: (blk(i), 0)),
            pl.BlockSpec((t, GLA_DK), lambda i: (blk(i), 1)),
            pl.BlockSpec((t, GLA_DV), lambda i: (blk(i), v_blk)),
            pl.BlockSpec((t, 2 * GLA_GATE_RANK), lambda i: (blk(i), 0)),
        ]

    return pl.pallas_call(
        _gla_kernel,
        grid=(nb,),
        in_specs=operands(fwd) + operands(bwd) + [
            pl.BlockSpec((2 * GLA_GATE_RANK, GLA_DK), const),
            pl.BlockSpec((1, GLA_DK), const),
            pl.BlockSpec((2 * GLA_GATE_RANK, GLA_DK), const),
            pl.BlockSpec((1, GLA_DK), const),
            pl.BlockSpec((t, t), const),
            pl.BlockSpec((t, t), const),
        ],
        out_specs=[
            pl.BlockSpec((t, GLA_DV), lambda i: (fwd(i), 0)),
            pl.BlockSpec((t, GLA_DV), lambda i: (bwd(i), 0)),
        ],
        out_shape=[jax.ShapeDtypeStruct((seq, GLA_DV), F32)] * 2,
        scratch_shapes=[
            pltpu.VMEM((2 * GLA_HEADS, GLA_HEAD_V, GLA_HEAD_K), F32),
            pltpu.VMEM((2, t, GLA_DK), F32),
        ],
        compiler_params=_params(("arbitrary",), 48),
        name="gla_scan",
    )(proj, proj, proj, lr, proj, proj, proj, lr,
      _gate_weights(w_f, 0), b_f.reshape(1, GLA_DK), _gate_weights(w_b, GLA_GATE_RANK), b_b.reshape(1, GLA_DK),
      _chunk_tri(False), _chunk_tri(True))


def _gla_post_kernel(of_ref, ob_ref, og_ref, gh_ref, w_ref, h_ref, gp_ref, o_ref, y_ref):
    for hd in range(GLA_HEADS):
        sl = slice(hd * GLA_HEAD_V, (hd + 1) * GLA_HEAD_V)
        o = _rms(of_ref[:, sl] + ob_ref[:, sl], gh_ref[...])
        og = og_ref[:, sl]
        y_ref[:, sl] = (o * (og / (1.0 + jnp.exp(-og)))).astype(BF16)
    mix = _dot(y_ref[...], w_ref[...])
    o_ref[...] = h_ref[...] + _rms(mix, gp_ref[...])


def gla_post(o_f, o_b, proj, g_head, w_out, h, g_post, tm=512):
    m, d = h.shape
    og_blk = (2 * GLA_DK + GLA_DV) // GLA_DV
    row = lambda i: (i, 0)
    return pl.pallas_call(
        _gla_post_kernel,
        grid=(m // tm,),
        in_specs=[
            pl.BlockSpec((tm, GLA_DV), row),
            pl.BlockSpec((tm, GLA_DV), row),
            pl.BlockSpec((tm, GLA_DV), lambda i: (i, og_blk)),
            _resident((1, GLA_HEAD_V)),
            _resident((GLA_DV, d)),
            pl.BlockSpec((tm, d), row),
            _resident((1, d)),
        ],
        out_specs=pl.BlockSpec((tm, d), row),
        out_shape=jax.ShapeDtypeStruct((m, d), F32),
        scratch_shapes=[pltpu.VMEM((tm, GLA_DV), BF16)],
        compiler_params=_params(("parallel",), 56),
        name="gla_post",
    )(o_f, o_b, proj, g_head.reshape(1, GLA_HEAD_V), w_out, h, g_post.reshape(1, d))


def _matmul_norm_res_kernel(a_ref, w_ref, h_ref, g_ref, o_ref, w_bf_ref):
    @pl.when(pl.program_id(0) == 0)
    def _():
        w_bf_ref[...] = w_ref[...].astype(BF16)

    o_ref[...] = h_ref[...] + _rms(_dot(a_ref[...], w_bf_ref[...]), g_ref[...])


def matmul_norm_res(a, w, h, g, tm=512):
    m, d = h.shape
    k = a.shape[1]
    row = lambda i: (i, 0)
    return pl.pallas_call(
        _matmul_norm_res_kernel,
        grid=(m // tm,),
        in_specs=[
            pl.BlockSpec((tm, k), row),
            _resident((k, d)),
            pl.BlockSpec((tm, d), row),
            _resident((1, d)),
        ],
        out_specs=pl.BlockSpec((tm, d), row),
        out_shape=jax.ShapeDtypeStruct((m, d), F32),
        scratch_shapes=[pltpu.VMEM((k, d), BF16)],
        compiler_params=_params(("arbitrary",), 56),
        name="matmul_norm_res",
    )(a, w, h, g.reshape(1, d))


def _mlp_kernel(h_ref, g1_ref, wu_ref, wd_ref, g2_ref, o_ref, u_ref):
    f = pl.program_id(1)

    @pl.when(f == 0)
    def _():
        u_ref[...] = _rms(h_ref[...], g1_ref[...]).astype(BF16)
        o_ref[...] = jnp.zeros_like(o_ref)

    hid = jnp.maximum(_dot(u_ref[...], wu_ref[...].astype(BF16)), 0.0)
    o_ref[...] += _dot((hid * hid).astype(BF16), wd_ref[...].astype(BF16))

    @pl.when(f == pl.num_programs(1) - 1)
    def _():
        o_ref[...] = h_ref[...] + _rms(o_ref[...], g2_ref[...])


def mlp_block(h, g_pre, w_up, w_down, g_post, layer, tm=1024, tf=512):
    m, d = h.shape
    ff = w_up.shape[2]
    return pl.pallas_call(
        _mlp_kernel,
        grid=(m // tm, ff // tf),
        in_specs=[
            pl.BlockSpec((tm, d), lambda i, f: (i, 0), pipeline_mode=pl.Buffered(1)),
            _resident((1, d)),
            pl.BlockSpec((None, d, tf), lambda i, f: (layer, 0, f)),
            pl.BlockSpec((None, tf, d), lambda i, f: (layer, f, 0)),
            _resident((1, d)),
        ],
        out_specs=pl.BlockSpec((tm, d), lambda i, f: (i, 0)),
        out_shape=jax.ShapeDtypeStruct((m, d), F32),
        scratch_shapes=[pltpu.VMEM((tm, d), BF16)],
        compiler_params=_params(("parallel", "arbitrary"), 58),
        name="mlp_block",
    )(h, g_pre.reshape(1, d), w_up, w_down, g_post.reshape(1, d))


def _ple_kernel(h_ref, p_ref, wg_ref, wp_ref, g_ref, o_ref, wg_bf_ref, wp_bf_ref):
    @pl.when(pl.program_id(0) == 0)
    def _():
        wg_bf_ref[...] = wg_ref[...].astype(BF16)
        wp_bf_ref[...] = wp_ref[...].astype(BF16)

    x = h_ref[...]
    gate = 1.0 / (1.0 + jnp.exp(-_dot(x.astype(BF16), wg_bf_ref[...])))
    e = _dot(p_ref[...].astype(BF16), wp_bf_ref[...])
    o_ref[...] = x + _rms(gate * e, g_ref[...])


def ple_block(h, p, w_gate, w_proj, g, layer, tm=512):
    m, d = h.shape
    pd = p.shape[-1]
    row = lambda i: (i, 0)
    once = pl.Buffered(1)
    return pl.pallas_call(
        _ple_kernel,
        grid=(m // tm,),
        in_specs=[
            pl.BlockSpec((tm, d), row),
            pl.BlockSpec((None, None, tm, pd), lambda i: (layer, 0, i, 0)),
            pl.BlockSpec((None, d, d), lambda i: (layer, 0, 0), pipeline_mode=once),
            pl.BlockSpec((None, pd, d), lambda i: (layer, 0, 0), pipeline_mode=once),
            _resident((1, d)),
        ],
        out_specs=pl.BlockSpec((tm, d), row),
        out_shape=jax.ShapeDtypeStruct((m, d), F32),
        scratch_shapes=[pltpu.VMEM((d, d), BF16), pltpu.VMEM((pd, d), BF16)],
        compiler_params=_params(("arbitrary",), 56),
        name="ple_block",
    )(h, p, w_gate, w_proj, g.reshape(1, d))


def _rope_tables(seq):
    rows = seq // GRID_W
    t_row = np.repeat(np.arange(rows), GRID_W).astype(np.float64)
    t_col = np.tile(np.arange(GRID_W), rows).astype(np.float64)
    inv_freq = 1.0 / (ROPE_THETA ** (np.arange(0, ROPE_AXIS_DIM, 2, dtype=np.float64) / ROPE_AXIS_DIM))
    a_row = t_row[:, None] * inv_freq
    a_col = t_col[:, None] * inv_freq
    ang = np.concatenate([a_row, a_row, a_col, a_col], axis=1)
    half = ROPE_AXIS_DIM // 2
    sign = np.where((np.arange(ATTN_HEAD_DIM) % ROPE_AXIS_DIM) < half, -1.0, 1.0)
    return np.cos(ang).astype(np.float32), (np.sin(ang) * sign).astype(np.float32)


def _attn_proj_kernel(h_ref, g_ref, w_ref, cos_ref, sin_ref, cost_ref, sint_ref, gq_ref, gk_ref,
                      qt_ref, k_ref, vt_ref, u_ref):
    hd = ATTN_HEAD_DIM
    half = ROPE_AXIS_DIM // 2
    tm = h_ref.shape[0]
    per_dot = ATTN_KV_HEADS
    u_ref[...] = _rms(h_ref[...], g_ref[...]).astype(BF16)

    def project(first_head):
        return _dot(u_ref[...], w_ref[:, first_head * hd:(first_head + per_dot) * hd])

    cos_t = cost_ref[...]
    sin_t = sint_ref[...]
    gq = jnp.tile(gq_ref[...], (1, tm // LANES))
    for c in range(ATTN_Q_HEADS // per_dot):
        x = project(c * per_dot)
        for i in range(per_dot):
            xt = x[:, i * hd:(i + 1) * hd].T
            y = xt * lax.rsqrt(jnp.mean(xt * xt, axis=0, keepdims=True) + NORM_EPS) * gq
            partner = jnp.concatenate([y[half:2 * half], y[:half], y[3 * half:], y[2 * half:3 * half]], axis=0)
            head = c * per_dot + i
            qt_ref[head * hd:(head + 1) * hd, :] = (y * cos_t + partner * sin_t).astype(BF16)

    cos = cos_ref[...]
    sin = sin_ref[...]
    lane = lax.broadcasted_iota(jnp.int32, cos.shape, 1)
    first_half = (lane % ROPE_AXIS_DIM) < half
    xk = project(ATTN_Q_HEADS)
    xv = project(ATTN_Q_HEADS + ATTN_KV_HEADS)
    for i in range(ATTN_KV_HEADS):
        sl = slice(i * hd, (i + 1) * hd)
        y = _rms(xk[:, sl], gk_ref[...])
        partner = jnp.where(first_half, pltpu.roll(y, hd - half, 1), pltpu.roll(y, half, 1))
        k_ref[:, sl] = (y * cos + partner * sin).astype(BF16)
        vt_ref[i, :hd, :] = xv[:, sl].T.astype(BF16)
        vt_ref[i, hd:, :] = jnp.ones((ATTN_V_ROWS - hd, tm), BF16)


def attn_proj(h, g_pre, w, g_q, g_k, tm=512):
    seq, d = h.shape
    hd = ATTN_HEAD_DIM
    cos, sin = _rope_tables(seq)
    q_scale = (hd ** -0.5) * LOG2E
    cos_t, sin_t = jnp.asarray(cos.T * q_scale), jnp.asarray(sin.T * q_scale)
    cos, sin = jnp.asarray(cos), jnp.asarray(sin)
    gq_rows = jnp.broadcast_to(g_q.astype(F32)[:, None], (hd, LANES))
    row = lambda i: (i, 0)
    col = lambda i: (0, i)
    return pl.pallas_call(
        _attn_proj_kernel,
        grid=(seq // tm,),
        in_specs=[
            pl.BlockSpec((tm, d), row),
            _resident((1, d)),
            _resident(w.shape),
            pl.BlockSpec((tm, hd), row),
            pl.BlockSpec((tm, hd), row),
            pl.BlockSpec((hd, tm), col),
            pl.BlockSpec((hd, tm), col),
            _resident((hd, LANES)),
            _resident((1, hd)),
        ],
        out_specs=[
            pl.BlockSpec((ATTN_Q_HEADS * hd, tm), col),
            pl.BlockSpec((tm, ATTN_KV_HEADS * hd), row),
            pl.BlockSpec((ATTN_KV_HEADS, ATTN_V_ROWS, tm), lambda i: (0, 0, i)),
        ],
        out_shape=[
            jax.ShapeDtypeStruct((ATTN_Q_HEADS * hd, seq), BF16),
            jax.ShapeDtypeStruct((seq, ATTN_KV_HEADS * hd), BF16),
            jax.ShapeDtypeStruct((ATTN_KV_HEADS, ATTN_V_ROWS, seq), BF16),
        ],
        scratch_shapes=[pltpu.VMEM((tm, d), BF16)],
        compiler_params=_params(("parallel",), 48),
        name="attn_proj",
    )(h, g_pre.reshape(1, d), w, cos, sin, cos_t, sin_t, gq_rows, g_k.reshape(1, hd))


def _flash_kernel(qt_ref, k_ref, vt_ref, o_ref, st_ref, m_ref, acc_ref):
    hd = ATTN_HEAD_DIM
    tk = ATTN_KV_TILE
    qt = jnp.concatenate([qt_ref[i * hd:(i + 1) * hd, :] for i in range(ATTN_GROUP)], axis=1)
    m_ref[...] = jnp.full_like(m_ref, -jnp.inf)
    acc_ref[...] = jnp.zeros_like(acc_ref)

    def scores(j, slot):
        start = pl.multiple_of(j * tk, tk)
        st_ref[slot] = _dot(k_ref[pl.ds(start, tk), :], qt)

    def consume(j, slot):
        start = pl.multiple_of(j * tk, tk)
        st = st_ref[slot]
        m_prev = m_ref[...]
        m_next = jnp.maximum(m_prev, jnp.max(st, axis=0, keepdims=True))
        alpha = jnp.exp2(m_prev - m_next)
        pt = jnp.exp2(st - m_next)
        acc_ref[...] = alpha * acc_ref[...] + _dot(vt_ref[:, pl.ds(start, tk)], pt.astype(BF16))
        m_ref[...] = m_next

    def pair(j, last):
        scores(j + 1, 1)
        consume(j, 0)
        if not last:
            scores(j + 2, 0)
        consume(j + 1, 1)

    npair = k_ref.shape[0] // (2 * tk)
    scores(0, 0)

    def body(jj, carry):
        pair(2 * jj, last=False)
        return carry

    lax.fori_loop(0, npair - 1, body, 0)
    pair(2 * (npair - 1), last=True)
    ot = acc_ref[:hd, :] / acc_ref[hd:hd + 1, :]
    tq = o_ref.shape[0]
    for i in range(ATTN_GROUP):
        o_ref[:, i * hd:(i + 1) * hd] = ot[:, i * tq:(i + 1) * tq].T.astype(o_ref.dtype)


def flash_attention(qt, k, vt, tq=256):
    seq = k.shape[0]
    hd = ATTN_HEAD_DIM
    gw = ATTN_GROUP * hd
    return pl.pallas_call(
        _flash_kernel,
        grid=(ATTN_KV_HEADS, seq // tq),
        in_specs=[
            pl.BlockSpec((gw, tq), lambda g, i: (g, i)),
            pl.BlockSpec((seq, hd), lambda g, i: (0, g)),
            pl.BlockSpec((None, ATTN_V_ROWS, seq), lambda g, i: (g, 0, 0)),
        ],
        out_specs=pl.BlockSpec((tq, gw), lambda g, i: (i, g)),
        out_shape=jax.ShapeDtypeStruct((seq, ATTN_Q_HEADS * hd), BF16),
        scratch_shapes=[
            pltpu.VMEM((2, ATTN_KV_TILE, ATTN_GROUP * tq), F32),
            pltpu.VMEM((1, ATTN_GROUP * tq), F32),
            pltpu.VMEM((ATTN_V_ROWS, ATTN_GROUP * tq), F32),
        ],
        compiler_params=_params(("parallel", "arbitrary"), 48),
        name="flash_attention",
    )(qt, k, vt)


def _chunk_tri(reverse):
    idx = np.arange(GLA_BLOCK)
    same = (idx[:, None] // GLA_CHUNK) == (idx[None, :] // GLA_CHUNK)
    tri = (idx[None, :] >= idx[:, None]) if reverse else (idx[None, :] <= idx[:, None])
    return jnp.asarray((same & tri).astype(np.float32), dtype=BF16)


def _gate_weights(w_gk, first_row):
    w = jnp.zeros((2 * GLA_GATE_RANK, GLA_DK), F32).at[first_row:first_row + GLA_GATE_RANK].set(w_gk)
    return w.astype(BF16)


def gla_layer(h, g_pre, w_in, w_gk_f, b_gk_f, w_gk_b, b_gk_b, g_head, w_out, g_post):
    proj, lr = norm_matmul(h, g_pre, w_in.astype(BF16), GLA_MAIN, tm=256, tn=1024)
    o_f, o_b = gla_scan(proj, lr, w_gk_f, b_gk_f, w_gk_b, b_gk_b)
    return gla_post(o_f, o_b, proj, g_head, w_out.astype(BF16), h, g_post)


def gqa_layer(h, g_pre, w_in, g_q, g_k, w_out, g_post):
    qt, k, vt = attn_proj(h, g_pre, w_in.astype(BF16), g_q, g_k)
    o = flash_attention(qt, k, vt)
    seq, d = o.shape
    nblk = seq // ATTN_Q_BLOCK
    o = o.reshape(nblk, ATTN_Q_BLOCK, d).transpose(1, 0, 2).reshape(seq, d)
    return matmul_norm_res(o, w_out, h, g_post)


def kernel(x, p, g_pre_mix, g_post_mix, g_pre_mlp, g_post_mlp, g_ple, w_mlp_up, w_mlp_down,
           w_ple_proj, w_ple_gate, gla_w_in, gla_w_gk_fwd, gla_b_gk_fwd, gla_w_gk_bwd,
           gla_b_gk_bwd, gla_g_head, gla_w_out, attn_w_in, attn_g_q, attn_g_k, attn_w_out):
    assert x.shape == (1, SEQ, D_MODEL)
    h = x[0]
    for i in range(DEPTH):
        j = i // 2
        if i % 2 == 0:
            h = gla_layer(h, g_pre_mix[i], gla_w_in[j], gla_w_gk_fwd[j], gla_b_gk_fwd[j],
                          gla_w_gk_bwd[j], gla_b_gk_bwd[j], gla_g_head[j], gla_w_out[j], g_post_mix[i])
        else:
            h = gqa_layer(h, g_pre_mix[i], attn_w_in[j], attn_g_q[j], attn_g_k[j], attn_w_out[j],
                          g_post_mix[i])
        h = mlp_block(h, g_pre_mlp[i], w_mlp_up, w_mlp_down, g_post_mlp[i], i)
        h = ple_block(h, p, w_ple_gate, w_ple_proj, g_ple[i], i)
    return h[None]
```

```python
import functools

import numpy as np
import jax
import jax.numpy as jnp
from jax import lax
from jax.experimental import pallas as pl
from jax.experimental.pallas import tpu as pltpu

F32 = jnp.float32
BF16 = jnp.bfloat16

D_MODEL = 2048
SEQ = 8192
DEPTH = 2
GRID_W = 64
PLE_DIM = 256
NORM_EPS = 1e-6
D_FF = 4 * D_MODEL

GLA_HEADS = 4
GLA_DK = D_MODEL // 2
GLA_DV = D_MODEL
GLA_HEAD_K = GLA_DK // GLA_HEADS
GLA_HEAD_V = GLA_DV // GLA_HEADS
GLA_GATE_RANK = 16
GLA_GATE_TAU = 16.0
GLA_CHUNK = 64
GLA_MAIN = 2 * GLA_DK + 2 * GLA_DV
GLA_BLOCK = 256

ATTN_HEAD_DIM = 128
ATTN_Q_HEADS = D_MODEL // ATTN_HEAD_DIM
ATTN_KV_HEADS = 4
ATTN_GROUP = ATTN_Q_HEADS // ATTN_KV_HEADS
ATTN_Q_BLOCK = 128
ATTN_KV_TILE = 1024
ATTN_V_ROWS = ATTN_HEAD_DIM + 16
ROPE_THETA = 10000.0
ROPE_AXIS_DIM = ATTN_HEAD_DIM // 2

LANES = 128
MIB = 1024 * 1024
LOG2E = 1.4426950408889634


def _params(semantics, vmem_mib):
    return pltpu.CompilerParams(dimension_semantics=semantics, vmem_limit_bytes=vmem_mib * MIB)


def _rms(x, gain):
    return x * lax.rsqrt(jnp.mean(x * x, axis=-1, keepdims=True) + NORM_EPS) * gain


def _dot(a, b):
    return jnp.dot(a, b, preferred_element_type=F32)


def _dot_nt(a, b):
    return lax.dot_general(a, b, (((1,), (1,)), ((), ())), preferred_element_type=F32)


def _dot_tn(a, b):
    return lax.dot_general(a, b, (((0,), (0,)), ((), ())), preferred_element_type=F32)


def _resident(shape):
    return pl.BlockSpec(shape, lambda *_: (0,) * len(shape), pipeline_mode=pl.Buffered(1))


def _norm_matmul_kernel(h_ref, g_ref, w_ref, o_ref, ox_ref, u_ref, *, tn):
    n = o_ref.shape[1]
    u_ref[...] = _rms(h_ref[...], g_ref[...]).astype(BF16)
    ox_ref[...] = _dot(u_ref[...], w_ref[:, n:]).astype(ox_ref.dtype)
    for c in range(n // tn):
        sl = slice(c * tn, (c + 1) * tn)
        o_ref[:, sl] = _dot(u_ref[...], w_ref[:, sl]).astype(o_ref.dtype)


def norm_matmul(h, g, w, n, tm, tn):
    m, k = h.shape
    nx = w.shape[1] - n
    return pl.pallas_call(
        functools.partial(_norm_matmul_kernel, tn=tn),
        grid=(m // tm,),
        in_specs=[
            pl.BlockSpec((tm, k), lambda i: (i, 0)),
            _resident((1, k)),
            _resident(w.shape),
        ],
        out_specs=[
            pl.BlockSpec((tm, n), lambda i: (i, 0)),
            pl.BlockSpec((tm, nx), lambda i: (i, 0)),
        ],
        out_shape=[jax.ShapeDtypeStruct((m, n), F32), jax.ShapeDtypeStruct((m, nx), BF16)],
        scratch_shapes=[pltpu.VMEM((tm, k), BF16)],
        compiler_params=_params(("parallel",), 56),
        name="norm_matmul",
    )(h, g.reshape(1, k), w)


def _gla_kernel(qf_ref, kf_ref, vf_ref, lrf_ref, qb_ref, kb_ref, vb_ref, lrb_ref, wf_ref, bf_ref, wb_ref, bb_ref,
                trif_ref, trib_ref, of_ref, ob_ref, s_ref, cum_ref):
    c = GLA_CHUNK
    nchunk = GLA_BLOCK // c

    @pl.when(pl.program_id(0) == 0)
    def _():
        s_ref[...] = jnp.zeros_like(s_ref)

    def cumulative(d, lr_ref, w_ref, b_ref, tri_ref):
        z = _dot(lr_ref[...], w_ref[...]) + b_ref[...]
        la = (jnp.minimum(z, 0.0) - jnp.log1p(jnp.exp(-jnp.abs(z)))) * (1.0 / GLA_GATE_TAU)
        la_hi = la.astype(BF16)
        la_lo = (la - la_hi.astype(F32)).astype(BF16)
        tri = tri_ref[...]
        cum_ref[d] = _dot(tri, la_hi) + _dot(tri, la_lo)

    cumulative(0, lrf_ref, wf_ref, bf_ref, trif_ref)
    cumulative(1, lrb_ref, wb_ref, bb_ref, trib_ref)

    row = lax.broadcasted_iota(jnp.int32, (c, c), 0)
    col = lax.broadcasted_iota(jnp.int32, (c, c), 1)
    directions = (
        (qf_ref, kf_ref, vf_ref, of_ref, col <= row, c // 2 - 1, c - 1),
        (qb_ref, kb_ref, vb_ref, ob_ref, col > row, c // 2, 0),
    )
    for step in range(nchunk):
        for d, (q_ref, k_ref, v_ref, o_ref, mask, ref_row, last_row) in enumerate(directions):
            j = nchunk - 1 - step if d else step
            sl = slice(j * c, (j + 1) * c)
            for h in range(GLA_HEADS):
                hk = slice(h * GLA_HEAD_K, (h + 1) * GLA_HEAD_K)
                hv = slice(h * GLA_HEAD_V, (h + 1) * GLA_HEAD_V)
                cumc = cum_ref[d, sl, hk]
                refc = cumc[ref_row:ref_row + 1]
                lastc = cumc[last_row:last_row + 1]
                qc = q_ref[sl, hk] * (GLA_HEAD_K ** -0.5)
                kc = k_ref[sl, hk]
                vc = v_ref[sl, hv].astype(BF16)
                qt = (qc * jnp.exp(cumc - refc)).astype(BF16)
                kt = (kc * jnp.exp(refc - cumc)).astype(BF16)
                qd = (qc * jnp.exp(cumc)).astype(BF16)
                ke = (kc * jnp.exp(lastc - cumc)).astype(BF16)
                scores = jnp.where(mask, _dot_nt(qt, kt), 0.0).astype(BF16)
                state = s_ref[d * GLA_HEADS + h]
                o_ref[sl, hv] = _dot(scores, vc) + _dot_nt(qd, state.astype(BF16))
                s_ref[d * GLA_HEADS + h] = state * jnp.exp(lastc) + _dot_tn(vc, ke)


def gla_scan(proj, lr, w_f, b_f, w_b, b_b):
    seq = proj.shape[0]
    t = GLA_BLOCK
    nb = seq // t
    fwd = lambda i: i
    bwd = lambda i: nb - 1 - i
    const = lambda i: (0, 0)
    v_blk = 2 * GLA_DK // GLA_DV

    def operands(blk):
        return [
            pl.BlockSpec((t, GLA_DK), lambda i: (blk(i), 0)),
            pl.BlockSpec((t, GLA_DK), lambda i: (blk(i), 1)),
            pl.BlockSpec((t, GLA_DV), lambda i: (blk(i), v_blk)),
            pl.BlockSpec((t, 2 * GLA_GATE_RANK), lambda i: (blk(i), 0)),
        ]

    return pl.pallas_call(
        _gla_kernel,
        grid=(nb,),
        in_specs=operands(fwd) + operands(bwd) + [
            pl.BlockSpec((2 * GLA_GATE_RANK, GLA_DK), const),
            pl.BlockSpec((1, GLA_DK), const),
            pl.BlockSpec((2 * GLA_GATE_RANK, GLA_DK), const),
            pl.BlockSpec((1, GLA_DK), const),
            pl.BlockSpec((t, t), const),
            pl.BlockSpec((t, t), const),
        ],
        out_specs=[
            pl.BlockSpec((t, GLA_DV), lambda i: (fwd(i), 0)),
            pl.BlockSpec((t, GLA_DV), lambda i: (bwd(i), 0)),
        ],
        out_shape=[jax.ShapeDtypeStruct((seq, GLA_DV), F32)] * 2,
        scratch_shapes=[
            pltpu.VMEM((2 * GLA_HEADS, GLA_HEAD_V, GLA_HEAD_K), F32),
            pltpu.VMEM((2, t, GLA_DK), F32),
        ],
        compiler_params=_params(("arbitrary",), 48),
        name="gla_scan",
    )(proj, proj, proj, lr, proj, proj, proj, lr,
      _gate_weights(w_f, 0), b_f.reshape(1, GLA_DK), _gate_weights(w_b, GLA_GATE_RANK), b_b.reshape(1, GLA_DK),
      _chunk_tri(False), _chunk_tri(True))


def _gla_post_kernel(of_ref, ob_ref, og_ref, gh_ref, w_ref, h_ref, gp_ref, o_ref, y_ref):
    for hd in range(GLA_HEADS):
        sl = slice(hd * GLA_HEAD_V, (hd + 1) * GLA_HEAD_V)
        o = _rms(of_ref[:, sl] + ob_ref[:, sl], gh_ref[...])
        og = og_ref[:, sl]
        y_ref[:, sl] = (o * (og / (1.0 + jnp.exp(-og)))).astype(BF16)
    mix = _dot(y_ref[...], w_ref[...])
    o_ref[...] = h_ref[...] + _rms(mix, gp_ref[...])


def gla_post(o_f, o_b, proj, g_head, w_out, h, g_post, tm=512):
    m, d = h.shape
    og_blk = (2 * GLA_DK + GLA_DV) // GLA_DV
    row = lambda i: (i, 0)
    return pl.pallas_call(
        _gla_post_kernel,
        grid=(m // tm,),
        in_specs=[
            pl.BlockSpec((tm, GLA_DV), row),
            pl.BlockSpec((tm, GLA_DV), row),
            pl.BlockSpec((tm, GLA_DV), lambda i: (i, og_blk)),
            _resident((1, GLA_HEAD_V)),
            _resident((GLA_DV, d)),
            pl.BlockSpec((tm, d), row),
            _resident((1, d)),
        ],
        out_specs=pl.BlockSpec((tm, d), row),
        out_shape=jax.ShapeDtypeStruct((m, d), F32),
        scratch_shapes=[pltpu.VMEM((tm, GLA_DV), BF16)],
        compiler_params=_params(("parallel",), 56),
        name="gla_post",
    )(o_f, o_b, proj, g_head.reshape(1, GLA_HEAD_V), w_out, h, g_post.reshape(1, d))


def _matmul_norm_res_kernel(a_ref, w_ref, h_ref, g_ref, o_ref, w_bf_ref):
    @pl.when(pl.program_id(0) == 0)
    def _():
        w_bf_ref[...] = w_ref[...].astype(BF16)

    o_ref[...] = h_ref[...] + _rms(_dot(a_ref[...], w_bf_ref[...]), g_ref[...])


def matmul_norm_res(a, w, h, g, tm=512):
    m, d = h.shape
    k = a.shape[1]
    row = lambda i: (i, 0)
    return pl.pallas_call(
        _matmul_norm_res_kernel,
        grid=(m // tm,),
        in_specs=[
            pl.BlockSpec((tm, k), row),
            _resident((k, d)),
            pl.BlockSpec((tm, d), row),
            _resident((1, d)),
        ],
        out_specs=pl.BlockSpec((tm, d), row),
        out_shape=jax.ShapeDtypeStruct((m, d), F32),
        scratch_shapes=[pltpu.VMEM((k, d), BF16)],
        compiler_params=_params(("arbitrary",), 56),
        name="matmul_norm_res",
    )(a, w, h, g.reshape(1, d))


def _mlp_kernel(h_ref, g1_ref, wu_ref, wd_ref, g2_ref, o_ref, u_ref):
    f = pl.program_id(1)

    @pl.when(f == 0)
    def _():
        u_ref[...] = _rms(h_ref[...], g1_ref[...]).astype(BF16)
        o_ref[...] = jnp.zeros_like(o_ref)

    hid = jnp.maximum(_dot(u_ref[...], wu_ref[...].astype(BF16)), 0.0)
    o_ref[...] += _dot((hid * hid).astype(BF16), wd_ref[...].astype(BF16))

    @pl.when(f == pl.num_programs(1) - 1)
    def _():
        o_ref[...] = h_ref[...] + _rms(o_ref[...], g2_ref[...])


def mlp_block(h, g_pre, w_up, w_down, g_post, layer, tm=1024, tf=512):
    m, d = h.shape
    ff = w_up.shape[2]
    return pl.pallas_call(
        _mlp_kernel,
        grid=(m // tm, ff // tf),
        in_specs=[
            pl.BlockSpec((tm, d), lambda i, f: (i, 0), pipeline_mode=pl.Buffered(1)),
            _resident((1, d)),
            pl.BlockSpec((None, d, tf), lambda i, f: (layer, 0, f)),
            pl.BlockSpec((None, tf, d), lambda i, f: (layer, f, 0)),
            _resident((1, d)),
        ],
        out_specs=pl.BlockSpec((tm, d), lambda i, f: (i, 0)),
        out_shape=jax.ShapeDtypeStruct((m, d), F32),
        scratch_shapes=[pltpu.VMEM((tm, d), BF16)],
        compiler_params=_params(("parallel", "arbitrary"), 58),
        name="mlp_block",
    )(h, g_pre.reshape(1, d), w_up, w_down, g_post.reshape(1, d))


def _ple_kernel(h_ref, p_ref, wg_ref, wp_ref, g_ref, o_ref, wg_bf_ref, wp_bf_ref):
    @pl.when(pl.program_id(0) == 0)
    def _():
        wg_bf_ref[...] = wg_ref[...].astype(BF16)
        wp_bf_ref[...] = wp_ref[...].astype(BF16)

    x = h_ref[...]
    gate = 1.0 / (1.0 + jnp.exp(-_dot(x.astype(BF16), wg_bf_ref[...])))
    e = _dot(p_ref[...].astype(BF16), wp_bf_ref[...])
    o_ref[...] = x + _rms(gate * e, g_ref[...])


def ple_block(h, p, w_gate, w_proj, g, layer, tm=512):
    m, d = h.shape
    pd = p.shape[-1]
    row = lambda i: (i, 0)
    once = pl.Buffered(1)
    return pl.pallas_call(
        _ple_kernel,
        grid=(m // tm,),
        in_specs=[
            pl.BlockSpec((tm, d), row),
            pl.BlockSpec((None, None, tm, pd), lambda i: (layer, 0, i, 0)),
            pl.BlockSpec((None, d, d), lambda i: (layer, 0, 0), pipeline_mode=once),
            pl.BlockSpec((None, pd, d), lambda i: (layer, 0, 0), pipeline_mode=once),
            _resident((1, d)),
        ],
        out_specs=pl.BlockSpec((tm, d), row),
        out_shape=jax.ShapeDtypeStruct((m, d), F32),
        scratch_shapes=[pltpu.VMEM((d, d), BF16), pltpu.VMEM((pd, d), BF16)],
        compiler_params=_params(("arbitrary",), 56),
        name="ple_block",
    )(h, p, w_gate, w_proj, g.reshape(1, d))


def _rope_tables(seq):
    rows = seq // GRID_W
    t_row = np.repeat(np.arange(rows), GRID_W).astype(np.float64)
    t_col = np.tile(np.arange(GRID_W), rows).astype(np.float64)
    inv_freq = 1.0 / (ROPE_THETA ** (np.arange(0, ROPE_AXIS_DIM, 2, dtype=np.float64) / ROPE_AXIS_DIM))
    a_row = t_row[:, None] * inv_freq
    a_col = t_col[:, None] * inv_freq
    ang = np.concatenate([a_row, a_row, a_col, a_col], axis=1)
    half = ROPE_AXIS_DIM // 2
    sign = np.where((np.arange(ATTN_HEAD_DIM) % ROPE_AXIS_DIM) < half, -1.0, 1.0)
    return np.cos(ang).astype(np.float32), (np.sin(ang) * sign).astype(np.float32)


def _attn_proj_kernel(h_ref, g_ref, w_ref, cos_ref, sin_ref, cost_ref, sint_ref, gq_ref, gk_ref,
                      qt_ref, k_ref, vt_ref, u_ref):
    hd = ATTN_HEAD_DIM
    half = ROPE_AXIS_DIM // 2
    tm = h_ref.shape[0]
    per_dot = ATTN_KV_HEADS
    u_ref[...] = _rms(h_ref[...], g_ref[...]).astype(BF16)

    def project(first_head):
        return _dot(u_ref[...], w_ref[:, first_head * hd:(first_head + per_dot) * hd])

    cos_t = cost_ref[...]
    sin_t = sint_ref[...]
    gq = jnp.tile(gq_ref[...], (1, tm // LANES))
    for c in range(ATTN_Q_HEADS // per_dot):
        x = project(c * per_dot)
        for i in range(per_dot):
            xt = x[:, i * hd:(i + 1) * hd].T
            y = xt * lax.rsqrt(jnp.mean(xt * xt, axis=0, keepdims=True) + NORM_EPS) * gq
            partner = jnp.concatenate([y[half:2 * half], y[:half], y[3 * half:], y[2 * half:3 * half]], axis=0)
            head = c * per_dot + i
            qt_ref[head * hd:(head + 1) * hd, :] = (y * cos_t + partner * sin_t).astype(BF16)

    cos = cos_ref[...]
    sin = sin_ref[...]
    lane = lax.broadcasted_iota(jnp.int32, cos.shape, 1)
    first_half = (lane % ROPE_AXIS_DIM) < half
    xk = project(ATTN_Q_HEADS)
    xv = project(ATTN_Q_HEADS + ATTN_KV_HEADS)
    for i in range(ATTN_KV_HEADS):
        sl = slice(i * hd, (i + 1) * hd)
        y = _rms(xk[:, sl], gk_ref[...])
        partner = jnp.where(first_half, pltpu.roll(y, hd - half, 1), pltpu.roll(y, half, 1))
        k_ref[:, sl] = (y * cos + partner * sin).astype(BF16)
        vt_ref[i, :hd, :] = xv[:, sl].T.astype(BF16)
        vt_ref[i, hd:, :] = jnp.ones((ATTN_V_ROWS - hd, tm), BF16)


def attn_proj(h, g_pre, w, g_q, g_k, tm=512):
    seq, d = h.shape
    hd = ATTN_HEAD_DIM
    cos, sin = _rope_tables(seq)
    q_scale = (hd ** -0.5) * LOG2E
    cos_t, sin_t = jnp.asarray(cos.T * q_scale), jnp.asarray(sin.T * q_scale)
    cos, sin = jnp.asarray(cos), jnp.asarray(sin)
    gq_rows = jnp.broadcast_to(g_q.astype(F32)[:, None], (hd, LANES))
    row = lambda i: (i, 0)
    col = lambda i: (0, i)
    return pl.pallas_call(
        _attn_proj_kernel,
        grid=(seq // tm,),
        in_specs=[
            pl.BlockSpec((tm, d), row),
            _resident((1, d)),
            _resident(w.shape),
            pl.BlockSpec((tm, hd), row),
            pl.BlockSpec((tm, hd), row),
            pl.BlockSpec((hd, tm), col),
            pl.BlockSpec((hd, tm), col),
            _resident((hd, LANES)),
            _resident((1, hd)),
        ],
        out_specs=[
            pl.BlockSpec((ATTN_Q_HEADS * hd, tm), col),
            pl.BlockSpec((tm, ATTN_KV_HEADS * hd), row),
            pl.BlockSpec((ATTN_KV_HEADS, ATTN_V_ROWS, tm), lambda i: (0, 0, i)),
        ],
        out_shape=[
            jax.ShapeDtypeStruct((ATTN_Q_HEADS * hd, seq), BF16),
            jax.ShapeDtypeStruct((seq, ATTN_KV_HEADS * hd), BF16),
            jax.ShapeDtypeStruct((ATTN_KV_HEADS, ATTN_V_ROWS, seq), BF16),
        ],
        scratch_shapes=[pltpu.VMEM((tm, d), BF16)],
        compiler_params=_params(("parallel",), 48),
        name="attn_proj",
    )(h, g_pre.reshape(1, d), w, cos, sin, cos_t, sin_t, gq_rows, g_k.reshape(1, hd))


def _flash_kernel(qt_ref, k_ref, vt_ref, o_ref, st_ref, m_ref, acc_ref, *, tq):
    hd = ATTN_HEAD_DIM
    tk = ATTN_KV_TILE
    npair = k_ref.shape[0] // (2 * tk)
    nqb = qt_ref.shape[1] // tq

    def queries(qb):
        start = pl.multiple_of(qb * tq, tq)
        return jnp.concatenate([qt_ref[i * hd:(i + 1) * hd, pl.ds(start, tq)] for i in range(ATTN_GROUP)], axis=1)

    def scores(j, slot, qt):
        start = pl.multiple_of(j * tk, tk)
        st_ref[slot] = _dot(k_ref[pl.ds(start, tk), :], qt)

    def consume(j, slot):
        start = pl.multiple_of(j * tk, tk)
        st = st_ref[slot]
        m_prev = m_ref[...]
        m_next = jnp.maximum(m_prev, jnp.max(st, axis=0, keepdims=True))
        alpha = jnp.exp2(m_prev - m_next)
        pt = jnp.exp2(st - m_next)
        acc_ref[...] = alpha * acc_ref[...] + _dot(vt_ref[:, pl.ds(start, tk)], pt.astype(BF16))
        m_ref[...] = m_next

    def pair(j, qt, qt_after):
        scores(j + 1, 1, qt)
        consume(j, 0)
        if qt_after is None:
            scores(j + 2, 0, qt)
        else:
            scores(0, 0, qt_after)
        consume(j + 1, 1)

    scores(0, 0, queries(0))

    def query_block(qb, carry):
        qt = queries(qb)
        m_ref[...] = jnp.full_like(m_ref, -jnp.inf)
        acc_ref[...] = jnp.zeros_like(acc_ref)

        def body(jj, c):
            pair(2 * jj, qt, None)
            return c

        lax.fori_loop(0, npair - 1, body, 0)
        pair(2 * (npair - 1), qt, queries(jnp.minimum(qb + 1, nqb - 1)))
        ot = acc_ref[:hd, :] / acc_ref[hd:hd + 1, :]
        rows = pl.ds(pl.multiple_of(qb * tq, tq), tq)
        for i in range(ATTN_GROUP):
            o_ref[rows, i * hd:(i + 1) * hd] = ot[:, i * tq:(i + 1) * tq].T.astype(o_ref.dtype)
        return carry

    lax.fori_loop(0, nqb, query_block, 0)


def flash_attention(qt, k, vt, tq=256):
    seq = k.shape[0]
    hd = ATTN_HEAD_DIM
    gw = ATTN_GROUP * hd
    once = pl.Buffered(1)
    return pl.pallas_call(
        functools.partial(_flash_kernel, tq=tq),
        grid=(ATTN_KV_HEADS,),
        in_specs=[
            pl.BlockSpec((gw, seq), lambda g: (g, 0), pipeline_mode=once),
            pl.BlockSpec((seq, hd), lambda g: (0, g), pipeline_mode=once),
            pl.BlockSpec((None, ATTN_V_ROWS, seq), lambda g: (g, 0, 0), pipeline_mode=once),
        ],
        out_specs=pl.BlockSpec((seq, gw), lambda g: (0, g)),
        out_shape=jax.ShapeDtypeStruct((seq, ATTN_Q_HEADS * hd), BF16),
        scratch_shapes=[
            pltpu.VMEM((2, ATTN_KV_TILE, ATTN_GROUP * tq), F32),
            pltpu.VMEM((1, ATTN_GROUP * tq), F32),
            pltpu.VMEM((ATTN_V_ROWS, ATTN_GROUP * tq), F32),
        ],
        compiler_params=_params(("parallel",), 56),
        name="flash_attention",
    )(qt, k, vt)


def _chunk_tri(reverse):
    idx = np.arange(GLA_BLOCK)
    same = (idx[:, None] // GLA_CHUNK) == (idx[None, :] // GLA_CHUNK)
    tri = (idx[None, :] >= idx[:, None]) if reverse else (idx[None, :] <= idx[:, None])
    return jnp.asarray((same & tri).astype(np.float32), dtype=BF16)


def _gate_weights(w_gk, first_row):
    w = jnp.zeros((2 * GLA_GATE_RANK, GLA_DK), F32).at[first_row:first_row + GLA_GATE_RANK].set(w_gk)
    return w.astype(BF16)


def gla_layer(h, g_pre, w_in, w_gk_f, b_gk_f, w_gk_b, b_gk_b, g_head, w_out, g_post):
    proj, lr = norm_matmul(h, g_pre, w_in.astype(BF16), GLA_MAIN, tm=256, tn=1024)
    o_f, o_b = gla_scan(proj, lr, w_gk_f, b_gk_f, w_gk_b, b_gk_b)
    return gla_post(o_f, o_b, proj, g_head, w_out.astype(BF16), h, g_post)


def gqa_layer(h, g_pre, w_in, g_q, g_k, w_out, g_post):
    qt, k, vt = attn_proj(h, g_pre, w_in.astype(BF16), g_q, g_k)
    o = flash_attention(qt, k, vt)
    seq, d = o.shape
    nblk = seq // ATTN_Q_BLOCK
    o = o.reshape(nblk, ATTN_Q_BLOCK, d).transpose(1, 0, 2).reshape(seq, d)
    return matmul_norm_res(o, w_out, h, g_post)


def kernel(x, p, g_pre_mix, g_post_mix, g_pre_mlp, g_post_mlp, g_ple, w_mlp_up, w_mlp_down,
           w_ple_proj, w_ple_gate, gla_w_in, gla_w_gk_fwd, gla_b_gk_fwd, gla_w_gk_bwd,
           gla_b_gk_bwd, gla_g_head, gla_w_out, attn_w_in, attn_g_q, attn_g_k, attn_w_out):
    assert x.shape == (1, SEQ, D_MODEL)
    h = x[0]
    for i in range(DEPTH):
        j = i // 2
        if i % 2 == 0:
            h = gla_layer(h, g_pre_mix[i], gla_w_in[j], gla_w_gk_fwd[j], gla_b_gk_fwd[j],
                          gla_w_gk_bwd[j], gla_b_gk_bwd[j], gla_g_head[j], gla_w_out[j], g_post_mix[i])
        else:
            h = gqa_layer(h, g_pre_mix[i], attn_w_in[j], attn_g_q[j], attn_g_k[j], attn_w_out[j],
                          g_post_mix[i])
        h = mlp_block(h, g_pre_mlp[i], w_mlp_up, w_mlp_down, g_post_mlp[i], i)
        h = ple_block(h, p, w_ple_gate, w_ple_proj, g_ple[i], i)
    return h[None]
```

```python
import functools

import numpy as np
import jax
import jax.numpy as jnp
from jax import lax
from jax.experimental import pallas as pl
from jax.experimental.pallas import tpu as pltpu

F32 = jnp.float32
BF16 = jnp.bfloat16

D_MODEL = 2048
SEQ = 8192
DEPTH = 2
GRID_W = 64
PLE_DIM = 256
NORM_EPS = 1e-6
D_FF = 4 * D_MODEL

GLA_HEADS = 4
GLA_DK = D_MODEL // 2
GLA_DV = D_MODEL
GLA_HEAD_K = GLA_DK // GLA_HEADS
GLA_HEAD_V = GLA_DV // GLA_HEADS
GLA_GATE_RANK = 16
GLA_GATE_TAU = 16.0
GLA_CHUNK = 64
GLA_MAIN = 2 * GLA_DK + 2 * GLA_DV
GLA_BLOCK = 256

ATTN_HEAD_DIM = 128
ATTN_Q_HEADS = D_MODEL // ATTN_HEAD_DIM
ATTN_KV_HEADS = 4
ATTN_GROUP = ATTN_Q_HEADS // ATTN_KV_HEADS
ATTN_Q_BLOCK = 128
ATTN_KV_TILE = 1024
ATTN_V_ROWS = ATTN_HEAD_DIM + 16
ROPE_THETA = 10000.0
ROPE_AXIS_DIM = ATTN_HEAD_DIM // 2

LANES = 128
MIB = 1024 * 1024
LOG2E = 1.4426950408889634


def _params(semantics, vmem_mib):
    return pltpu.CompilerParams(dimension_semantics=semantics, vmem_limit_bytes=vmem_mib * MIB)


def _rms(x, gain):
    return x * lax.rsqrt(jnp.mean(x * x, axis=-1, keepdims=True) + NORM_EPS) * gain


def _dot(a, b):
    return jnp.dot(a, b, preferred_element_type=F32)


def _dot_nt(a, b):
    return lax.dot_general(a, b, (((1,), (1,)), ((), ())), preferred_element_type=F32)


def _dot_tn(a, b):
    return lax.dot_general(a, b, (((0,), (0,)), ((), ())), preferred_element_type=F32)


def _resident(shape):
    return pl.BlockSpec(shape, lambda *_: (0,) * len(shape), pipeline_mode=pl.Buffered(1))


def _norm_matmul_kernel(h_ref, g_ref, w_ref, o_ref, ox_ref, u_ref, *, tn):
    n = o_ref.shape[1]
    u_ref[...] = _rms(h_ref[...], g_ref[...]).astype(BF16)
    ox_ref[...] = _dot(u_ref[...], w_ref[:, n:]).astype(ox_ref.dtype)
    for c in range(n // tn):
        sl = slice(c * tn, (c + 1) * tn)
        o_ref[:, sl] = _dot(u_ref[...], w_ref[:, sl]).astype(o_ref.dtype)


def norm_matmul(h, g, w, n, tm, tn):
    m, k = h.shape
    nx = w.shape[1] - n
    return pl.pallas_call(
        functools.partial(_norm_matmul_kernel, tn=tn),
        grid=(m // tm,),
        in_specs=[
            pl.BlockSpec((tm, k), lambda i: (i, 0)),
            _resident((1, k)),
            _resident(w.shape),
        ],
        out_specs=[
            pl.BlockSpec((tm, n), lambda i: (i, 0)),
            pl.BlockSpec((tm, nx), lambda i: (i, 0)),
        ],
        out_shape=[jax.ShapeDtypeStruct((m, n), F32), jax.ShapeDtypeStruct((m, nx), BF16)],
        scratch_shapes=[pltpu.VMEM((tm, k), BF16)],
        compiler_params=_params(("parallel",), 56),
        name="norm_matmul",
    )(h, g.reshape(1, k), w)


def _gla_kernel(qf_ref, kf_ref, vf_ref, lrf_ref, qb_ref, kb_ref, vb_ref, lrb_ref, wf_ref, bf_ref, wb_ref, bb_ref,
                trif_ref, trib_ref, of_ref, ob_ref, s_ref, cum_ref):
    c = GLA_CHUNK
    nchunk = GLA_BLOCK // c

    @pl.when(pl.program_id(0) == 0)
    def _():
        s_ref[...] = jnp.zeros_like(s_ref)

    def cumulative(d, lr_ref, w_ref, b_ref, tri_ref):
        z = _dot(lr_ref[...], w_ref[...]) + b_ref[...]
        la = (jnp.minimum(z, 0.0) - jnp.log1p(jnp.exp(-jnp.abs(z)))) * (1.0 / GLA_GATE_TAU)
        la_hi = la.astype(BF16)
        la_lo = (la - la_hi.astype(F32)).astype(BF16)
        tri = tri_ref[...]
        cum_ref[d] = _dot(tri, la_hi) + _dot(tri, la_lo)

    cumulative(0, lrf_ref, wf_ref, bf_ref, trif_ref)
    cumulative(1, lrb_ref, wb_ref, bb_ref, trib_ref)

    row = lax.broadcasted_iota(jnp.int32, (c, c), 0)
    col = lax.broadcasted_iota(jnp.int32, (c, c), 1)
    directions = (
        (qf_ref, kf_ref, vf_ref, of_ref, col <= row, c // 2 - 1, c - 1),
        (qb_ref, kb_ref, vb_ref, ob_ref, col > row, c // 2, 0),
    )
    for step in range(nchunk):
        for d, (q_ref, k_ref, v_ref, o_ref, mask, ref_row, last_row) in enumerate(directions):
            j = nchunk - 1 - step if d else step
            sl = slice(j * c, (j + 1) * c)
            for h in range(GLA_HEADS):
                hk = slice(h * GLA_HEAD_K, (h + 1) * GLA_HEAD_K)
                hv = slice(h * GLA_HEAD_V, (h + 1) * GLA_HEAD_V)
                cumc = cum_ref[d, sl, hk]
                refc = cumc[ref_row:ref_row + 1]
                lastc = cumc[last_row:last_row + 1]
                qc = q_ref[sl, hk] * (GLA_HEAD_K ** -0.5)
                kc = k_ref[sl, hk]
                vc = v_ref[sl, hv].astype(BF16)
                qt = (qc * jnp.exp(cumc - refc)).astype(BF16)
                kt = (kc * jnp.exp(refc - cumc)).astype(BF16)
                qd = (qc * jnp.exp(cumc)).astype(BF16)
                ke = (kc * jnp.exp(lastc - cumc)).astype(BF16)
                scores = jnp.where(mask, _dot_nt(qt, kt), 0.0).astype(BF16)
                state = s_ref[d * GLA_HEADS + h]
                o_ref[sl, hv] = _dot(scores, vc) + _dot_nt(qd, state.astype(BF16))
                s_ref[d * GLA_HEADS + h] = state * jnp.exp(lastc) + _dot_tn(vc, ke)


def gla_scan(proj, lr, w_f, b_f, w_b, b_b):
    seq = proj.shape[0]
    t = GLA_BLOCK
    nb = seq // t
    fwd = lambda i: i
    bwd = lambda i: nb - 1 - i
    const = lambda i: (0, 0)
    v_blk = 2 * GLA_DK // GLA_DV

    def operands(blk):
        return [
            pl.BlockSpec((t, GLA_DK), lambda i: (blk(i), 0)),
            pl.BlockSpec((t, GLA_DK), lambda i: (blk(i), 1)),
            pl.BlockSpec((t, GLA_DV), lambda i: (blk(i), v_blk)),
            pl.BlockSpec((t, 2 * GLA_GATE_RANK), lambda i: (blk(i), 0)),
        ]

    return pl.pallas_call(
        _gla_kernel,
        grid=(nb,),
        in_specs=operands(fwd) + operands(bwd) + [
            pl.BlockSpec((2 * GLA_GATE_RANK, GLA_DK), const),
            pl.BlockSpec((1, GLA_DK), const),
            pl.BlockSpec((2 * GLA_GATE_RANK, GLA_DK), const),
            pl.BlockSpec((1, GLA_DK), const),
            pl.BlockSpec((t, t), const),
            pl.BlockSpec((t, t), const),
        ],
        out_specs=[
            pl.BlockSpec((t, GLA_DV), lambda i: (fwd(i), 0)),
            pl.BlockSpec((t, GLA_DV), lambda i: (bwd(i), 0)),
        ],
        out_shape=[jax.ShapeDtypeStruct((seq, GLA_DV), F32)] * 2,
        scratch_shapes=[
            pltpu.VMEM((2 * GLA_HEADS, GLA_HEAD_V, GLA_HEAD_K), F32),
            pltpu.VMEM((2, t, GLA_DK), F32),
        ],
        compiler_params=_params(("arbitrary",), 48),
        name="gla_scan",
    )(proj, proj, proj, lr, proj, proj, proj, lr,
      _gate_weights(w_f, 0), b_f.reshape(1, GLA_DK), _gate_weights(w_b, GLA_GATE_RANK), b_b.reshape(1, GLA_DK),
      _chunk_tri(False), _chunk_tri(True))


def _gla_post_kernel(of_ref, ob_ref, og_ref, gh_ref, w_ref, h_ref, gp_ref, o_ref, y_ref):
    for hd in range(GLA_HEADS):
        sl = slice(hd * GLA_HEAD_V, (hd + 1) * GLA_HEAD_V)
        o = _rms(of_ref[:, sl] + ob_ref[:, sl], gh_ref[...])
        og = og_ref[:, sl]
        y_ref[:, sl] = (o * (og / (1.0 + jnp.exp(-og)))).astype(BF16)
    mix = _dot(y_ref[...], w_ref[...])
    o_ref[...] = h_ref[...] + _rms(mix, gp_ref[...])


def gla_post(o_f, o_b, proj, g_head, w_out, h, g_post, tm=512):
    m, d = h.shape
    og_blk = (2 * GLA_DK + GLA_DV) // GLA_DV
    row = lambda i: (i, 0)
    return pl.pallas_call(
        _gla_post_kernel,
        grid=(m // tm,),
        in_specs=[
            pl.BlockSpec((tm, GLA_DV), row),
            pl.BlockSpec((tm, GLA_DV), row),
            pl.BlockSpec((tm, GLA_DV), lambda i: (i, og_blk)),
            _resident((1, GLA_HEAD_V)),
            _resident((GLA_DV, d)),
            pl.BlockSpec((tm, d), row),
            _resident((1, d)),
        ],
        out_specs=pl.BlockSpec((tm, d), row),
        out_shape=jax.ShapeDtypeStruct((m, d), F32),
        scratch_shapes=[pltpu.VMEM((tm, GLA_DV), BF16)],
        compiler_params=_params(("parallel",), 56),
        name="gla_post",
    )(o_f, o_b, proj, g_head.reshape(1, GLA_HEAD_V), w_out, h, g_post.reshape(1, d))


def _matmul_norm_res_kernel(a_ref, w_ref, h_ref, g_ref, o_ref, w_bf_ref):
    @pl.when(pl.program_id(0) == 0)
    def _():
        w_bf_ref[...] = w_ref[...].astype(BF16)

    o_ref[...] = h_ref[...] + _rms(_dot(a_ref[...], w_bf_ref[...]), g_ref[...])


def matmul_norm_res(a, w, h, g, tm=512):
    m, d = h.shape
    k = a.shape[1]
    row = lambda i: (i, 0)
    return pl.pallas_call(
        _matmul_norm_res_kernel,
        grid=(m // tm,),
        in_specs=[
            pl.BlockSpec((tm, k), row),
            _resident((k, d)),
            pl.BlockSpec((tm, d), row),
            _resident((1, d)),
        ],
        out_specs=pl.BlockSpec((tm, d), row),
        out_shape=jax.ShapeDtypeStruct((m, d), F32),
        scratch_shapes=[pltpu.VMEM((k, d), BF16)],
        compiler_params=_params(("arbitrary",), 56),
        name="matmul_norm_res",
    )(a, w, h, g.reshape(1, d))


def _mlp_kernel(h_ref, g1_ref, wu_ref, wd_ref, g2_ref, o_ref, u_ref):
    f = pl.program_id(1)

    @pl.when(f == 0)
    def _():
        u_ref[...] = _rms(h_ref[...], g1_ref[...]).astype(BF16)
        o_ref[...] = jnp.zeros_like(o_ref)

    hid = jnp.maximum(_dot(u_ref[...], wu_ref[...].astype(BF16)), 0.0)
    o_ref[...] += _dot((hid * hid).astype(BF16), wd_ref[...].astype(BF16))

    @pl.when(f == pl.num_programs(1) - 1)
    def _():
        o_ref[...] = h_ref[...] + _rms(o_ref[...], g2_ref[...])


def mlp_block(h, g_pre, w_up, w_down, g_post, layer, tm=1024, tf=512):
    m, d = h.shape
    ff = w_up.shape[2]
    return pl.pallas_call(
        _mlp_kernel,
        grid=(m // tm, ff // tf),
        in_specs=[
            pl.BlockSpec((tm, d), lambda i, f: (i, 0)),
            _resident((1, d)),
            pl.BlockSpec((None, d, tf), lambda i, f: (layer, 0, f)),
            pl.BlockSpec((None, tf, d), lambda i, f: (layer, f, 0)),
            _resident((1, d)),
        ],
        out_specs=pl.BlockSpec((tm, d), lambda i, f: (i, 0)),
        out_shape=jax.ShapeDtypeStruct((m, d), F32),
        scratch_shapes=[pltpu.VMEM((tm, d), BF16)],
        compiler_params=_params(("parallel", "arbitrary"), 62),
        name="mlp_block",
    )(h, g_pre.reshape(1, d), w_up, w_down, g_post.reshape(1, d))


def _ple_kernel(h_ref, p_ref, wg_ref, wp_ref, g_ref, o_ref, wg_bf_ref, wp_bf_ref):
    @pl.when(pl.program_id(0) == 0)
    def _():
        wg_bf_ref[...] = wg_ref[...].astype(BF16)
        wp_bf_ref[...] = wp_ref[...].astype(BF16)

    x = h_ref[...]
    gate = 1.0 / (1.0 + jnp.exp(-_dot(x.astype(BF16), wg_bf_ref[...])))
    e = _dot(p_ref[...].astype(BF16), wp_bf_ref[...])
    o_ref[...] = x + _rms(gate * e, g_ref[...])


def ple_block(h, p, w_gate, w_proj, g, layer, tm=512):
    m, d = h.shape
    pd = p.shape[-1]
    row = lambda i: (i, 0)
    once = pl.Buffered(1)
    return pl.pallas_call(
        _ple_kernel,
        grid=(m // tm,),
        in_specs=[
            pl.BlockSpec((tm, d), row),
            pl.BlockSpec((None, None, tm, pd), lambda i: (layer, 0, i, 0)),
            pl.BlockSpec((None, d, d), lambda i: (layer, 0, 0), pipeline_mode=once),
            pl.BlockSpec((None, pd, d), lambda i: (layer, 0, 0), pipeline_mode=once),
            _resident((1, d)),
        ],
        out_specs=pl.BlockSpec((tm, d), row),
        out_shape=jax.ShapeDtypeStruct((m, d), F32),
        scratch_shapes=[pltpu.VMEM((d, d), BF16), pltpu.VMEM((pd, d), BF16)],
        compiler_params=_params(("arbitrary",), 56),
        name="ple_block",
    )(h, p, w_gate, w_proj, g.reshape(1, d))


def _rope_tables(seq):
    rows = seq // GRID_W
    t_row = np.repeat(np.arange(rows), GRID_W).astype(np.float64)
    t_col = np.tile(np.arange(GRID_W), rows).astype(np.float64)
    inv_freq = 1.0 / (ROPE_THETA ** (np.arange(0, ROPE_AXIS_DIM, 2, dtype=np.float64) / ROPE_AXIS_DIM))
    a_row = t_row[:, None] * inv_freq
    a_col = t_col[:, None] * inv_freq
    ang = np.concatenate([a_row, a_row, a_col, a_col], axis=1)
    half = ROPE_AXIS_DIM // 2
    sign = np.where((np.arange(ATTN_HEAD_DIM) % ROPE_AXIS_DIM) < half, -1.0, 1.0)
    return np.cos(ang).astype(np.float32), (np.sin(ang) * sign).astype(np.float32)


def _attn_proj_kernel(h_ref, g_ref, w_ref, cos_ref, sin_ref, cost_ref, sint_ref, gq_ref, gk_ref,
                      qt_ref, k_ref, vt_ref, u_ref):
    hd = ATTN_HEAD_DIM
    half = ROPE_AXIS_DIM // 2
    tm = h_ref.shape[0]
    per_dot = ATTN_KV_HEADS
    u_ref[...] = _rms(h_ref[...], g_ref[...]).astype(BF16)

    def project(first_head):
        return _dot(u_ref[...], w_ref[:, first_head * hd:(first_head + per_dot) * hd])

    cos_t = cost_ref[...]
    sin_t = sint_ref[...]
    gq = jnp.tile(gq_ref[...], (1, tm // LANES))
    for c in range(ATTN_Q_HEADS // per_dot):
        x = project(c * per_dot)
        for i in range(per_dot):
            xt = x[:, i * hd:(i + 1) * hd].T
            y = xt * lax.rsqrt(jnp.mean(xt * xt, axis=0, keepdims=True) + NORM_EPS) * gq
            partner = jnp.concatenate([y[half:2 * half], y[:half], y[3 * half:], y[2 * half:3 * half]], axis=0)
            head = c * per_dot + i
            qt_ref[head * hd:(head + 1) * hd, :] = (y * cos_t + partner * sin_t).astype(BF16)

    cos = cos_ref[...]
    sin = sin_ref[...]
    lane = lax.broadcasted_iota(jnp.int32, cos.shape, 1)
    first_half = (lane % ROPE_AXIS_DIM) < half
    xk = project(ATTN_Q_HEADS)
    xv = project(ATTN_Q_HEADS + ATTN_KV_HEADS)
    for i in range(ATTN_KV_HEADS):
        sl = slice(i * hd, (i + 1) * hd)
        y = _rms(xk[:, sl], gk_ref[...])
        partner = jnp.where(first_half, pltpu.roll(y, hd - half, 1), pltpu.roll(y, half, 1))
        k_ref[:, sl] = (y * cos + partner * sin).astype(BF16)
        vt_ref[i, :hd, :] = xv[:, sl].T.astype(BF16)
        vt_ref[i, hd:, :] = jnp.ones((ATTN_V_ROWS - hd, tm), BF16)


def attn_proj(h, g_pre, w, g_q, g_k, tm=512):
    seq, d = h.shape
    hd = ATTN_HEAD_DIM
    cos, sin = _rope_tables(seq)
    q_scale = (hd ** -0.5) * LOG2E
    cos_t, sin_t = jnp.asarray(cos.T * q_scale), jnp.asarray(sin.T * q_scale)
    cos, sin = jnp.asarray(cos), jnp.asarray(sin)
    gq_rows = jnp.broadcast_to(g_q.astype(F32)[:, None], (hd, LANES))
    row = lambda i: (i, 0)
    col = lambda i: (0, i)
    return pl.pallas_call(
        _attn_proj_kernel,
        grid=(seq // tm,),
        in_specs=[
            pl.BlockSpec((tm, d), row),
            _resident((1, d)),
            _resident(w.shape),
            pl.BlockSpec((tm, hd), row),
            pl.BlockSpec((tm, hd), row),
            pl.BlockSpec((hd, tm), col),
            pl.BlockSpec((hd, tm), col),
            _resident((hd, LANES)),
            _resident((1, hd)),
        ],
        out_specs=[
            pl.BlockSpec((ATTN_Q_HEADS * hd, tm), col),
            pl.BlockSpec((tm, ATTN_KV_HEADS * hd), row),
            pl.BlockSpec((ATTN_KV_HEADS, ATTN_V_ROWS, tm), lambda i: (0, 0, i)),
        ],
        out_shape=[
            jax.ShapeDtypeStruct((ATTN_Q_HEADS * hd, seq), BF16),
            jax.ShapeDtypeStruct((seq, ATTN_KV_HEADS * hd), BF16),
            jax.ShapeDtypeStruct((ATTN_KV_HEADS, ATTN_V_ROWS, seq), BF16),
        ],
        scratch_shapes=[pltpu.VMEM((tm, d), BF16)],
        compiler_params=_params(("parallel",), 48),
        name="attn_proj",
    )(h, g_pre.reshape(1, d), w, cos, sin, cos_t, sin_t, gq_rows, g_k.reshape(1, hd))


def _flash_kernel(qt_ref, k_ref, vt_ref, o_ref, st_ref, m_ref, acc_ref, *, tq):
    hd = ATTN_HEAD_DIM
    tk = ATTN_KV_TILE
    npair = k_ref.shape[0] // (2 * tk)
    nqb = qt_ref.shape[1] // tq

    def queries(qb):
        start = pl.multiple_of(qb * tq, tq)
        return jnp.concatenate([qt_ref[i * hd:(i + 1) * hd, pl.ds(start, tq)] for i in range(ATTN_GROUP)], axis=1)

    def scores(j, slot, qt):
        start = pl.multiple_of(j * tk, tk)
        st_ref[slot] = _dot(k_ref[pl.ds(start, tk), :], qt)

    def consume(j, slot):
        start = pl.multiple_of(j * tk, tk)
        st = st_ref[slot]
        m_prev = m_ref[...]
        m_next = jnp.maximum(m_prev, jnp.max(st, axis=0, keepdims=True))
        alpha = jnp.exp2(m_prev - m_next)
        pt = jnp.exp2(st - m_next)
        acc_ref[...] = alpha * acc_ref[...] + _dot(vt_ref[:, pl.ds(start, tk)], pt.astype(BF16))
        m_ref[...] = m_next

    def pair(j, qt, qt_after):
        scores(j + 1, 1, qt)
        consume(j, 0)
        if qt_after is None:
            scores(j + 2, 0, qt)
        else:
            scores(0, 0, qt_after)
        consume(j + 1, 1)

    scores(0, 0, queries(0))

    def query_block(qb, carry):
        qt = queries(qb)
        m_ref[...] = jnp.full_like(m_ref, -jnp.inf)
        acc_ref[...] = jnp.zeros_like(acc_ref)

        def body(jj, c):
            pair(2 * jj, qt, None)
            return c

        lax.fori_loop(0, npair - 1, body, 0)
        pair(2 * (npair - 1), qt, queries(jnp.minimum(qb + 1, nqb - 1)))
        ot = acc_ref[:hd, :] / acc_ref[hd:hd + 1, :]
        rows = pl.ds(pl.multiple_of(qb * tq, tq), tq)
        for i in range(ATTN_GROUP):
            o_ref[rows, i * hd:(i + 1) * hd] = ot[:, i * tq:(i + 1) * tq].T.astype(o_ref.dtype)
        return carry

    lax.fori_loop(0, nqb, query_block, 0)


def flash_attention(qt, k, vt, tq=256):
    seq = k.shape[0]
    hd = ATTN_HEAD_DIM
    gw = ATTN_GROUP * hd
    once = pl.Buffered(1)
    return pl.pallas_call(
        functools.partial(_flash_kernel, tq=tq),
        grid=(ATTN_KV_HEADS,),
        in_specs=[
            pl.BlockSpec((gw, seq), lambda g: (g, 0), pipeline_mode=once),
            pl.BlockSpec((seq, hd), lambda g: (0, g), pipeline_mode=once),
            pl.BlockSpec((None, ATTN_V_ROWS, seq), lambda g: (g, 0, 0), pipeline_mode=once),
        ],
        out_specs=pl.BlockSpec((seq, gw), lambda g: (0, g)),
        out_shape=jax.ShapeDtypeStruct((seq, ATTN_Q_HEADS * hd), BF16),
        scratch_shapes=[
            pltpu.VMEM((2, ATTN_KV_TILE, ATTN_GROUP * tq), F32),
            pltpu.VMEM((1, ATTN_GROUP * tq), F32),
            pltpu.VMEM((ATTN_V_ROWS, ATTN_GROUP * tq), F32),
        ],
        compiler_params=_params(("parallel",), 56),
        name="flash_attention",
    )(qt, k, vt)


def _chunk_tri(reverse):
    idx = np.arange(GLA_BLOCK)
    same = (idx[:, None] // GLA_CHUNK) == (idx[None, :] // GLA_CHUNK)
    tri = (idx[None, :] >= idx[:, None]) if reverse else (idx[None, :] <= idx[:, None])
    return jnp.asarray((same & tri).astype(np.float32), dtype=BF16)


def _gate_weights(w_gk, first_row):
    w = jnp.zeros((2 * GLA_GATE_RANK, GLA_DK), F32).at[first_row:first_row + GLA_GATE_RANK].set(w_gk)
    return w.astype(BF16)


def gla_layer(h, g_pre, w_in, w_gk_f, b_gk_f, w_gk_b, b_gk_b, g_head, w_out, g_post):
    proj, lr = norm_matmul(h, g_pre, w_in.astype(BF16), GLA_MAIN, tm=256, tn=1024)
    o_f, o_b = gla_scan(proj, lr, w_gk_f, b_gk_f, w_gk_b, b_gk_b)
    return gla_post(o_f, o_b, proj, g_head, w_out.astype(BF16), h, g_post)


def gqa_layer(h, g_pre, w_in, g_q, g_k, w_out, g_post):
    qt, k, vt = attn_proj(h, g_pre, w_in.astype(BF16), g_q, g_k)
    o = flash_attention(qt, k, vt)
    seq, d = o.shape
    nblk = seq // ATTN_Q_BLOCK
    o = o.reshape(nblk, ATTN_Q_BLOCK, d).transpose(1, 0, 2).reshape(seq, d)
    return matmul_norm_res(o, w_out, h, g_post)


def kernel(x, p, g_pre_mix, g_post_mix, g_pre_mlp, g_post_mlp, g_ple, w_mlp_up, w_mlp_down,
           w_ple_proj, w_ple_gate, gla_w_in, gla_w_gk_fwd, gla_b_gk_fwd, gla_w_gk_bwd,
           gla_b_gk_bwd, gla_g_head, gla_w_out, attn_w_in, attn_g_q, attn_g_k, attn_w_out):
    assert x.shape == (1, SEQ, D_MODEL)
    h = x[0]
    for i in range(DEPTH):
        j = i // 2
        if i % 2 == 0:
            h = gla_layer(h, g_pre_mix[i], gla_w_in[j], gla_w_gk_fwd[j], gla_b_gk_fwd[j],
                          gla_w_gk_bwd[j], gla_b_gk_bwd[j], gla_g_head[j], gla_w_out[j], g_post_mix[i])
        else:
            h = gqa_layer(h, g_pre_mix[i], attn_w_in[j], attn_g_q[j], attn_g_k[j], attn_w_out[j],
                          g_post_mix[i])
        h = mlp_block(h, g_pre_mlp[i], w_mlp_up, w_mlp_down, g_post_mlp[i], i)
        h = ple_block(h, p, w_ple_gate, w_ple_proj, g_ple[i], i)
    return h[None]
```

```python
import functools

import numpy as np
import jax
import jax.numpy as jnp
from jax import lax
from jax.experimental import pallas as pl
from jax.experimental.pallas import tpu as pltpu

F32 = jnp.float32
BF16 = jnp.bfloat16

D_MODEL = 2048
SEQ = 8192
DEPTH = 2
GRID_W = 64
PLE_DIM = 256
NORM_EPS = 1e-6
D_FF = 4 * D_MODEL

GLA_HEADS = 4
GLA_DK = D_MODEL // 2
GLA_DV = D_MODEL
GLA_HEAD_K = GLA_DK // GLA_HEADS
GLA_HEAD_V = GLA_DV // GLA_HEADS
GLA_GATE_RANK = 16
GLA_GATE_TAU = 16.0
GLA_CHUNK = 64
GLA_MAIN = 2 * GLA_DK + 2 * GLA_DV
GLA_BLOCK = 256

ATTN_HEAD_DIM = 128
ATTN_Q_HEADS = D_MODEL // ATTN_HEAD_DIM
ATTN_KV_HEADS = 4
ATTN_GROUP = ATTN_Q_HEADS // ATTN_KV_HEADS
ATTN_Q_BLOCK = 128
ATTN_KV_TILE = 1024
ATTN_V_ROWS = ATTN_HEAD_DIM + 16
ROPE_THETA = 10000.0
ROPE_AXIS_DIM = ATTN_HEAD_DIM // 2

LANES = 128
MIB = 1024 * 1024
LOG2E = 1.4426950408889634


def _params(semantics, vmem_mib):
    return pltpu.CompilerParams(dimension_semantics=semantics, vmem_limit_bytes=vmem_mib * MIB)


def _rms(x, gain):
    return x * lax.rsqrt(jnp.mean(x * x, axis=-1, keepdims=True) + NORM_EPS) * gain


def _dot(a, b):
    return jnp.dot(a, b, preferred_element_type=F32)


def _dot_nt(a, b):
    return lax.dot_general(a, b, (((1,), (1,)), ((), ())), preferred_element_type=F32)


def _dot_tn(a, b):
    return lax.dot_general(a, b, (((0,), (0,)), ((), ())), preferred_element_type=F32)


def _resident(shape):
    return pl.BlockSpec(shape, lambda *_: (0,) * len(shape), pipeline_mode=pl.Buffered(1))


def _norm_matmul_kernel(h_ref, g_ref, w_ref, o_ref, ox_ref, u_ref, *, tn):
    n = o_ref.shape[1]
    u_ref[...] = _rms(h_ref[...], g_ref[...]).astype(BF16)
    ox_ref[...] = _dot(u_ref[...], w_ref[:, n:]).astype(ox_ref.dtype)
    for c in range(n // tn):
        sl = slice(c * tn, (c + 1) * tn)
        o_ref[:, sl] = _dot(u_ref[...], w_ref[:, sl]).astype(o_ref.dtype)


def norm_matmul(h, g, w, n, tm, tn):
    m, k = h.shape
    nx = w.shape[1] - n
    return pl.pallas_call(
        functools.partial(_norm_matmul_kernel, tn=tn),
        grid=(m // tm,),
        in_specs=[
            pl.BlockSpec((tm, k), lambda i: (i, 0)),
            _resident((1, k)),
            _resident(w.shape),
        ],
        out_specs=[
            pl.BlockSpec((tm, n), lambda i: (i, 0)),
            pl.BlockSpec((tm, nx), lambda i: (i, 0)),
        ],
        out_shape=[jax.ShapeDtypeStruct((m, n), F32), jax.ShapeDtypeStruct((m, nx), BF16)],
        scratch_shapes=[pltpu.VMEM((tm, k), BF16)],
        compiler_params=_params(("parallel",), 56),
        name="norm_matmul",
    )(h, g.reshape(1, k), w)


def _gla_kernel(qf_ref, kf_ref, vf_ref, lrf_ref, qb_ref, kb_ref, vb_ref, lrb_ref, wf_ref, bf_ref, wb_ref, bb_ref,
                trif_ref, trib_ref, of_ref, ob_ref, s_ref, cum_ref):
    c = GLA_CHUNK
    nchunk = GLA_BLOCK // c

    @pl.when(pl.program_id(0) == 0)
    def _():
        s_ref[...] = jnp.zeros_like(s_ref)

    def cumulative(d, lr_ref, w_ref, b_ref, tri_ref):
        z = _dot(lr_ref[...], w_ref[...]) + b_ref[...]
        la = (jnp.minimum(z, 0.0) - jnp.log1p(jnp.exp(-jnp.abs(z)))) * (1.0 / GLA_GATE_TAU)
        la_hi = la.astype(BF16)
        la_lo = (la - la_hi.astype(F32)).astype(BF16)
        tri = tri_ref[...]
        cum_ref[d] = _dot(tri, la_hi) + _dot(tri, la_lo)

    cumulative(0, lrf_ref, wf_ref, bf_ref, trif_ref)
    cumulative(1, lrb_ref, wb_ref, bb_ref, trib_ref)

    row = lax.broadcasted_iota(jnp.int32, (c, c), 0)
    col = lax.broadcasted_iota(jnp.int32, (c, c), 1)
    directions = (
        (qf_ref, kf_ref, vf_ref, of_ref, col <= row, c // 2 - 1, c - 1),
        (qb_ref, kb_ref, vb_ref, ob_ref, col > row, c // 2, 0),
    )
    for step in range(nchunk):
        for d, (q_ref, k_ref, v_ref, o_ref, mask, ref_row, last_row) in enumerate(directions):
            j = nchunk - 1 - step if d else step
            sl = slice(j * c, (j + 1) * c)
            for h in range(GLA_HEADS):
                hk = slice(h * GLA_HEAD_K, (h + 1) * GLA_HEAD_K)
                hv = slice(h * GLA_HEAD_V, (h + 1) * GLA_HEAD_V)
                cumc = cum_ref[d, sl, hk]
                refc = cumc[ref_row:ref_row + 1]
                lastc = cumc[last_row:last_row + 1]
                qc = q_ref[sl, hk] * (GLA_HEAD_K ** -0.5)
                kc = k_ref[sl, hk]
                vc = v_ref[sl, hv].astype(BF16)
                qt = (qc * jnp.exp(cumc - refc)).astype(BF16)
                kt = (kc * jnp.exp(refc - cumc)).astype(BF16)
                qd = (qc * jnp.exp(cumc)).astype(BF16)
                ke = (kc * jnp.exp(lastc - cumc)).astype(BF16)
                scores = jnp.where(mask, _dot_nt(qt, kt), 0.0).astype(BF16)
                state = s_ref[d * GLA_HEADS + h]
                o_ref[sl, hv] = _dot(scores, vc) + _dot_nt(qd, state.astype(BF16))
                s_ref[d * GLA_HEADS + h] = state * jnp.exp(lastc) + _dot_tn(vc, ke)


def gla_scan(proj, lr, w_f, b_f, w_b, b_b):
    seq = proj.shape[0]
    t = GLA_BLOCK
    nb = seq // t
    fwd = lambda i: i
    bwd = lambda i: nb - 1 - i
    const = lambda i: (0, 0)
    v_blk = 2 * GLA_DK // GLA_DV

    def operands(blk):
        return [
            pl.BlockSpec((t, GLA_DK), lambda i: (blk(i), 0)),
            pl.BlockSpec((t, GLA_DK), lambda i: (blk(i), 1)),
            pl.BlockSpec((t, GLA_DV), lambda i: (blk(i), v_blk)),
            pl.BlockSpec((t, 2 * GLA_GATE_RANK), lambda i: (blk(i), 0)),
        ]

    return pl.pallas_call(
        _gla_kernel,
        grid=(nb,),
        in_specs=operands(fwd) + operands(bwd) + [
            pl.BlockSpec((2 * GLA_GATE_RANK, GLA_DK), const),
            pl.BlockSpec((1, GLA_DK), const),
            pl.BlockSpec((2 * GLA_GATE_RANK, GLA_DK), const),
            pl.BlockSpec((1, GLA_DK), const),
            pl.BlockSpec((t, t), const),
            pl.BlockSpec((t, t), const),
        ],
        out_specs=[
            pl.BlockSpec((t, GLA_DV), lambda i: (fwd(i), 0)),
            pl.BlockSpec((t, GLA_DV), lambda i: (bwd(i), 0)),
        ],
        out_shape=[jax.ShapeDtypeStruct((seq, GLA_DV), F32)] * 2,
        scratch_shapes=[
            pltpu.VMEM((2 * GLA_HEADS, GLA_HEAD_V, GLA_HEAD_K), F32),
            pltpu.VMEM((2, t, GLA_DK), F32),
        ],
        compiler_params=_params(("arbitrary",), 48),
        name="gla_scan",
    )(proj, proj, proj, lr, proj, proj, proj, lr,
      _gate_weights(w_f, 0), b_f.reshape(1, GLA_DK), _gate_weights(w_b, GLA_GATE_RANK), b_b.reshape(1, GLA_DK),
      _chunk_tri(False), _chunk_tri(True))


def _gla_post_kernel(of_ref, ob_ref, og_ref, gh_ref, w_ref, h_ref, gp_ref, o_ref, y_ref):
    for hd in range(GLA_HEADS):
        sl = slice(hd * GLA_HEAD_V, (hd + 1) * GLA_HEAD_V)
        o = _rms(of_ref[:, sl] + ob_ref[:, sl], gh_ref[...])
        og = og_ref[:, sl]
        y_ref[:, sl] = (o * (og / (1.0 + jnp.exp(-og)))).astype(BF16)
    mix = _dot(y_ref[...], w_ref[...])
    o_ref[...] = h_ref[...] + _rms(mix, gp_ref[...])


def gla_post(o_f, o_b, proj, g_head, w_out, h, g_post, tm=512):
    m, d = h.shape
    og_blk = (2 * GLA_DK + GLA_DV) // GLA_DV
    row = lambda i: (i, 0)
    return pl.pallas_call(
        _gla_post_kernel,
        grid=(m // tm,),
        in_specs=[
            pl.BlockSpec((tm, GLA_DV), row),
            pl.BlockSpec((tm, GLA_DV), row),
            pl.BlockSpec((tm, GLA_DV), lambda i: (i, og_blk)),
            _resident((1, GLA_HEAD_V)),
            _resident((GLA_DV, d)),
            pl.BlockSpec((tm, d), row),
            _resident((1, d)),
        ],
        out_specs=pl.BlockSpec((tm, d), row),
        out_shape=jax.ShapeDtypeStruct((m, d), F32),
        scratch_shapes=[pltpu.VMEM((tm, GLA_DV), BF16)],
        compiler_params=_params(("parallel",), 56),
        name="gla_post",
    )(o_f, o_b, proj, g_head.reshape(1, GLA_HEAD_V), w_out, h, g_post.reshape(1, d))


def _matmul_norm_res_kernel(a_ref, w_ref, h_ref, g_ref, o_ref, w_bf_ref):
    @pl.when(pl.program_id(0) == 0)
    def _():
        w_bf_ref[...] = w_ref[...].astype(BF16)

    o_ref[...] = h_ref[...] + _rms(_dot(a_ref[...], w_bf_ref[...]), g_ref[...])


def matmul_norm_res(a, w, h, g, tm=512):
    m, d = h.shape
    k = a.shape[1]
    row = lambda i: (i, 0)
    return pl.pallas_call(
        _matmul_norm_res_kernel,
        grid=(m // tm,),
        in_specs=[
            pl.BlockSpec((tm, k), row),
            _resident((k, d)),
            pl.BlockSpec((tm, d), row),
            _resident((1, d)),
        ],
        out_specs=pl.BlockSpec((tm, d), row),
        out_shape=jax.ShapeDtypeStruct((m, d), F32),
        scratch_shapes=[pltpu.VMEM((k, d), BF16)],
        compiler_params=_params(("arbitrary",), 56),
        name="matmul_norm_res",
    )(a, w, h, g.reshape(1, d))


def _mlp_kernel(h_ref, g1_ref, wu_ref, wd_ref, g2_ref, o_ref, u_ref):
    f = pl.program_id(1)
    last = pl.num_programs(1) - 1

    def partial_product():
        hid = jnp.maximum(_dot(u_ref[...], wu_ref[...].astype(BF16)), 0.0)
        return _dot((hid * hid).astype(BF16), wd_ref[...].astype(BF16))

    @pl.when(f == 0)
    def _():
        u_ref[...] = _rms(h_ref[...], g1_ref[...]).astype(BF16)
        o_ref[...] = partial_product()

    @pl.when(jnp.logical_and(f > 0, f < last))
    def _():
        o_ref[...] += partial_product()

    @pl.when(f == last)
    def _():
        o_ref[...] = h_ref[...] + _rms(o_ref[...] + partial_product(), g2_ref[...])


def mlp_block(h, g_pre, w_up, w_down, g_post, layer, tm=1024, tf=512):
    m, d = h.shape
    ff = w_up.shape[2]
    return pl.pallas_call(
        _mlp_kernel,
        grid=(m // tm, ff // tf),
        in_specs=[
            pl.BlockSpec((tm, d), lambda i, f: (i, 0)),
            _resident((1, d)),
            pl.BlockSpec((None, d, tf), lambda i, f: (layer, 0, f)),
            pl.BlockSpec((None, tf, d), lambda i, f: (layer, f, 0)),
            _resident((1, d)),
        ],
        out_specs=pl.BlockSpec((tm, d), lambda i, f: (i, 0)),
        out_shape=jax.ShapeDtypeStruct((m, d), F32),
        scratch_shapes=[pltpu.VMEM((tm, d), BF16)],
        compiler_params=_params(("parallel", "arbitrary"), 62),
        name="mlp_block",
    )(h, g_pre.reshape(1, d), w_up, w_down, g_post.reshape(1, d))


def _ple_kernel(h_ref, p_ref, wg_ref, wp_ref, g_ref, o_ref, wg_bf_ref, wp_bf_ref):
    @pl.when(pl.program_id(0) == 0)
    def _():
        wg_bf_ref[...] = wg_ref[...].astype(BF16)
        wp_bf_ref[...] = wp_ref[...].astype(BF16)

    x = h_ref[...]
    gate = 1.0 / (1.0 + jnp.exp(-_dot(x.astype(BF16), wg_bf_ref[...])))
    e = _dot(p_ref[...].astype(BF16), wp_bf_ref[...])
    o_ref[...] = x + _rms(gate * e, g_ref[...])


def ple_block(h, p, w_gate, w_proj, g, layer, tm=512):
    m, d = h.shape
    pd = p.shape[-1]
    row = lambda i: (i, 0)
    once = pl.Buffered(1)
    return pl.pallas_call(
        _ple_kernel,
        grid=(m // tm,),
        in_specs=[
            pl.BlockSpec((tm, d), row),
            pl.BlockSpec((None, None, tm, pd), lambda i: (layer, 0, i, 0)),
            pl.BlockSpec((None, d, d), lambda i: (layer, 0, 0), pipeline_mode=once),
            pl.BlockSpec((None, pd, d), lambda i: (layer, 0, 0), pipeline_mode=once),
            _resident((1, d)),
        ],
        out_specs=pl.BlockSpec((tm, d), row),
        out_shape=jax.ShapeDtypeStruct((m, d), F32),
        scratch_shapes=[pltpu.VMEM((d, d), BF16), pltpu.VMEM((pd, d), BF16)],
        compiler_params=_params(("arbitrary",), 56),
        name="ple_block",
    )(h, p, w_gate, w_proj, g.reshape(1, d))


def _rope_tables(seq):
    rows = seq // GRID_W
    t_row = np.repeat(np.arange(rows), GRID_W).astype(np.float64)
    t_col = np.tile(np.arange(GRID_W), rows).astype(np.float64)
    inv_freq = 1.0 / (ROPE_THETA ** (np.arange(0, ROPE_AXIS_DIM, 2, dtype=np.float64) / ROPE_AXIS_DIM))
    a_row = t_row[:, None] * inv_freq
    a_col = t_col[:, None] * inv_freq
    ang = np.concatenate([a_row, a_row, a_col, a_col], axis=1)
    half = ROPE_AXIS_DIM // 2
    sign = np.where((np.arange(ATTN_HEAD_DIM) % ROPE_AXIS_DIM) < half, -1.0, 1.0)
    return np.cos(ang).astype(np.float32), (np.sin(ang) * sign).astype(np.float32)


def _attn_proj_kernel(h_ref, g_ref, w_ref, cos_ref, sin_ref, cost_ref, sint_ref, gq_ref, gk_ref,
                      qt_ref, k_ref, vt_ref, u_ref):
    hd = ATTN_HEAD_DIM
    half = ROPE_AXIS_DIM // 2
    tm = h_ref.shape[0]
    per_dot = ATTN_KV_HEADS
    u_ref[...] = _rms(h_ref[...], g_ref[...]).astype(BF16)

    def project(first_head):
        return _dot(u_ref[...], w_ref[:, first_head * hd:(first_head + per_dot) * hd])

    cos_t = cost_ref[...]
    sin_t = sint_ref[...]
    gq = jnp.tile(gq_ref[...], (1, tm // LANES))
    for c in range(ATTN_Q_HEADS // per_dot):
        x = project(c * per_dot)
        for i in range(per_dot):
            xt = x[:, i * hd:(i + 1) * hd].T
            y = xt * lax.rsqrt(jnp.mean(xt * xt, axis=0, keepdims=True) + NORM_EPS) * gq
            partner = jnp.concatenate([y[half:2 * half], y[:half], y[3 * half:], y[2 * half:3 * half]], axis=0)
            head = c * per_dot + i
            qt_ref[head * hd:(head + 1) * hd, :] = (y * cos_t + partner * sin_t).astype(BF16)

    cos = cos_ref[...]
    sin = sin_ref[...]
    lane = lax.broadcasted_iota(jnp.int32, cos.shape, 1)
    first_half = (lane % ROPE_AXIS_DIM) < half
    xk = project(ATTN_Q_HEADS)
    xv = project(ATTN_Q_HEADS + ATTN_KV_HEADS)
    for i in range(ATTN_KV_HEADS):
        sl = slice(i * hd, (i + 1) * hd)
        y = _rms(xk[:, sl], gk_ref[...])
        partner = jnp.where(first_half, pltpu.roll(y, hd - half, 1), pltpu.roll(y, half, 1))
        k_ref[:, sl] = (y * cos + partner * sin).astype(BF16)
        vt_ref[i, :hd, :] = xv[:, sl].T.astype(BF16)
        vt_ref[i, hd:, :] = jnp.ones((ATTN_V_ROWS - hd, tm), BF16)


def attn_proj(h, g_pre, w, g_q, g_k, tm=512):
    seq, d = h.shape
    hd = ATTN_HEAD_DIM
    cos, sin = _rope_tables(seq)
    q_scale = (hd ** -0.5) * LOG2E
    cos_t, sin_t = jnp.asarray(cos.T * q_scale), jnp.asarray(sin.T * q_scale)
    cos, sin = jnp.asarray(cos), jnp.asarray(sin)
    gq_rows = jnp.broadcast_to(g_q.astype(F32)[:, None], (hd, LANES))
    row = lambda i: (i, 0)
    col = lambda i: (0, i)
    return pl.pallas_call(
        _attn_proj_kernel,
        grid=(seq // tm,),
        in_specs=[
            pl.BlockSpec((tm, d), row),
            _resident((1, d)),
            _resident(w.shape),
            pl.BlockSpec((tm, hd), row),
            pl.BlockSpec((tm, hd), row),
            pl.BlockSpec((hd, tm), col),
            pl.BlockSpec((hd, tm), col),
            _resident((hd, LANES)),
            _resident((1, hd)),
        ],
        out_specs=[
            pl.BlockSpec((ATTN_Q_HEADS * hd, tm), col),
            pl.BlockSpec((tm, ATTN_KV_HEADS * hd), row),
            pl.BlockSpec((ATTN_KV_HEADS, ATTN_V_ROWS, tm), lambda i: (0, 0, i)),
        ],
        out_shape=[
            jax.ShapeDtypeStruct((ATTN_Q_HEADS * hd, seq), BF16),
            jax.ShapeDtypeStruct((seq, ATTN_KV_HEADS * hd), BF16),
            jax.ShapeDtypeStruct((ATTN_KV_HEADS, ATTN_V_ROWS, seq), BF16),
        ],
        scratch_shapes=[pltpu.VMEM((tm, d), BF16)],
        compiler_params=_params(("parallel",), 48),
        name="attn_proj",
    )(h, g_pre.reshape(1, d), w, cos, sin, cos_t, sin_t, gq_rows, g_k.reshape(1, hd))


def _flash_kernel(qt_ref, k_ref, vt_ref, o_ref, st_ref, m_ref, acc_ref, *, tq):
    hd = ATTN_HEAD_DIM
    tk = ATTN_KV_TILE
    npair = k_ref.shape[0] // (2 * tk)
    nqb = qt_ref.shape[1] // tq

    def queries(qb):
        start = pl.multiple_of(qb * tq, tq)
        return jnp.concatenate([qt_ref[i * hd:(i + 1) * hd, pl.ds(start, tq)] for i in range(ATTN_GROUP)], axis=1)

    def scores(j, slot, qt):
        start = pl.multiple_of(j * tk, tk)
        st_ref[slot] = _dot(k_ref[pl.ds(start, tk), :], qt)

    def consume(j, slot):
        start = pl.multiple_of(j * tk, tk)
        st = st_ref[slot]
        m_prev = m_ref[...]
        m_next = jnp.maximum(m_prev, jnp.max(st, axis=0, keepdims=True))
        alpha = jnp.exp2(m_prev - m_next)
        pt = jnp.exp2(st - m_next)
        acc_ref[...] = alpha * acc_ref[...] + _dot(vt_ref[:, pl.ds(start, tk)], pt.astype(BF16))
        m_ref[...] = m_next

    def pair(j, qt, qt_after):
        scores(j + 1, 1, qt)
        consume(j, 0)
        if qt_after is None:
            scores(j + 2, 0, qt)
        else:
            scores(0, 0, qt_after)
        consume(j + 1, 1)

    scores(0, 0, queries(0))

    def query_block(qb, carry):
        qt = queries(qb)
        m_ref[...] = jnp.full_like(m_ref, -jnp.inf)
        acc_ref[...] = jnp.zeros_like(acc_ref)

        def body(jj, c):
            pair(2 * jj, qt, None)
            return c

        lax.fori_loop(0, npair - 1, body, 0)
        pair(2 * (npair - 1), qt, queries(jnp.minimum(qb + 1, nqb - 1)))
        ot = acc_ref[:hd, :] / acc_ref[hd:hd + 1, :]
        rows = pl.ds(pl.multiple_of(qb * tq, tq), tq)
        for i in range(ATTN_GROUP):
            o_ref[rows, i * hd:(i + 1) * hd] = ot[:, i * tq:(i + 1) * tq].T.astype(o_ref.dtype)
        return carry

    lax.fori_loop(0, nqb, query_block, 0)


def flash_attention(qt, k, vt, tq=256):
    seq = k.shape[0]
    hd = ATTN_HEAD_DIM
    gw = ATTN_GROUP * hd
    once = pl.Buffered(1)
    return pl.pallas_call(
        functools.partial(_flash_kernel, tq=tq),
        grid=(ATTN_KV_HEADS,),
        in_specs=[
            pl.BlockSpec((gw, seq), lambda g: (g, 0), pipeline_mode=once),
            pl.BlockSpec((seq, hd), lambda g: (0, g), pipeline_mode=once),
            pl.BlockSpec((None, ATTN_V_ROWS, seq), lambda g: (g, 0, 0), pipeline_mode=once),
        ],
        out_specs=pl.BlockSpec((seq, gw), lambda g: (0, g)),
        out_shape=jax.ShapeDtypeStruct((seq, ATTN_Q_HEADS * hd), BF16),
        scratch_shapes=[
            pltpu.VMEM((2, ATTN_KV_TILE, ATTN_GROUP * tq), F32),
            pltpu.VMEM((1, ATTN_GROUP * tq), F32),
            pltpu.VMEM((ATTN_V_ROWS, ATTN_GROUP * tq), F32),
        ],
        compiler_params=_params(("parallel",), 56),
        name="flash_attention",
    )(qt, k, vt)


def _chunk_tri(reverse):
    idx = np.arange(GLA_BLOCK)
    same = (idx[:, None] // GLA_CHUNK) == (idx[None, :] // GLA_CHUNK)
    tri = (idx[None, :] >= idx[:, None]) if reverse else (idx[None, :] <= idx[:, None])
    return jnp.asarray((same & tri).astype(np.float32), dtype=BF16)


def _gate_weights(w_gk, first_row):
    w = jnp.zeros((2 * GLA_GATE_RANK, GLA_DK), F32).at[first_row:first_row + GLA_GATE_RANK].set(w_gk)
    return w.astype(BF16)


def gla_layer(h, g_pre, w_in, w_gk_f, b_gk_f, w_gk_b, b_gk_b, g_head, w_out, g_post):
    proj, lr = norm_matmul(h, g_pre, w_in.astype(BF16), GLA_MAIN, tm=256, tn=1024)
    o_f, o_b = gla_scan(proj, lr, w_gk_f, b_gk_f, w_gk_b, b_gk_b)
    return gla_post(o_f, o_b, proj, g_head, w_out.astype(BF16), h, g_post)


def gqa_layer(h, g_pre, w_in, g_q, g_k, w_out, g_post):
    qt, k, vt = attn_proj(h, g_pre, w_in.astype(BF16), g_q, g_k)
    o = flash_attention(qt, k, vt)
    seq, d = o.shape
    nblk = seq // ATTN_Q_BLOCK
    o = o.reshape(nblk, ATTN_Q_BLOCK, d).transpose(1, 0, 2).reshape(seq, d)
    return matmul_norm_res(o, w_out, h, g_post)


def kernel(x, p, g_pre_mix, g_post_mix, g_pre_mlp, g_post_mlp, g_ple, w_mlp_up, w_mlp_down,
           w_ple_proj, w_ple_gate, gla_w_in, gla_w_gk_fwd, gla_b_gk_fwd, gla_w_gk_bwd,
           gla_b_gk_bwd, gla_g_head, gla_w_out, attn_w_in, attn_g_q, attn_g_k, attn_w_out):
    assert x.shape == (1, SEQ, D_MODEL)
    h = x[0]
    for i in range(DEPTH):
        j = i // 2
        if i % 2 == 0:
            h = gla_layer(h, g_pre_mix[i], gla_w_in[j], gla_w_gk_fwd[j], gla_b_gk_fwd[j],
                          gla_w_gk_bwd[j], gla_b_gk_bwd[j], gla_g_head[j], gla_w_out[j], g_post_mix[i])
        else:
            h = gqa_layer(h, g_pre_mix[i], attn_w_in[j], attn_g_q[j], attn_g_k[j], attn_w_out[j],
                          g_post_mix[i])
        h = mlp_block(h, g_pre_mlp[i], w_mlp_up, w_mlp_down, g_post_mlp[i], i)
        h = ple_block(h, p, w_ple_gate, w_ple_proj, g_ple[i], i)
    return h[None]
```

```python
import functools

import numpy as np
import jax
import jax.numpy as jnp
from jax import lax
from jax.experimental import pallas as pl
from jax.experimental.pallas import tpu as pltpu

F32 = jnp.float32
BF16 = jnp.bfloat16

D_MODEL = 2048
SEQ = 8192
DEPTH = 2
GRID_W = 64
PLE_DIM = 256
NORM_EPS = 1e-6
D_FF = 4 * D_MODEL

GLA_HEADS = 4
GLA_DK = D_MODEL // 2
GLA_DV = D_MODEL
GLA_HEAD_K = GLA_DK // GLA_HEADS
GLA_HEAD_V = GLA_DV // GLA_HEADS
GLA_GATE_RANK = 16
GLA_GATE_TAU = 16.0
GLA_CHUNK = 64
GLA_MAIN = 2 * GLA_DK + 2 * GLA_DV
GLA_BLOCK = 256

ATTN_HEAD_DIM = 128
ATTN_Q_HEADS = D_MODEL // ATTN_HEAD_DIM
ATTN_KV_HEADS = 4
ATTN_GROUP = ATTN_Q_HEADS // ATTN_KV_HEADS
ATTN_Q_BLOCK = 128
ATTN_KV_TILE = 1024
ATTN_V_ROWS = ATTN_HEAD_DIM + 16
ROPE_THETA = 10000.0
ROPE_AXIS_DIM = ATTN_HEAD_DIM // 2

LANES = 128
MIB = 1024 * 1024
LOG2E = 1.4426950408889634


def _params(semantics, vmem_mib):
    return pltpu.CompilerParams(dimension_semantics=semantics, vmem_limit_bytes=vmem_mib * MIB)


def _rms(x, gain):
    return x * lax.rsqrt(jnp.mean(x * x, axis=-1, keepdims=True) + NORM_EPS) * gain


def _dot(a, b):
    return jnp.dot(a, b, preferred_element_type=F32)


def _dot_nt(a, b):
    return lax.dot_general(a, b, (((1,), (1,)), ((), ())), preferred_element_type=F32)


def _dot_tn(a, b):
    return lax.dot_general(a, b, (((0,), (0,)), ((), ())), preferred_element_type=F32)


def _resident(shape):
    return pl.BlockSpec(shape, lambda *_: (0,) * len(shape), pipeline_mode=pl.Buffered(1))


def _norm_matmul_kernel(h_ref, g_ref, w_ref, o_ref, ox_ref, u_ref, *, tn):
    n = o_ref.shape[1]
    u_ref[...] = _rms(h_ref[...], g_ref[...]).astype(BF16)
    ox_ref[...] = _dot(u_ref[...], w_ref[:, n:]).astype(ox_ref.dtype)
    for c in range(n // tn):
        sl = slice(c * tn, (c + 1) * tn)
        o_ref[:, sl] = _dot(u_ref[...], w_ref[:, sl]).astype(o_ref.dtype)


def norm_matmul(h, g, w, n, tm, tn):
    m, k = h.shape
    nx = w.shape[1] - n
    return pl.pallas_call(
        functools.partial(_norm_matmul_kernel, tn=tn),
        grid=(m // tm,),
        in_specs=[
            pl.BlockSpec((tm, k), lambda i: (i, 0)),
            _resident((1, k)),
            _resident(w.shape),
        ],
        out_specs=[
            pl.BlockSpec((tm, n), lambda i: (i, 0)),
            pl.BlockSpec((tm, nx), lambda i: (i, 0)),
        ],
        out_shape=[jax.ShapeDtypeStruct((m, n), F32), jax.ShapeDtypeStruct((m, nx), BF16)],
        scratch_shapes=[pltpu.VMEM((tm, k), BF16)],
        compiler_params=_params(("parallel",), 56),
        name="norm_matmul",
    )(h, g.reshape(1, k), w)


def _gla_kernel(qf_ref, kf_ref, vf_ref, lrf_ref, qb_ref, kb_ref, vb_ref, lrb_ref, wf_ref, bf_ref, wb_ref, bb_ref,
                trif_ref, trib_ref, of_ref, ob_ref, s_ref, cum_ref):
    c = GLA_CHUNK
    nchunk = GLA_BLOCK // c

    @pl.when(pl.program_id(0) == 0)
    def _():
        s_ref[...] = jnp.zeros_like(s_ref)

    def cumulative(d, lr_ref, w_ref, b_ref, tri_ref):
        z = _dot(lr_ref[...], w_ref[...]) + b_ref[...]
        la = (jnp.minimum(z, 0.0) - jnp.log1p(jnp.exp(-jnp.abs(z)))) * (1.0 / GLA_GATE_TAU)
        la_hi = la.astype(BF16)
        la_lo = (la - la_hi.astype(F32)).astype(BF16)
        tri = tri_ref[...]
        cum_ref[d] = _dot(tri, la_hi) + _dot(tri, la_lo)

    cumulative(0, lrf_ref, wf_ref, bf_ref, trif_ref)
    cumulative(1, lrb_ref, wb_ref, bb_ref, trib_ref)

    row = lax.broadcasted_iota(jnp.int32, (c, c), 0)
    col = lax.broadcasted_iota(jnp.int32, (c, c), 1)
    directions = (
        (qf_ref, kf_ref, vf_ref, of_ref, col <= row, c // 2 - 1, c - 1),
        (qb_ref, kb_ref, vb_ref, ob_ref, col > row, c // 2, 0),
    )
    for step in range(nchunk):
        for d, (q_ref, k_ref, v_ref, o_ref, mask, ref_row, last_row) in enumerate(directions):
            j = nchunk - 1 - step if d else step
            sl = slice(j * c, (j + 1) * c)
            for h in range(GLA_HEADS):
                hk = slice(h * GLA_HEAD_K, (h + 1) * GLA_HEAD_K)
                hv = slice(h * GLA_HEAD_V, (h + 1) * GLA_HEAD_V)
                cumc = cum_ref[d, sl, hk]
                refc = cumc[ref_row:ref_row + 1]
                lastc = cumc[last_row:last_row + 1]
                qc = q_ref[sl, hk] * (GLA_HEAD_K ** -0.5)
                kc = k_ref[sl, hk]
                vc = v_ref[sl, hv].astype(BF16)
                qt = (qc * jnp.exp(cumc - refc)).astype(BF16)
                kt = (kc * jnp.exp(refc - cumc)).astype(BF16)
                qd = (qc * jnp.exp(cumc)).astype(BF16)
                ke = (kc * jnp.exp(lastc - cumc)).astype(BF16)
                scores = jnp.where(mask, _dot_nt(qt, kt), 0.0).astype(BF16)
                state = s_ref[d * GLA_HEADS + h]
                o_ref[sl, hv] = _dot(scores, vc) + _dot_nt(qd, state.astype(BF16))
                s_ref[d * GLA_HEADS + h] = state * jnp.exp(lastc) + _dot_tn(vc, ke)


def gla_scan(proj, lr, w_f, b_f, w_b, b_b):
    seq = proj.shape[0]
    t = GLA_BLOCK
    nb = seq // t
    fwd = lambda i: i
    bwd = lambda i: nb - 1 - i
    const = lambda i: (0, 0)
    v_blk = 2 * GLA_DK // GLA_DV

    def operands(blk):
        return [
            pl.BlockSpec((t, GLA_DK), lambda i: (blk(i), 0)),
            pl.BlockSpec((t, GLA_DK), lambda i: (blk(i), 1)),
            pl.BlockSpec((t, GLA_DV), lambda i: (blk(i), v_blk)),
            pl.BlockSpec((t, 2 * GLA_GATE_RANK), lambda i: (blk(i), 0)),
        ]

    return pl.pallas_call(
        _gla_kernel,
        grid=(nb,),
        in_specs=operands(fwd) + operands(bwd) + [
            pl.BlockSpec((2 * GLA_GATE_RANK, GLA_DK), const),
            pl.BlockSpec((1, GLA_DK), const),
            pl.BlockSpec((2 * GLA_GATE_RANK, GLA_DK), const),
            pl.BlockSpec((1, GLA_DK), const),
            pl.BlockSpec((t, t), const),
            pl.BlockSpec((t, t), const),
        ],
        out_specs=[
            pl.BlockSpec((t, GLA_DV), lambda i: (fwd(i), 0)),
            pl.BlockSpec((t, GLA_DV), lambda i: (bwd(i), 0)),
        ],
        out_shape=[jax.ShapeDtypeStruct((seq, GLA_DV), F32)] * 2,
        scratch_shapes=[
            pltpu.VMEM((2 * GLA_HEADS, GLA_HEAD_V, GLA_HEAD_K), F32),
            pltpu.VMEM((2, t, GLA_DK), F32),
        ],
        compiler_params=_params(("arbitrary",), 48),
        name="gla_scan",
    )(proj, proj, proj, lr, proj, proj, proj, lr,
      _gate_weights(w_f, 0), b_f.reshape(1, GLA_DK), _gate_weights(w_b, GLA_GATE_RANK), b_b.reshape(1, GLA_DK),
      _chunk_tri(False), _chunk_tri(True))


def _gla_post_kernel(of_ref, ob_ref, og_ref, gh_ref, w_ref, h_ref, gp_ref, o_ref, y_ref):
    for hd in range(GLA_HEADS):
        sl = slice(hd * GLA_HEAD_V, (hd + 1) * GLA_HEAD_V)
        o = _rms(of_ref[:, sl] + ob_ref[:, sl], gh_ref[...])
        og = og_ref[:, sl]
        y_ref[:, sl] = (o * (og / (1.0 + jnp.exp(-og)))).astype(BF16)
    mix = _dot(y_ref[...], w_ref[...])
    o_ref[...] = h_ref[...] + _rms(mix, gp_ref[...])


def gla_post(o_f, o_b, proj, g_head, w_out, h, g_post, tm=512):
    m, d = h.shape
    og_blk = (2 * GLA_DK + GLA_DV) // GLA_DV
    row = lambda i: (i, 0)
    return pl.pallas_call(
        _gla_post_kernel,
        grid=(m // tm,),
        in_specs=[
            pl.BlockSpec((tm, GLA_DV), row),
            pl.BlockSpec((tm, GLA_DV), row),
            pl.BlockSpec((tm, GLA_DV), lambda i: (i, og_blk)),
            _resident((1, GLA_HEAD_V)),
            _resident((GLA_DV, d)),
            pl.BlockSpec((tm, d), row),
            _resident((1, d)),
        ],
        out_specs=pl.BlockSpec((tm, d), row),
        out_shape=jax.ShapeDtypeStruct((m, d), F32),
        scratch_shapes=[pltpu.VMEM((tm, GLA_DV), BF16)],
        compiler_params=_params(("parallel",), 56),
        name="gla_post",
    )(o_f, o_b, proj, g_head.reshape(1, GLA_HEAD_V), w_out, h, g_post.reshape(1, d))


def _matmul_norm_res_kernel(a_ref, w_ref, h_ref, g_ref, o_ref, w_bf_ref):
    @pl.when(pl.program_id(0) == 0)
    def _():
        w_bf_ref[...] = w_ref[...].astype(BF16)

    o_ref[...] = h_ref[...] + _rms(_dot(a_ref[...], w_bf_ref[...]), g_ref[...])


def matmul_norm_res(a, w, h, g, tm=512):
    m, d = h.shape
    k = a.shape[1]
    row = lambda i: (i, 0)
    return pl.pallas_call(
        _matmul_norm_res_kernel,
        grid=(m // tm,),
        in_specs=[
            pl.BlockSpec((tm, k), row),
            _resident((k, d)),
            pl.BlockSpec((tm, d), row),
            _resident((1, d)),
        ],
        out_specs=pl.BlockSpec((tm, d), row),
        out_shape=jax.ShapeDtypeStruct((m, d), F32),
        scratch_shapes=[pltpu.VMEM((k, d), BF16)],
        compiler_params=_params(("arbitrary",), 56),
        name="matmul_norm_res",
    )(a, w, h, g.reshape(1, d))


def _mlp_kernel(h_ref, g1_ref, wu_ref, wd_ref, g2_ref, o_ref, u_ref):
    f = pl.program_id(1)
    last = pl.num_programs(1) - 1

    def partial_product():
        hid = jnp.maximum(_dot(u_ref[...], wu_ref[...].astype(BF16)), 0.0)
        return _dot((hid * hid).astype(BF16), wd_ref[...].astype(BF16))

    @pl.when(f == 0)
    def _():
        u_ref[...] = _rms(h_ref[...], g1_ref[...]).astype(BF16)
        o_ref[...] = partial_product()

    @pl.when(jnp.logical_and(f > 0, f < last))
    def _():
        o_ref[...] += partial_product()

    @pl.when(f == last)
    def _():
        o_ref[...] = h_ref[...] + _rms(o_ref[...] + partial_product(), g2_ref[...])


def mlp_block(h, g_pre, w_up, w_down, g_post, layer, tm=1024, tf=512):
    m, d = h.shape
    ff = w_up.shape[2]
    return pl.pallas_call(
        _mlp_kernel,
        grid=(m // tm, ff // tf),
        in_specs=[
            pl.BlockSpec((tm, d), lambda i, f: (i, 0)),
            _resident((1, d)),
            pl.BlockSpec((None, d, tf), lambda i, f: (layer, 0, f)),
            pl.BlockSpec((None, tf, d), lambda i, f: (layer, f, 0)),
            _resident((1, d)),
        ],
        out_specs=pl.BlockSpec((tm, d), lambda i, f: (i, 0)),
        out_shape=jax.ShapeDtypeStruct((m, d), F32),
        scratch_shapes=[pltpu.VMEM((tm, d), BF16)],
        compiler_params=_params(("parallel", "arbitrary"), 62),
        name="mlp_block",
    )(h, g_pre.reshape(1, d), w_up, w_down, g_post.reshape(1, d))


def _ple_kernel(h_ref, p_ref, wg_ref, wp_ref, g_ref, o_ref, wg_bf_ref, wp_bf_ref):
    @pl.when(pl.program_id(0) == 0)
    def _():
        wg_bf_ref[...] = wg_ref[...].astype(BF16)
        wp_bf_ref[...] = wp_ref[...].astype(BF16)

    x = h_ref[...]
    gate = 1.0 / (1.0 + jnp.exp(-_dot(x.astype(BF16), wg_bf_ref[...])))
    e = _dot(p_ref[...].astype(BF16), wp_bf_ref[...])
    o_ref[...] = x + _rms(gate * e, g_ref[...])


def ple_block(h, p, w_gate, w_proj, g, layer, tm=512):
    m, d = h.shape
    pd = p.shape[-1]
    row = lambda i: (i, 0)
    once = pl.Buffered(1)
    return pl.pallas_call(
        _ple_kernel,
        grid=(m // tm,),
        in_specs=[
            pl.BlockSpec((tm, d), row),
            pl.BlockSpec((None, None, tm, pd), lambda i: (layer, 0, i, 0)),
            pl.BlockSpec((None, d, d), lambda i: (layer, 0, 0), pipeline_mode=once),
            pl.BlockSpec((None, pd, d), lambda i: (layer, 0, 0), pipeline_mode=once),
            _resident((1, d)),
        ],
        out_specs=pl.BlockSpec((tm, d), row),
        out_shape=jax.ShapeDtypeStruct((m, d), F32),
        scratch_shapes=[pltpu.VMEM((d, d), BF16), pltpu.VMEM((pd, d), BF16)],
        compiler_params=_params(("arbitrary",), 56),
        name="ple_block",
    )(h, p, w_gate, w_proj, g.reshape(1, d))


def _rope_tables(seq):
    rows = seq // GRID_W
    t_row = np.repeat(np.arange(rows), GRID_W).astype(np.float64)
    t_col = np.tile(np.arange(GRID_W), rows).astype(np.float64)
    inv_freq = 1.0 / (ROPE_THETA ** (np.arange(0, ROPE_AXIS_DIM, 2, dtype=np.float64) / ROPE_AXIS_DIM))
    a_row = t_row[:, None] * inv_freq
    a_col = t_col[:, None] * inv_freq
    ang = np.concatenate([a_row, a_row, a_col, a_col], axis=1)
    half = ROPE_AXIS_DIM // 2
    sign = np.where((np.arange(ATTN_HEAD_DIM) % ROPE_AXIS_DIM) < half, -1.0, 1.0)
    return np.cos(ang).astype(np.float32), (np.sin(ang) * sign).astype(np.float32)


def _attn_proj_kernel(h_ref, g_ref, w_ref, cos_ref, sin_ref, cost_ref, sint_ref, gq_ref, gk_ref,
                      qt_ref, k_ref, vt_ref, u_ref):
    hd = ATTN_HEAD_DIM
    half = ROPE_AXIS_DIM // 2
    tm = h_ref.shape[0]
    per_dot = ATTN_KV_HEADS
    u_ref[...] = _rms(h_ref[...], g_ref[...]).astype(BF16)

    def project(first_head):
        return _dot(u_ref[...], w_ref[:, first_head * hd:(first_head + per_dot) * hd])

    cos_t = cost_ref[...]
    sin_t = sint_ref[...]
    gq = jnp.tile(gq_ref[...], (1, tm // LANES))
    for c in range(ATTN_Q_HEADS // per_dot):
        x = project(c * per_dot)
        for i in range(per_dot):
            xt = x[:, i * hd:(i + 1) * hd].T
            y = xt * lax.rsqrt(jnp.mean(xt * xt, axis=0, keepdims=True) + NORM_EPS) * gq
            partner = jnp.concatenate([y[half:2 * half], y[:half], y[3 * half:], y[2 * half:3 * half]], axis=0)
            head = c * per_dot + i
            qt_ref[head * hd:(head + 1) * hd, :] = (y * cos_t + partner * sin_t).astype(BF16)

    cos = cos_ref[...]
    sin = sin_ref[...]
    lane = lax.broadcasted_iota(jnp.int32, cos.shape, 1)
    first_half = (lane % ROPE_AXIS_DIM) < half
    xk = project(ATTN_Q_HEADS)
    xv = project(ATTN_Q_HEADS + ATTN_KV_HEADS)
    for i in range(ATTN_KV_HEADS):
        sl = slice(i * hd, (i + 1) * hd)
        y = _rms(xk[:, sl], gk_ref[...])
        partner = jnp.where(first_half, pltpu.roll(y, hd - half, 1), pltpu.roll(y, half, 1))
        k_ref[:, sl] = (y * cos + partner * sin).astype(BF16)
        vt_ref[i, :hd, :] = xv[:, sl].T.astype(BF16)
        vt_ref[i, hd:, :] = jnp.ones((ATTN_V_ROWS - hd, tm), BF16)


def attn_proj(h, g_pre, w, g_q, g_k, tm=512):
    seq, d = h.shape
    hd = ATTN_HEAD_DIM
    cos, sin = _rope_tables(seq)
    q_scale = (hd ** -0.5) * LOG2E
    cos_t, sin_t = jnp.asarray(cos.T * q_scale), jnp.asarray(sin.T * q_scale)
    cos, sin = jnp.asarray(cos), jnp.asarray(sin)
    gq_rows = jnp.broadcast_to(g_q.astype(F32)[:, None], (hd, LANES))
    row = lambda i: (i, 0)
    col = lambda i: (0, i)
    return pl.pallas_call(
        _attn_proj_kernel,
        grid=(seq // tm,),
        in_specs=[
            pl.BlockSpec((tm, d), row),
            _resident((1, d)),
            _resident(w.shape),
            pl.BlockSpec((tm, hd), row),
            pl.BlockSpec((tm, hd), row),
            pl.BlockSpec((hd, tm), col),
            pl.BlockSpec((hd, tm), col),
            _resident((hd, LANES)),
            _resident((1, hd)),
        ],
        out_specs=[
            pl.BlockSpec((ATTN_Q_HEADS * hd, tm), col),
            pl.BlockSpec((tm, ATTN_KV_HEADS * hd), row),
            pl.BlockSpec((ATTN_KV_HEADS, ATTN_V_ROWS, tm), lambda i: (0, 0, i)),
        ],
        out_shape=[
            jax.ShapeDtypeStruct((ATTN_Q_HEADS * hd, seq), BF16),
            jax.ShapeDtypeStruct((seq, ATTN_KV_HEADS * hd), BF16),
            jax.ShapeDtypeStruct((ATTN_KV_HEADS, ATTN_V_ROWS, seq), BF16),
        ],
        scratch_shapes=[pltpu.VMEM((tm, d), BF16)],
        compiler_params=_params(("parallel",), 48),
        name="attn_proj",
    )(h, g_pre.reshape(1, d), w, cos, sin, cos_t, sin_t, gq_rows, g_k.reshape(1, hd))


def _flash_kernel(qt_ref, k_ref, vt_ref, o_ref, st_ref, m_ref, acc_ref, *, tq):
    hd = ATTN_HEAD_DIM
    tk = ATTN_KV_TILE
    npair = k_ref.shape[0] // (2 * tk)
    nqb = qt_ref.shape[1] // tq

    def queries(qb):
        start = pl.multiple_of(qb * tq, tq)
        return jnp.concatenate([qt_ref[i * hd:(i + 1) * hd, pl.ds(start, tq)] for i in range(ATTN_GROUP)], axis=1)

    def scores(j, slot, qt):
        start = pl.multiple_of(j * tk, tk)
        st_ref[slot] = _dot(k_ref[pl.ds(start, tk), :], qt)

    def consume(j, slot):
        start = pl.multiple_of(j * tk, tk)
        st = st_ref[slot]
        m_prev = m_ref[...]
        m_next = jnp.maximum(m_prev, jnp.max(st, axis=0, keepdims=True))
        alpha = jnp.exp2(m_prev - m_next)
        pt = jnp.exp2(st - m_next)
        acc_ref[...] = alpha * acc_ref[...] + _dot(vt_ref[:, pl.ds(start, tk)], pt.astype(BF16))
        m_ref[...] = m_next

    def pair(j, qt, qt_after):
        scores(j + 1, 1, qt)
        consume(j, 0)
        if qt_after is None:
            scores(j + 2, 0, qt)
        else:
            scores(0, 0, qt_after)
        consume(j + 1, 1)

    scores(0, 0, queries(0))

    def query_block(qb, carry):
        qt = queries(qb)
        m_ref[...] = jnp.full_like(m_ref, -jnp.inf)
        acc_ref[...] = jnp.zeros_like(acc_ref)

        for jj in range(npair - 1):
            pair(2 * jj, qt, None)
        pair(2 * (npair - 1), qt, queries(jnp.minimum(qb + 1, nqb - 1)))
        ot = acc_ref[:hd, :] / acc_ref[hd:hd + 1, :]
        rows = pl.ds(pl.multiple_of(qb * tq, tq), tq)
        for i in range(ATTN_GROUP):
            o_ref[rows, i * hd:(i + 1) * hd] = ot[:, i * tq:(i + 1) * tq].T.astype(o_ref.dtype)
        return carry

    lax.fori_loop(0, nqb, query_block, 0)


def flash_attention(qt, k, vt, tq=256):
    seq = k.shape[0]
    hd = ATTN_HEAD_DIM
    gw = ATTN_GROUP * hd
    once = pl.Buffered(1)
    return pl.pallas_call(
        functools.partial(_flash_kernel, tq=tq),
        grid=(ATTN_KV_HEADS,),
        in_specs=[
            pl.BlockSpec((gw, seq), lambda g: (g, 0), pipeline_mode=once),
            pl.BlockSpec((seq, hd), lambda g: (0, g), pipeline_mode=once),
            pl.BlockSpec((None, ATTN_V_ROWS, seq), lambda g: (g, 0, 0), pipeline_mode=once),
        ],
        out_specs=pl.BlockSpec((seq, gw), lambda g: (0, g)),
        out_shape=jax.ShapeDtypeStruct((seq, ATTN_Q_HEADS * hd), BF16),
        scratch_shapes=[
            pltpu.VMEM((2, ATTN_KV_TILE, ATTN_GROUP * tq), F32),
            pltpu.VMEM((1, ATTN_GROUP * tq), F32),
            pltpu.VMEM((ATTN_V_ROWS, ATTN_GROUP * tq), F32),
        ],
        compiler_params=_params(("parallel",), 56),
        name="flash_attention",
    )(qt, k, vt)


def _chunk_tri(reverse):
    idx = np.arange(GLA_BLOCK)
    same = (idx[:, None] // GLA_CHUNK) == (idx[None, :] // GLA_CHUNK)
    tri = (idx[None, :] >= idx[:, None]) if reverse else (idx[None, :] <= idx[:, None])
    return jnp.asarray((same & tri).astype(np.float32), dtype=BF16)


def _gate_weights(w_gk, first_row):
    w = jnp.zeros((2 * GLA_GATE_RANK, GLA_DK), F32).at[first_row:first_row + GLA_GATE_RANK].set(w_gk)
    return w.astype(BF16)


def gla_layer(h, g_pre, w_in, w_gk_f, b_gk_f, w_gk_b, b_gk_b, g_head, w_out, g_post):
    proj, lr = norm_matmul(h, g_pre, w_in.astype(BF16), GLA_MAIN, tm=256, tn=1024)
    o_f, o_b = gla_scan(proj, lr, w_gk_f, b_gk_f, w_gk_b, b_gk_b)
    return gla_post(o_f, o_b, proj, g_head, w_out.astype(BF16), h, g_post)


def gqa_layer(h, g_pre, w_in, g_q, g_k, w_out, g_post):
    qt, k, vt = attn_proj(h, g_pre, w_in.astype(BF16), g_q, g_k)
    o = flash_attention(qt, k, vt)
    seq, d = o.shape
    nblk = seq // ATTN_Q_BLOCK
    o = o.reshape(nblk, ATTN_Q_BLOCK, d).transpose(1, 0, 2).reshape(seq, d)
    return matmul_norm_res(o, w_out, h, g_post)


def kernel(x, p, g_pre_mix, g_post_mix, g_pre_mlp, g_post_mlp, g_ple, w_mlp_up, w_mlp_down,
           w_ple_proj, w_ple_gate, gla_w_in, gla_w_gk_fwd, gla_b_gk_fwd, gla_w_gk_bwd,
           gla_b_gk_bwd, gla_g_head, gla_w_out, attn_w_in, attn_g_q, attn_g_k, attn_w_out):
    assert x.shape == (1, SEQ, D_MODEL)
    h = x[0]
    for i in range(DEPTH):
        j = i // 2
        if i % 2 == 0:
            h = gla_layer(h, g_pre_mix[i], gla_w_in[j], gla_w_gk_fwd[j], gla_b_gk_fwd[j],
                          gla_w_gk_bwd[j], gla_b_gk_bwd[j], gla_g_head[j], gla_w_out[j], g_post_mix[i])
        else:
            h = gqa_layer(h, g_pre_mix[i], attn_w_in[j], attn_g_q[j], attn_g_k[j], attn_w_out[j],
                          g_post_mix[i])
        h = mlp_block(h, g_pre_mlp[i], w_mlp_up, w_mlp_down, g_post_mlp[i], i)
        h = ple_block(h, p, w_ple_gate, w_ple_proj, g_ple[i], i)
    return h[None]
```

```python
import functools

import numpy as np
import jax
import jax.numpy as jnp
from jax import lax
from jax.experimental import pallas as pl
from jax.experimental.pallas import tpu as pltpu

F32 = jnp.float32
BF16 = jnp.bfloat16

D_MODEL = 2048
SEQ = 8192
DEPTH = 2
GRID_W = 64
PLE_DIM = 256
NORM_EPS = 1e-6
D_FF = 4 * D_MODEL

GLA_HEADS = 4
GLA_DK = D_MODEL // 2
GLA_DV = D_MODEL
GLA_HEAD_K = GLA_DK // GLA_HEADS
GLA_HEAD_V = GLA_DV // GLA_HEADS
GLA_GATE_RANK = 16
GLA_GATE_TAU = 16.0
GLA_CHUNK = 64
GLA_MAIN = 2 * GLA_DK + 2 * GLA_DV
GLA_BLOCK = 256

ATTN_HEAD_DIM = 128
ATTN_Q_HEADS = D_MODEL // ATTN_HEAD_DIM
ATTN_KV_HEADS = 4
ATTN_GROUP = ATTN_Q_HEADS // ATTN_KV_HEADS
ATTN_Q_BLOCK = 128
ATTN_KV_TILE = 1024
ATTN_V_ROWS = ATTN_HEAD_DIM + 16
ROPE_THETA = 10000.0
ROPE_AXIS_DIM = ATTN_HEAD_DIM // 2

LANES = 128
MIB = 1024 * 1024
LOG2E = 1.4426950408889634


def _params(semantics, vmem_mib):
    return pltpu.CompilerParams(dimension_semantics=semantics, vmem_limit_bytes=vmem_mib * MIB)


def _rms(x, gain):
    return x * lax.rsqrt(jnp.mean(x * x, axis=-1, keepdims=True) + NORM_EPS) * gain


def _dot(a, b):
    return jnp.dot(a, b, preferred_element_type=F32)


def _dot_nt(a, b):
    return lax.dot_general(a, b, (((1,), (1,)), ((), ())), preferred_element_type=F32)


def _dot_tn(a, b):
    return lax.dot_general(a, b, (((0,), (0,)), ((), ())), preferred_element_type=F32)


def _resident(shape):
    return pl.BlockSpec(shape, lambda *_: (0,) * len(shape), pipeline_mode=pl.Buffered(1))


def _norm_matmul_kernel(h_ref, g_ref, w_ref, o_ref, ox_ref, u_ref, *, tn):
    n = o_ref.shape[1]
    u_ref[...] = _rms(h_ref[...], g_ref[...]).astype(BF16)
    ox_ref[...] = _dot(u_ref[...], w_ref[:, n:]).astype(ox_ref.dtype)
    for c in range(n // tn):
        sl = slice(c * tn, (c + 1) * tn)
        o_ref[:, sl] = _dot(u_ref[...], w_ref[:, sl]).astype(o_ref.dtype)


def norm_matmul(h, g, w, n, tm, tn):
    m, k = h.shape
    nx = w.shape[1] - n
    return pl.pallas_call(
        functools.partial(_norm_matmul_kernel, tn=tn),
        grid=(m // tm,),
        in_specs=[
            pl.BlockSpec((tm, k), lambda i: (i, 0)),
            _resident((1, k)),
            _resident(w.shape),
        ],
        out_specs=[
            pl.BlockSpec((tm, n), lambda i: (i, 0)),
            pl.BlockSpec((tm, nx), lambda i: (i, 0)),
        ],
        out_shape=[jax.ShapeDtypeStruct((m, n), F32), jax.ShapeDtypeStruct((m, nx), BF16)],
        scratch_shapes=[pltpu.VMEM((tm, k), BF16)],
        compiler_params=_params(("parallel",), 56),
        name="norm_matmul",
    )(h, g.reshape(1, k), w)


def _gla_kernel(qf_ref, kf_ref, vf_ref, lrf_ref, qb_ref, kb_ref, vb_ref, lrb_ref, wf_ref, bf_ref, wb_ref, bb_ref,
                trif_ref, trib_ref, of_ref, ob_ref, s_ref, cum_ref):
    c = GLA_CHUNK
    nchunk = GLA_BLOCK // c

    @pl.when(pl.program_id(0) == 0)
    def _():
        s_ref[...] = jnp.zeros_like(s_ref)

    def cumulative(d, lr_ref, w_ref, b_ref, tri_ref):
        z = _dot(lr_ref[...], w_ref[...]) + b_ref[...]
        la = (jnp.minimum(z, 0.0) - jnp.log(1.0 + jnp.exp(-jnp.abs(z)))) * (LOG2E / GLA_GATE_TAU)
        la_hi = la.astype(BF16)
        la_lo = (la - la_hi.astype(F32)).astype(BF16)
        tri = tri_ref[...]
        cum_ref[d] = _dot(tri, la_hi) + _dot(tri, la_lo)

    cumulative(0, lrf_ref, wf_ref, bf_ref, trif_ref)
    cumulative(1, lrb_ref, wb_ref, bb_ref, trib_ref)

    row = lax.broadcasted_iota(jnp.int32, (c, c), 0)
    col = lax.broadcasted_iota(jnp.int32, (c, c), 1)
    directions = (
        (qf_ref, kf_ref, vf_ref, of_ref, col <= row, c // 2 - 1, c - 1),
        (qb_ref, kb_ref, vb_ref, ob_ref, col > row, c // 2, 0),
    )
    for step in range(nchunk):
        for d, (q_ref, k_ref, v_ref, o_ref, mask, ref_row, last_row) in enumerate(directions):
            j = nchunk - 1 - step if d else step
            sl = slice(j * c, (j + 1) * c)
            for h in range(GLA_HEADS):
                hk = slice(h * GLA_HEAD_K, (h + 1) * GLA_HEAD_K)
                hv = slice(h * GLA_HEAD_V, (h + 1) * GLA_HEAD_V)
                cumc = cum_ref[d, sl, hk]
                refc = cumc[ref_row:ref_row + 1]
                lastc = cumc[last_row:last_row + 1]
                qc = q_ref[sl, hk] * (GLA_HEAD_K ** -0.5)
                kc = k_ref[sl, hk]
                vc = v_ref[sl, hv].astype(BF16)
                qt = (qc * jnp.exp2(cumc - refc)).astype(BF16)
                kt = (kc * jnp.exp2(refc - cumc)).astype(BF16)
                qd = (qc * jnp.exp2(cumc)).astype(BF16)
                ke = (kc * jnp.exp2(lastc - cumc)).astype(BF16)
                scores = jnp.where(mask, _dot_nt(qt, kt), 0.0).astype(BF16)
                state = s_ref[d * GLA_HEADS + h]
                o_ref[sl, hv] = _dot(scores, vc) + _dot_nt(qd, state.astype(BF16))
                s_ref[d * GLA_HEADS + h] = state * jnp.exp2(lastc) + _dot_tn(vc, ke)


def gla_scan(proj, lr, w_f, b_f, w_b, b_b):
    seq = proj.shape[0]
    t = GLA_BLOCK
    nb = seq // t
    fwd = lambda i: i
    bwd = lambda i: nb - 1 - i
    const = lambda i: (0, 0)
    v_blk = 2 * GLA_DK // GLA_DV

    def operands(blk):
        return [
            pl.BlockSpec((t, GLA_DK), lambda i: (blk(i), 0)),
            pl.BlockSpec((t, GLA_DK), lambda i: (blk(i), 1)),
            pl.BlockSpec((t, GLA_DV), lambda i: (blk(i), v_blk)),
            pl.BlockSpec((t, 2 * GLA_GATE_RANK), lambda i: (blk(i), 0)),
        ]

    return pl.pallas_call(
        _gla_kernel,
        grid=(nb,),
        in_specs=operands(fwd) + operands(bwd) + [
            pl.BlockSpec((2 * GLA_GATE_RANK, GLA_DK), const),
            pl.BlockSpec((1, GLA_DK), const),
            pl.BlockSpec((2 * GLA_GATE_RANK, GLA_DK), const),
            pl.BlockSpec((1, GLA_DK), const),
            pl.BlockSpec((t, t), const),
            pl.BlockSpec((t, t), const),
        ],
        out_specs=[
            pl.BlockSpec((t, GLA_DV), lambda i: (fwd(i), 0)),
            pl.BlockSpec((t, GLA_DV), lambda i: (bwd(i), 0)),
        ],
        out_shape=[jax.ShapeDtypeStruct((seq, GLA_DV), F32)] * 2,
        scratch_shapes=[
            pltpu.VMEM((2 * GLA_HEADS, GLA_HEAD_V, GLA_HEAD_K), F32),
            pltpu.VMEM((2, t, GLA_DK), F32),
        ],
        compiler_params=_params(("arbitrary",), 48),
        name="gla_scan",
    )(proj, proj, proj, lr, proj, proj, proj, lr,
      _gate_weights(w_f, 0), b_f.reshape(1, GLA_DK), _gate_weights(w_b, GLA_GATE_RANK), b_b.reshape(1, GLA_DK),
      _chunk_tri(False), _chunk_tri(True))


def _gla_post_kernel(of_ref, ob_ref, og_ref, gh_ref, w_ref, h_ref, gp_ref, o_ref, y_ref):
    for hd in range(GLA_HEADS):
        sl = slice(hd * GLA_HEAD_V, (hd + 1) * GLA_HEAD_V)
        o = _rms(of_ref[:, sl] + ob_ref[:, sl], gh_ref[...])
        og = og_ref[:, sl]
        y_ref[:, sl] = (o * (og / (1.0 + jnp.exp(-og)))).astype(BF16)
    mix = _dot(y_ref[...], w_ref[...])
    o_ref[...] = h_ref[...] + _rms(mix, gp_ref[...])


def gla_post(o_f, o_b, proj, g_head, w_out, h, g_post, tm=512):
    m, d = h.shape
    og_blk = (2 * GLA_DK + GLA_DV) // GLA_DV
    row = lambda i: (i, 0)
    return pl.pallas_call(
        _gla_post_kernel,
        grid=(m // tm,),
        in_specs=[
            pl.BlockSpec((tm, GLA_DV), row),
            pl.BlockSpec((tm, GLA_DV), row),
            pl.BlockSpec((tm, GLA_DV), lambda i: (i, og_blk)),
            _resident((1, GLA_HEAD_V)),
            _resident((GLA_DV, d)),
            pl.BlockSpec((tm, d), row),
            _resident((1, d)),
        ],
        out_specs=pl.BlockSpec((tm, d), row),
        out_shape=jax.ShapeDtypeStruct((m, d), F32),
        scratch_shapes=[pltpu.VMEM((tm, GLA_DV), BF16)],
        compiler_params=_params(("parallel",), 56),
        name="gla_post",
    )(o_f, o_b, proj, g_head.reshape(1, GLA_HEAD_V), w_out, h, g_post.reshape(1, d))


def _matmul_norm_res_kernel(a_ref, w_ref, h_ref, g_ref, o_ref, w_bf_ref):
    @pl.when(pl.program_id(0) == 0)
    def _():
        w_bf_ref[...] = w_ref[...].astype(BF16)

    o_ref[...] = h_ref[...] + _rms(_dot(a_ref[...], w_bf_ref[...]), g_ref[...])


def matmul_norm_res(a, w, h, g, tm=512):
    m, d = h.shape
    k = a.shape[1]
    row = lambda i: (i, 0)
    return pl.pallas_call(
        _matmul_norm_res_kernel,
        grid=(m // tm,),
        in_specs=[
            pl.BlockSpec((tm, k), row),
            _resident((k, d)),
            pl.BlockSpec((tm, d), row),
            _resident((1, d)),
        ],
        out_specs=pl.BlockSpec((tm, d), row),
        out_shape=jax.ShapeDtypeStruct((m, d), F32),
        scratch_shapes=[pltpu.VMEM((k, d), BF16)],
        compiler_params=_params(("arbitrary",), 56),
        name="matmul_norm_res",
    )(a, w, h, g.reshape(1, d))


def _mlp_kernel(h_ref, g1_ref, wu_ref, wd_ref, g2_ref, o_ref, u_ref):
    f = pl.program_id(1)
    last = pl.num_programs(1) - 1

    def partial_product():
        hid = jnp.maximum(_dot(u_ref[...], wu_ref[...].astype(BF16)), 0.0)
        return _dot((hid * hid).astype(BF16), wd_ref[...].astype(BF16))

    @pl.when(f == 0)
    def _():
        u_ref[...] = _rms(h_ref[...], g1_ref[...]).astype(BF16)
        o_ref[...] = partial_product()

    @pl.when(jnp.logical_and(f > 0, f < last))
    def _():
        o_ref[...] += partial_product()

    @pl.when(f == last)
    def _():
        o_ref[...] = h_ref[...] + _rms(o_ref[...] + partial_product(), g2_ref[...])


def mlp_block(h, g_pre, w_up, w_down, g_post, layer, tm=1024, tf=512):
    m, d = h.shape
    ff = w_up.shape[2]
    return pl.pallas_call(
        _mlp_kernel,
        grid=(m // tm, ff // tf),
        in_specs=[
            pl.BlockSpec((tm, d), lambda i, f: (i, 0)),
            _resident((1, d)),
            pl.BlockSpec((None, d, tf), lambda i, f: (layer, 0, f)),
            pl.BlockSpec((None, tf, d), lambda i, f: (layer, f, 0)),
            _resident((1, d)),
        ],
        out_specs=pl.BlockSpec((tm, d), lambda i, f: (i, 0)),
        out_shape=jax.ShapeDtypeStruct((m, d), F32),
        scratch_shapes=[pltpu.VMEM((tm, d), BF16)],
        compiler_params=_params(("parallel", "arbitrary"), 62),
        name="mlp_block",
    )(h, g_pre.reshape(1, d), w_up, w_down, g_post.reshape(1, d))


def _ple_kernel(h_ref, p_ref, wg_ref, wp_ref, g_ref, o_ref, wg_bf_ref, wp_bf_ref):
    @pl.when(pl.program_id(0) == 0)
    def _():
        wg_bf_ref[...] = wg_ref[...].astype(BF16)
        wp_bf_ref[...] = wp_ref[...].astype(BF16)

    x = h_ref[...]
    gate = 1.0 / (1.0 + jnp.exp(-_dot(x.astype(BF16), wg_bf_ref[...])))
    e = _dot(p_ref[...].astype(BF16), wp_bf_ref[...])
    o_ref[...] = x + _rms(gate * e, g_ref[...])


def ple_block(h, p, w_gate, w_proj, g, layer, tm=512):
    m, d = h.shape
    pd = p.shape[-1]
    row = lambda i: (i, 0)
    once = pl.Buffered(1)
    return pl.pallas_call(
        _ple_kernel,
        grid=(m // tm,),
        in_specs=[
            pl.BlockSpec((tm, d), row),
            pl.BlockSpec((None, None, tm, pd), lambda i: (layer, 0, i, 0)),
            pl.BlockSpec((None, d, d), lambda i: (layer, 0, 0), pipeline_mode=once),
            pl.BlockSpec((None, pd, d), lambda i: (layer, 0, 0), pipeline_mode=once),
            _resident((1, d)),
        ],
        out_specs=pl.BlockSpec((tm, d), row),
        out_shape=jax.ShapeDtypeStruct((m, d), F32),
        scratch_shapes=[pltpu.VMEM((d, d), BF16), pltpu.VMEM((pd, d), BF16)],
        compiler_params=_params(("arbitrary",), 56),
        name="ple_block",
    )(h, p, w_gate, w_proj, g.reshape(1, d))


def _rope_tables(seq):
    rows = seq // GRID_W
    t_row = np.repeat(np.arange(rows), GRID_W).astype(np.float64)
    t_col = np.tile(np.arange(GRID_W), rows).astype(np.float64)
    inv_freq = 1.0 / (ROPE_THETA ** (np.arange(0, ROPE_AXIS_DIM, 2, dtype=np.float64) / ROPE_AXIS_DIM))
    a_row = t_row[:, None] * inv_freq
    a_col = t_col[:, None] * inv_freq
    ang = np.concatenate([a_row, a_row, a_col, a_col], axis=1)
    half = ROPE_AXIS_DIM // 2
    sign = np.where((np.arange(ATTN_HEAD_DIM) % ROPE_AXIS_DIM) < half, -1.0, 1.0)
    return np.cos(ang).astype(np.float32), (np.sin(ang) * sign).astype(np.float32)


def _attn_proj_kernel(h_ref, g_ref, w_ref, cos_ref, sin_ref, cost_ref, sint_ref, gq_ref, gk_ref,
                      qt_ref, k_ref, vt_ref, u_ref):
    hd = ATTN_HEAD_DIM
    half = ROPE_AXIS_DIM // 2
    tm = h_ref.shape[0]
    per_dot = ATTN_KV_HEADS
    u_ref[...] = _rms(h_ref[...], g_ref[...]).astype(BF16)

    def project(first_head):
        return _dot(u_ref[...], w_ref[:, first_head * hd:(first_head + per_dot) * hd])

    cos_t = cost_ref[...]
    sin_t = sint_ref[...]
    gq = jnp.tile(gq_ref[...], (1, tm // LANES))
    for c in range(ATTN_Q_HEADS // per_dot):
        x = project(c * per_dot)
        for i in range(per_dot):
            xt = x[:, i * hd:(i + 1) * hd].T
            y = xt * lax.rsqrt(jnp.mean(xt * xt, axis=0, keepdims=True) + NORM_EPS) * gq
            partner = jnp.concatenate([y[half:2 * half], y[:half], y[3 * half:], y[2 * half:3 * half]], axis=0)
            head = c * per_dot + i
            qt_ref[head * hd:(head + 1) * hd, :] = (y * cos_t + partner * sin_t).astype(BF16)

    cos = cos_ref[...]
    sin = sin_ref[...]
    lane = lax.broadcasted_iota(jnp.int32, cos.shape, 1)
    first_half = (lane % ROPE_AXIS_DIM) < half
    xk = project(ATTN_Q_HEADS)
    xv = project(ATTN_Q_HEADS + ATTN_KV_HEADS)
    for i in range(ATTN_KV_HEADS):
        sl = slice(i * hd, (i + 1) * hd)
        y = _rms(xk[:, sl], gk_ref[...])
        partner = jnp.where(first_half, pltpu.roll(y, hd - half, 1), pltpu.roll(y, half, 1))
        k_ref[:, sl] = (y * cos + partner * sin).astype(BF16)
        vt_ref[i, :hd, :] = xv[:, sl].T.astype(BF16)
        vt_ref[i, hd:, :] = jnp.ones((ATTN_V_ROWS - hd, tm), BF16)


def attn_proj(h, g_pre, w, g_q, g_k, tm=512):
    seq, d = h.shape
    hd = ATTN_HEAD_DIM
    cos, sin = _rope_tables(seq)
    q_scale = (hd ** -0.5) * LOG2E
    cos_t, sin_t = jnp.asarray(cos.T * q_scale), jnp.asarray(sin.T * q_scale)
    cos, sin = jnp.asarray(cos), jnp.asarray(sin)
    gq_rows = jnp.broadcast_to(g_q.astype(F32)[:, None], (hd, LANES))
    row = lambda i: (i, 0)
    col = lambda i: (0, i)
    return pl.pallas_call(
        _attn_proj_kernel,
        grid=(seq // tm,),
        in_specs=[
            pl.BlockSpec((tm, d), row),
            _resident((1, d)),
            _resident(w.shape),
            pl.BlockSpec((tm, hd), row),
            pl.BlockSpec((tm, hd), row),
            pl.BlockSpec((hd, tm), col),
            pl.BlockSpec((hd, tm), col),
            _resident((hd, LANES)),
            _resident((1, hd)),
        ],
        out_specs=[
            pl.BlockSpec((ATTN_Q_HEADS * hd, tm), col),
            pl.BlockSpec((tm, ATTN_KV_HEADS * hd), row),
            pl.BlockSpec((ATTN_KV_HEADS, ATTN_V_ROWS, tm), lambda i: (0, 0, i)),
        ],
        out_shape=[
            jax.ShapeDtypeStruct((ATTN_Q_HEADS * hd, seq), BF16),
            jax.ShapeDtypeStruct((seq, ATTN_KV_HEADS * hd), BF16),
            jax.ShapeDtypeStruct((ATTN_KV_HEADS, ATTN_V_ROWS, seq), BF16),
        ],
        scratch_shapes=[pltpu.VMEM((tm, d), BF16)],
        compiler_params=_params(("parallel",), 48),
        name="attn_proj",
    )(h, g_pre.reshape(1, d), w, cos, sin, cos_t, sin_t, gq_rows, g_k.reshape(1, hd))


def _flash_kernel(qt_ref, k_ref, vt_ref, o_ref, st_ref, m_ref, acc_ref, *, tq):
    hd = ATTN_HEAD_DIM
    tk = ATTN_KV_TILE
    npair = k_ref.shape[0] // (2 * tk)
    nqb = qt_ref.shape[1] // tq

    def queries(qb):
        start = pl.multiple_of(qb * tq, tq)
        return jnp.concatenate([qt_ref[i * hd:(i + 1) * hd, pl.ds(start, tq)] for i in range(ATTN_GROUP)], axis=1)

    def scores(j, slot, qt):
        start = pl.multiple_of(j * tk, tk)
        st_ref[slot] = _dot(k_ref[pl.ds(start, tk), :], qt)

    def consume(j, slot):
        start = pl.multiple_of(j * tk, tk)
        st = st_ref[slot]
        m_prev = m_ref[...]
        m_next = jnp.maximum(m_prev, jnp.max(st, axis=0, keepdims=True))
        alpha = jnp.exp2(m_prev - m_next)
        pt = jnp.exp2(st - m_next)
        acc_ref[...] = alpha * acc_ref[...] + _dot(vt_ref[:, pl.ds(start, tk)], pt.astype(BF16))
        m_ref[...] = m_next

    def pair(j, qt, qt_after):
        scores(j + 1, 1, qt)
        consume(j, 0)
        if qt_after is None:
            scores(j + 2, 0, qt)
        else:
            scores(0, 0, qt_after)
        consume(j + 1, 1)

    scores(0, 0, queries(0))

    def query_block(qb, carry):
        qt = queries(qb)
        m_ref[...] = jnp.full_like(m_ref, -jnp.inf)
        acc_ref[...] = jnp.zeros_like(acc_ref)

        for jj in range(npair - 1):
            pair(2 * jj, qt, None)
        pair(2 * (npair - 1), qt, queries(jnp.minimum(qb + 1, nqb - 1)))
        ot = acc_ref[:hd, :] / acc_ref[hd:hd + 1, :]
        rows = pl.ds(pl.multiple_of(qb * tq, tq), tq)
        for i in range(ATTN_GROUP):
            o_ref[rows, i * hd:(i + 1) * hd] = ot[:, i * tq:(i + 1) * tq].T.astype(o_ref.dtype)
        return carry

    lax.fori_loop(0, nqb, query_block, 0)


def flash_attention(qt, k, vt, tq=256):
    seq = k.shape[0]
    hd = ATTN_HEAD_DIM
    gw = ATTN_GROUP * hd
    once = pl.Buffered(1)
    return pl.pallas_call(
        functools.partial(_flash_kernel, tq=tq),
        grid=(ATTN_KV_HEADS,),
        in_specs=[
            pl.BlockSpec((gw, seq), lambda g: (g, 0), pipeline_mode=once),
            pl.BlockSpec((seq, hd), lambda g: (0, g), pipeline_mode=once),
            pl.BlockSpec((None, ATTN_V_ROWS, seq), lambda g: (g, 0, 0), pipeline_mode=once),
        ],
        out_specs=pl.BlockSpec((seq, gw), lambda g: (0, g)),
        out_shape=jax.ShapeDtypeStruct((seq, ATTN_Q_HEADS * hd), BF16),
        scratch_shapes=[
            pltpu.VMEM((2, ATTN_KV_TILE, ATTN_GROUP * tq), F32),
            pltpu.VMEM((1, ATTN_GROUP * tq), F32),
            pltpu.VMEM((ATTN_V_ROWS, ATTN_GROUP * tq), F32),
        ],
        compiler_params=_params(("parallel",), 56),
        name="flash_attention",
    )(qt, k, vt)


def _chunk_tri(reverse):
    idx = np.arange(GLA_BLOCK)
    same = (idx[:, None] // GLA_CHUNK) == (idx[None, :] // GLA_CHUNK)
    tri = (idx[None, :] >= idx[:, None]) if reverse else (idx[None, :] <= idx[:, None])
    return jnp.asarray((same & tri).astype(np.float32), dtype=BF16)


def _gate_weights(w_gk, first_row):
    w = jnp.zeros((2 * GLA_GATE_RANK, GLA_DK), F32).at[first_row:first_row + GLA_GATE_RANK].set(w_gk)
    return w.astype(BF16)


def gla_layer(h, g_pre, w_in, w_gk_f, b_gk_f, w_gk_b, b_gk_b, g_head, w_out, g_post):
    proj, lr = norm_matmul(h, g_pre, w_in.astype(BF16), GLA_MAIN, tm=256, tn=1024)
    o_f, o_b = gla_scan(proj, lr, w_gk_f, b_gk_f, w_gk_b, b_gk_b)
    return gla_post(o_f, o_b, proj, g_head, w_out.astype(BF16), h, g_post)


def gqa_layer(h, g_pre, w_in, g_q, g_k, w_out, g_post):
    qt, k, vt = attn_proj(h, g_pre, w_in.astype(BF16), g_q, g_k)
    o = flash_attention(qt, k, vt)
    seq, d = o.shape
    nblk = seq // ATTN_Q_BLOCK
    o = o.reshape(nblk, ATTN_Q_BLOCK, d).transpose(1, 0, 2).reshape(seq, d)
    return matmul_norm_res(o, w_out, h, g_post)


def kernel(x, p, g_pre_mix, g_post_mix, g_pre_mlp, g_post_mlp, g_ple, w_mlp_up, w_mlp_down,
           w_ple_proj, w_ple_gate, gla_w_in, gla_w_gk_fwd, gla_b_gk_fwd, gla_w_gk_bwd,
           gla_b_gk_bwd, gla_g_head, gla_w_out, attn_w_in, attn_g_q, attn_g_k, attn_w_out):
    assert x.shape == (1, SEQ, D_MODEL)
    h = x[0]
    for i in range(DEPTH):
        j = i // 2
        if i % 2 == 0:
            h = gla_layer(h, g_pre_mix[i], gla_w_in[j], gla_w_gk_fwd[j], gla_b_gk_fwd[j],
                          gla_w_gk_bwd[j], gla_b_gk_bwd[j], gla_g_head[j], gla_w_out[j], g_post_mix[i])
        else:
            h = gqa_layer(h, g_pre_mix[i], attn_w_in[j], attn_g_q[j], attn_g_k[j], attn_w_out[j],
                          g_post_mix[i])
        h = mlp_block(h, g_pre_mlp[i], w_mlp_up, w_mlp_down, g_post_mlp[i], i)
        h = ple_block(h, p, w_ple_gate, w_ple_proj, g_ple[i], i)
    return h[None]
```

```python
import functools

import numpy as np
import jax
import jax.numpy as jnp
from jax import lax
from jax.experimental import pallas as pl
from jax.experimental.pallas import tpu as pltpu

F32 = jnp.float32
BF16 = jnp.bfloat16

D_MODEL = 2048
SEQ = 8192
DEPTH = 2
GRID_W = 64
PLE_DIM = 256
NORM_EPS = 1e-6
D_FF = 4 * D_MODEL

GLA_HEADS = 4
GLA_DK = D_MODEL // 2
GLA_DV = D_MODEL
GLA_HEAD_K = GLA_DK // GLA_HEADS
GLA_HEAD_V = GLA_DV // GLA_HEADS
GLA_GATE_RANK = 16
GLA_GATE_TAU = 16.0
GLA_CHUNK = 64
GLA_MAIN = 2 * GLA_DK + 2 * GLA_DV
GLA_BLOCK = 256

ATTN_HEAD_DIM = 128
ATTN_Q_HEADS = D_MODEL // ATTN_HEAD_DIM
ATTN_KV_HEADS = 4
ATTN_GROUP = ATTN_Q_HEADS // ATTN_KV_HEADS
ATTN_Q_BLOCK = 128
ATTN_KV_TILE = 1024
ATTN_V_ROWS = ATTN_HEAD_DIM + 16
ROPE_THETA = 10000.0
ROPE_AXIS_DIM = ATTN_HEAD_DIM // 2

LANES = 128
MIB = 1024 * 1024
LOG2E = 1.4426950408889634


def _params(semantics, vmem_mib):
    return pltpu.CompilerParams(dimension_semantics=semantics, vmem_limit_bytes=vmem_mib * MIB)


def _rms(x, gain):
    return x * lax.rsqrt(jnp.mean(x * x, axis=-1, keepdims=True) + NORM_EPS) * gain


def _dot(a, b):
    return jnp.dot(a, b, preferred_element_type=F32)


def _dot_nt(a, b):
    return lax.dot_general(a, b, (((1,), (1,)), ((), ())), preferred_element_type=F32)


def _dot_tn(a, b):
    return lax.dot_general(a, b, (((0,), (0,)), ((), ())), preferred_element_type=F32)


def _resident(shape):
    return pl.BlockSpec(shape, lambda *_: (0,) * len(shape), pipeline_mode=pl.Buffered(1))


def _norm_matmul_kernel(h_ref, g_ref, w_ref, o_ref, ox_ref, u_ref, *, tn):
    n = o_ref.shape[1]
    u_ref[...] = _rms(h_ref[...], g_ref[...]).astype(BF16)
    ox_ref[...] = _dot(u_ref[...], w_ref[:, n:]).astype(ox_ref.dtype)
    for c in range(n // tn):
        sl = slice(c * tn, (c + 1) * tn)
        o_ref[:, sl] = _dot(u_ref[...], w_ref[:, sl]).astype(o_ref.dtype)


def norm_matmul(h, g, w, n, tm, tn):
    m, k = h.shape
    nx = w.shape[1] - n
    return pl.pallas_call(
        functools.partial(_norm_matmul_kernel, tn=tn),
        grid=(m // tm,),
        in_specs=[
            pl.BlockSpec((tm, k), lambda i: (i, 0)),
            _resident((1, k)),
            _resident(w.shape),
        ],
        out_specs=[
            pl.BlockSpec((tm, n), lambda i: (i, 0)),
            pl.BlockSpec((tm, nx), lambda i: (i, 0)),
        ],
        out_shape=[jax.ShapeDtypeStruct((m, n), F32), jax.ShapeDtypeStruct((m, nx), BF16)],
        scratch_shapes=[pltpu.VMEM((tm, k), BF16)],
        compiler_params=_params(("parallel",), 56),
        name="norm_matmul",
    )(h, g.reshape(1, k), w)


def _gla_kernel(qf_ref, kf_ref, vf_ref, lrf_ref, qb_ref, kb_ref, vb_ref, lrb_ref, wf_ref, bf_ref, wb_ref, bb_ref,
                trif_ref, trib_ref, of_ref, ob_ref, s_ref, cum_ref):
    c = GLA_CHUNK
    nchunk = GLA_BLOCK // c

    @pl.when(pl.program_id(0) == 0)
    def _():
        s_ref[...] = jnp.zeros_like(s_ref)

    def cumulative(d, lr_ref, w_ref, b_ref, tri_ref):
        z = _dot(lr_ref[...], w_ref[...]) + b_ref[...]
        la = (jnp.minimum(z, 0.0) - jnp.log(1.0 + jnp.exp(-jnp.abs(z)))) * (LOG2E / GLA_GATE_TAU)
        la_hi = la.astype(BF16)
        la_lo = (la - la_hi.astype(F32)).astype(BF16)
        tri = tri_ref[...]
        cum_ref[d] = _dot(tri, la_hi) + _dot(tri, la_lo)

    cumulative(0, lrf_ref, wf_ref, bf_ref, trif_ref)
    cumulative(1, lrb_ref, wb_ref, bb_ref, trib_ref)

    row = lax.broadcasted_iota(jnp.int32, (c, c), 0)
    col = lax.broadcasted_iota(jnp.int32, (c, c), 1)
    directions = (
        (qf_ref, kf_ref, vf_ref, of_ref, col <= row, c // 2 - 1, c - 1),
        (qb_ref, kb_ref, vb_ref, ob_ref, col > row, c // 2, 0),
    )
    for step in range(nchunk):
        for d, (q_ref, k_ref, v_ref, o_ref, mask, ref_row, last_row) in enumerate(directions):
            j = nchunk - 1 - step if d else step
            sl = slice(j * c, (j + 1) * c)
            for h in range(GLA_HEADS):
                hk = slice(h * GLA_HEAD_K, (h + 1) * GLA_HEAD_K)
                hv = slice(h * GLA_HEAD_V, (h + 1) * GLA_HEAD_V)
                cumc = cum_ref[d, sl, hk]
                refc = cumc[ref_row:ref_row + 1]
                lastc = cumc[last_row:last_row + 1]
                qc = q_ref[sl, hk] * (GLA_HEAD_K ** -0.5)
                kc = k_ref[sl, hk]
                vc = v_ref[sl, hv].astype(BF16)
                qt = (qc * jnp.exp2(cumc - refc)).astype(BF16)
                kt = (kc * jnp.exp2(refc - cumc)).astype(BF16)
                qd = (qc * jnp.exp2(cumc)).astype(BF16)
                ke = (kc * jnp.exp2(lastc - cumc)).astype(BF16)
                scores = jnp.where(mask, _dot_nt(qt, kt), 0.0).astype(BF16)
                state = s_ref[d * GLA_HEADS + h]
                o_ref[sl, hv] = _dot(scores, vc) + _dot_nt(qd, state.astype(BF16))
                s_ref[d * GLA_HEADS + h] = state * jnp.exp2(lastc) + _dot_tn(vc, ke)


def gla_scan(proj, lr, w_f, b_f, w_b, b_b):
    seq = proj.shape[0]
    t = GLA_BLOCK
    nb = seq // t
    fwd = lambda i: i
    bwd = lambda i: nb - 1 - i
    const = lambda i: (0, 0)
    v_blk = 2 * GLA_DK // GLA_DV

    def operands(blk):
        return [
            pl.BlockSpec((t, GLA_DK), lambda i: (blk(i), 0)),
            pl.BlockSpec((t, GLA_DK), lambda i: (blk(i), 1)),
            pl.BlockSpec((t, GLA_DV), lambda i: (blk(i), v_blk)),
            pl.BlockSpec((t, 2 * GLA_GATE_RANK), lambda i: (blk(i), 0)),
        ]

    return pl.pallas_call(
        _gla_kernel,
        grid=(nb,),
        in_specs=operands(fwd) + operands(bwd) + [
            pl.BlockSpec((2 * GLA_GATE_RANK, GLA_DK), const),
            pl.BlockSpec((1, GLA_DK), const),
            pl.BlockSpec((2 * GLA_GATE_RANK, GLA_DK), const),
            pl.BlockSpec((1, GLA_DK), const),
            pl.BlockSpec((t, t), const),
            pl.BlockSpec((t, t), const),
        ],
        out_specs=[
            pl.BlockSpec((t, GLA_DV), lambda i: (fwd(i), 0)),
            pl.BlockSpec((t, GLA_DV), lambda i: (bwd(i), 0)),
        ],
        out_shape=[jax.ShapeDtypeStruct((seq, GLA_DV), F32)] * 2,
        scratch_shapes=[
            pltpu.VMEM((2 * GLA_HEADS, GLA_HEAD_V, GLA_HEAD_K), F32),
            pltpu.VMEM((2, t, GLA_DK), F32),
        ],
        compiler_params=_params(("arbitrary",), 48),
        name="gla_scan",
    )(proj, proj, proj, lr, proj, proj, proj, lr,
      _gate_weights(w_f, 0), b_f.reshape(1, GLA_DK), _gate_weights(w_b, GLA_GATE_RANK), b_b.reshape(1, GLA_DK),
      _chunk_tri(False), _chunk_tri(True))


def _gla_post_kernel(of_ref, ob_ref, og_ref, gh_ref, w_ref, h_ref, gp_ref, o_ref, y_ref):
    for hd in range(GLA_HEADS):
        sl = slice(hd * GLA_HEAD_V, (hd + 1) * GLA_HEAD_V)
        o = _rms(of_ref[:, sl] + ob_ref[:, sl], gh_ref[...])
        og = og_ref[:, sl]
        y_ref[:, sl] = (o * (og / (1.0 + jnp.exp(-og)))).astype(BF16)
    mix = _dot(y_ref[...], w_ref[...])
    o_ref[...] = h_ref[...] + _rms(mix, gp_ref[...])


def gla_post(o_f, o_b, proj, g_head, w_out, h, g_post, tm=512):
    m, d = h.shape
    og_blk = (2 * GLA_DK + GLA_DV) // GLA_DV
    row = lambda i: (i, 0)
    return pl.pallas_call(
        _gla_post_kernel,
        grid=(m // tm,),
        in_specs=[
            pl.BlockSpec((tm, GLA_DV), row),
            pl.BlockSpec((tm, GLA_DV), row),
            pl.BlockSpec((tm, GLA_DV), lambda i: (i, og_blk)),
            _resident((1, GLA_HEAD_V)),
            _resident((GLA_DV, d)),
            pl.BlockSpec((tm, d), row),
            _resident((1, d)),
        ],
        out_specs=pl.BlockSpec((tm, d), row),
        out_shape=jax.ShapeDtypeStruct((m, d), F32),
        scratch_shapes=[pltpu.VMEM((tm, GLA_DV), BF16)],
        compiler_params=_params(("parallel",), 56),
        name="gla_post",
    )(o_f, o_b, proj, g_head.reshape(1, GLA_HEAD_V), w_out, h, g_post.reshape(1, d))


def _matmul_norm_res_kernel(a_ref, w_ref, h_ref, g_ref, o_ref, w_bf_ref):
    @pl.when(pl.program_id(0) == 0)
    def _():
        w_bf_ref[...] = w_ref[...].astype(BF16)

    o_ref[...] = h_ref[...] + _rms(_dot(a_ref[...], w_bf_ref[...]), g_ref[...])


def matmul_norm_res(a, w, h, g, tm=512):
    m, d = h.shape
    k = a.shape[1]
    row = lambda i: (i, 0)
    return pl.pallas_call(
        _matmul_norm_res_kernel,
        grid=(m // tm,),
        in_specs=[
            pl.BlockSpec((tm, k), row),
            _resident((k, d)),
            pl.BlockSpec((tm, d), row),
            _resident((1, d)),
        ],
        out_specs=pl.BlockSpec((tm, d), row),
        out_shape=jax.ShapeDtypeStruct((m, d), F32),
        scratch_shapes=[pltpu.VMEM((k, d), BF16)],
        compiler_params=_params(("arbitrary",), 56),
        name="matmul_norm_res",
    )(a, w, h, g.reshape(1, d))


def _mlp_kernel(h_ref, g1_ref, wu_ref, wd_ref, g2_ref, o_ref, u_ref):
    f = pl.program_id(1)
    last = pl.num_programs(1) - 1

    def partial_product():
        hid = jnp.maximum(_dot(u_ref[...], wu_ref[...].astype(BF16)), 0.0)
        return _dot((hid * hid).astype(BF16), wd_ref[...].astype(BF16))

    @pl.when(f == 0)
    def _():
        u_ref[...] = _rms(h_ref[...], g1_ref[...]).astype(BF16)
        o_ref[...] = partial_product()

    @pl.when(jnp.logical_and(f > 0, f < last))
    def _():
        o_ref[...] += partial_product()

    @pl.when(f == last)
    def _():
        o_ref[...] = h_ref[...] + _rms(o_ref[...] + partial_product(), g2_ref[...])


def mlp_block(h, g_pre, w_up, w_down, g_post, layer, tm=1024, tf=512):
    m, d = h.shape
    ff = w_up.shape[2]
    return pl.pallas_call(
        _mlp_kernel,
        grid=(m // tm, ff // tf),
        in_specs=[
            pl.BlockSpec((tm, d), lambda i, f: (i, 0)),
            _resident((1, d)),
            pl.BlockSpec((None, d, tf), lambda i, f: (layer, 0, f)),
            pl.BlockSpec((None, tf, d), lambda i, f: (layer, f, 0)),
            _resident((1, d)),
        ],
        out_specs=pl.BlockSpec((tm, d), lambda i, f: (i, 0)),
        out_shape=jax.ShapeDtypeStruct((m, d), F32),
        scratch_shapes=[pltpu.VMEM((tm, d), BF16)],
        compiler_params=_params(("parallel", "arbitrary"), 62),
        name="mlp_block",
    )(h, g_pre.reshape(1, d), w_up, w_down, g_post.reshape(1, d))


def _ple_kernel(h_ref, p_ref, wg_ref, wp_ref, g_ref, o_ref, wg_bf_ref, wp_bf_ref):
    @pl.when(pl.program_id(0) == 0)
    def _():
        wg_bf_ref[...] = wg_ref[...].astype(BF16)
        wp_bf_ref[...] = wp_ref[...].astype(BF16)

    x = h_ref[...]
    gate = 1.0 / (1.0 + jnp.exp(-_dot(x.astype(BF16), wg_bf_ref[...])))
    e = _dot(p_ref[...].astype(BF16), wp_bf_ref[...])
    o_ref[...] = x + _rms(gate * e, g_ref[...])


def ple_block(h, p, w_gate, w_proj, g, layer, tm=512):
    m, d = h.shape
    pd = p.shape[-1]
    row = lambda i: (i, 0)
    once = pl.Buffered(1)
    return pl.pallas_call(
        _ple_kernel,
        grid=(m // tm,),
        in_specs=[
            pl.BlockSpec((tm, d), row),
            pl.BlockSpec((None, None, tm, pd), lambda i: (layer, 0, i, 0)),
            pl.BlockSpec((None, d, d), lambda i: (layer, 0, 0), pipeline_mode=once),
            pl.BlockSpec((None, pd, d), lambda i: (layer, 0, 0), pipeline_mode=once),
            _resident((1, d)),
        ],
        out_specs=pl.BlockSpec((tm, d), row),
        out_shape=jax.ShapeDtypeStruct((m, d), F32),
        scratch_shapes=[pltpu.VMEM((d, d), BF16), pltpu.VMEM((pd, d), BF16)],
        compiler_params=_params(("arbitrary",), 56),
        name="ple_block",
    )(h, p, w_gate, w_proj, g.reshape(1, d))


def _rope_tables(seq):
    rows = seq // GRID_W
    t_row = np.repeat(np.arange(rows), GRID_W).astype(np.float64)
    t_col = np.tile(np.arange(GRID_W), rows).astype(np.float64)
    inv_freq = 1.0 / (ROPE_THETA ** (np.arange(0, ROPE_AXIS_DIM, 2, dtype=np.float64) / ROPE_AXIS_DIM))
    a_row = t_row[:, None] * inv_freq
    a_col = t_col[:, None] * inv_freq
    ang = np.concatenate([a_row, a_row, a_col, a_col], axis=1)
    half = ROPE_AXIS_DIM // 2
    sign = np.where((np.arange(ATTN_HEAD_DIM) % ROPE_AXIS_DIM) < half, -1.0, 1.0)
    return np.cos(ang).astype(np.float32), (np.sin(ang) * sign).astype(np.float32)


def _attn_proj_kernel(h_ref, g_ref, w_ref, cos_ref, sin_ref, cost_ref, sint_ref, gq_ref, gk_ref,
                      qt_ref, k_ref, vt_ref, u_ref):
    hd = ATTN_HEAD_DIM
    half = ROPE_AXIS_DIM // 2
    tm = h_ref.shape[0]
    per_dot = ATTN_KV_HEADS
    u_ref[...] = _rms(h_ref[...], g_ref[...]).astype(BF16)

    def project(first_head):
        return _dot(u_ref[...], w_ref[:, first_head * hd:(first_head + per_dot) * hd])

    cos_t = cost_ref[...]
    sin_t = sint_ref[...]
    gq = jnp.tile(gq_ref[...], (1, tm // LANES))
    for c in range(ATTN_Q_HEADS // per_dot):
        x = project(c * per_dot)
        for i in range(per_dot):
            xt = x[:, i * hd:(i + 1) * hd].T
            y = xt * lax.rsqrt(jnp.mean(xt * xt, axis=0, keepdims=True) + NORM_EPS) * gq
            partner = jnp.concatenate([y[half:2 * half], y[:half], y[3 * half:], y[2 * half:3 * half]], axis=0)
            head = c * per_dot + i
            qt_ref[head * hd:(head + 1) * hd, :] = (y * cos_t + partner * sin_t).astype(BF16)

    cos = cos_ref[...]
    sin = sin_ref[...]
    lane = lax.broadcasted_iota(jnp.int32, cos.shape, 1)
    first_half = (lane % ROPE_AXIS_DIM) < half
    xk = project(ATTN_Q_HEADS)
    xv = project(ATTN_Q_HEADS + ATTN_KV_HEADS)
    for i in range(ATTN_KV_HEADS):
        sl = slice(i * hd, (i + 1) * hd)
        y = _rms(xk[:, sl], gk_ref[...])
        partner = jnp.where(first_half, pltpu.roll(y, hd - half, 1), pltpu.roll(y, half, 1))
        k_ref[:, sl] = (y * cos + partner * sin).astype(BF16)
        vt_ref[i, :hd, :] = xv[:, sl].T.astype(BF16)
        vt_ref[i, hd:, :] = jnp.ones((ATTN_V_ROWS - hd, tm), BF16)


def attn_proj(h, g_pre, w, g_q, g_k, tm=512):
    seq, d = h.shape
    hd = ATTN_HEAD_DIM
    cos, sin = _rope_tables(seq)
    q_scale = (hd ** -0.5) * LOG2E
    cos_t, sin_t = jnp.asarray(cos.T * q_scale), jnp.asarray(sin.T * q_scale)
    cos, sin = jnp.asarray(cos), jnp.asarray(sin)
    gq_rows = jnp.broadcast_to(g_q.astype(F32)[:, None], (hd, LANES))
    row = lambda i: (i, 0)
    col = lambda i: (0, i)
    return pl.pallas_call(
        _attn_proj_kernel,
        grid=(seq // tm,),
        in_specs=[
            pl.BlockSpec((tm, d), row),
            _resident((1, d)),
            _resident(w.shape),
            pl.BlockSpec((tm, hd), row),
            pl.BlockSpec((tm, hd), row),
            pl.BlockSpec((hd, tm), col),
            pl.BlockSpec((hd, tm), col),
            _resident((hd, LANES)),
            _resident((1, hd)),
        ],
        out_specs=[
            pl.BlockSpec((ATTN_Q_HEADS * hd, tm), col),
            pl.BlockSpec((tm, ATTN_KV_HEADS * hd), row),
            pl.BlockSpec((ATTN_KV_HEADS, ATTN_V_ROWS, tm), lambda i: (0, 0, i)),
        ],
        out_shape=[
            jax.ShapeDtypeStruct((ATTN_Q_HEADS * hd, seq), BF16),
            jax.ShapeDtypeStruct((seq, ATTN_KV_HEADS * hd), BF16),
            jax.ShapeDtypeStruct((ATTN_KV_HEADS, ATTN_V_ROWS, seq), BF16),
        ],
        scratch_shapes=[pltpu.VMEM((tm, d), BF16)],
        compiler_params=_params(("parallel",), 48),
        name="attn_proj",
    )(h, g_pre.reshape(1, d), w, cos, sin, cos_t, sin_t, gq_rows, g_k.reshape(1, hd))


def _flash_kernel(qt_ref, k_ref, vt_ref, o_ref, st_ref, m_ref, acc_ref, *, tq):
    hd = ATTN_HEAD_DIM
    tk = ATTN_KV_TILE
    npair = k_ref.shape[0] // (2 * tk)
    nqb = qt_ref.shape[1] // tq

    def queries(qb):
        start = pl.multiple_of(qb * tq, tq)
        return jnp.concatenate([qt_ref[i * hd:(i + 1) * hd, pl.ds(start, tq)] for i in range(ATTN_GROUP)], axis=1)

    def scores(j, slot, qt):
        start = pl.multiple_of(j * tk, tk)
        st_ref[slot] = _dot(k_ref[pl.ds(start, tk), :], qt)

    def consume(j, slot):
        start = pl.multiple_of(j * tk, tk)
        st = st_ref[slot]
        m_prev = m_ref[...]
        m_next = jnp.maximum(m_prev, jnp.max(st, axis=0, keepdims=True))
        alpha = jnp.exp2(m_prev - m_next)
        pt = jnp.exp2(st - m_next)
        acc_ref[...] = alpha * acc_ref[...] + _dot(vt_ref[:, pl.ds(start, tk)], pt.astype(BF16))
        m_ref[...] = m_next

    def pair(j, qt, qt_after):
        scores(j + 1, 1, qt)
        consume(j, 0)
        if qt_after is None:
            scores(j + 2, 0, qt)
        else:
            scores(0, 0, qt_after)
        consume(j + 1, 1)

    scores(0, 0, queries(0))

    def query_block(qb, carry):
        qt = queries(qb)
        m_ref[...] = jnp.full_like(m_ref, -jnp.inf)
        acc_ref[...] = jnp.zeros_like(acc_ref)

        for jj in range(npair - 1):
            pair(2 * jj, qt, None)
        pair(2 * (npair - 1), qt, queries(jnp.minimum(qb + 1, nqb - 1)))
        ot = acc_ref[:hd, :] / acc_ref[hd:hd + 1, :]
        rows = pl.ds(pl.multiple_of(qb * tq, tq), tq)
        for i in range(ATTN_GROUP):
            o_ref[rows, i * hd:(i + 1) * hd] = ot[:, i * tq:(i + 1) * tq].T.astype(o_ref.dtype)
        return carry

    lax.fori_loop(0, nqb, query_block, 0)


def flash_attention(qt, k, vt, tq=256):
    seq = k.shape[0]
    hd = ATTN_HEAD_DIM
    gw = ATTN_GROUP * hd
    once = pl.Buffered(1)
    return pl.pallas_call(
        functools.partial(_flash_kernel, tq=tq),
        grid=(ATTN_KV_HEADS,),
        in_specs=[
            pl.BlockSpec((gw, seq), lambda g: (g, 0), pipeline_mode=once),
            pl.BlockSpec((seq, hd), lambda g: (0, g), pipeline_mode=once),
            pl.BlockSpec((None, ATTN_V_ROWS, seq), lambda g: (g, 0, 0), pipeline_mode=once),
        ],
        out_specs=pl.BlockSpec((seq, gw), lambda g: (0, g)),
        out_shape=jax.ShapeDtypeStruct((seq, ATTN_Q_HEADS * hd), BF16),
        scratch_shapes=[
            pltpu.VMEM((2, ATTN_KV_TILE, ATTN_GROUP * tq), F32),
            pltpu.VMEM((1, ATTN_GROUP * tq), F32),
            pltpu.VMEM((ATTN_V_ROWS, ATTN_GROUP * tq), F32),
        ],
        compiler_params=_params(("parallel",), 56),
        name="flash_attention",
    )(qt, k, vt)


def _chunk_tri(reverse):
    idx = np.arange(GLA_BLOCK)
    same = (idx[:, None] // GLA_CHUNK) == (idx[None, :] // GLA_CHUNK)
    tri = (idx[None, :] >= idx[:, None]) if reverse else (idx[None, :] <= idx[:, None])
    return jnp.asarray((same & tri).astype(np.float32), dtype=BF16)


def _gate_weights(w_gk, first_row):
    w = jnp.zeros((2 * GLA_GATE_RANK, GLA_DK), F32).at[first_row:first_row + GLA_GATE_RANK].set(w_gk)
    return w.astype(BF16)


def _round_kernel(x_ref, o_ref):
    o_ref[...] = x_ref[...].astype(o_ref.dtype)


def round_to_bf16(w, tm=256):
    m, n = w.shape
    return pl.pallas_call(
        _round_kernel,
        grid=(m // tm,),
        in_specs=[pl.BlockSpec((tm, n), lambda i: (i, 0))],
        out_specs=pl.BlockSpec((tm, n), lambda i: (i, 0)),
        out_shape=jax.ShapeDtypeStruct((m, n), BF16),
        compiler_params=_params(("parallel",), 48),
        name="round_to_bf16",
    )(w)


def gla_layer(h, g_pre, w_in, w_gk_f, b_gk_f, w_gk_b, b_gk_b, g_head, w_out, g_post):
    proj, lr = norm_matmul(h, g_pre, round_to_bf16(w_in), GLA_MAIN, tm=256, tn=1024)
    o_f, o_b = gla_scan(proj, lr, w_gk_f, b_gk_f, w_gk_b, b_gk_b)
    return gla_post(o_f, o_b, proj, g_head, w_out.astype(BF16), h, g_post)


def gqa_layer(h, g_pre, w_in, g_q, g_k, w_out, g_post):
    qt, k, vt = attn_proj(h, g_pre, w_in.astype(BF16), g_q, g_k)
    o = flash_attention(qt, k, vt)
    seq, d = o.shape
    nblk = seq // ATTN_Q_BLOCK
    o = o.reshape(nblk, ATTN_Q_BLOCK, d).transpose(1, 0, 2).reshape(seq, d)
    return matmul_norm_res(o, w_out, h, g_post)


def kernel(x, p, g_pre_mix, g_post_mix, g_pre_mlp, g_post_mlp, g_ple, w_mlp_up, w_mlp_down,
           w_ple_proj, w_ple_gate, gla_w_in, gla_w_gk_fwd, gla_b_gk_fwd, gla_w_gk_bwd,
           gla_b_gk_bwd, gla_g_head, gla_w_out, attn_w_in, attn_g_q, attn_g_k, attn_w_out):
    assert x.shape == (1, SEQ, D_MODEL)
    h = x[0]
    for i in range(DEPTH):
        j = i // 2
        if i % 2 == 0:
            h = gla_layer(h, g_pre_mix[i], gla_w_in[j], gla_w_gk_fwd[j], gla_b_gk_fwd[j],
                          gla_w_gk_bwd[j], gla_b_gk_bwd[j], gla_g_head[j], gla_w_out[j], g_post_mix[i])
        else:
            h = gqa_layer(h, g_pre_mix[i], attn_w_in[j], attn_g_q[j], attn_g_k[j], attn_w_out[j],
                          g_post_mix[i])
        h = mlp_block(h, g_pre_mlp[i], w_mlp_up, w_mlp_down, g_post_mlp[i], i)
        h = ple_block(h, p, w_ple_gate, w_ple_proj, g_ple[i], i)
    return h[None]
```

```python
import functools

import numpy as np
import jax
import jax.numpy as jnp
from jax import lax
from jax.experimental import pallas as pl
from jax.experimental.pallas import tpu as pltpu

F32 = jnp.float32
BF16 = jnp.bfloat16

D_MODEL = 2048
SEQ = 8192
DEPTH = 2
GRID_W = 64
PLE_DIM = 256
NORM_EPS = 1e-6
D_FF = 4 * D_MODEL

GLA_HEADS = 4
GLA_DK = D_MODEL // 2
GLA_DV = D_MODEL
GLA_HEAD_K = GLA_DK // GLA_HEADS
GLA_HEAD_V = GLA_DV // GLA_HEADS
GLA_GATE_RANK = 16
GLA_GATE_TAU = 16.0
GLA_CHUNK = 64
GLA_MAIN = 2 * GLA_DK + 2 * GLA_DV
GLA_BLOCK = 256

ATTN_HEAD_DIM = 128
ATTN_Q_HEADS = D_MODEL // ATTN_HEAD_DIM
ATTN_KV_HEADS = 4
ATTN_GROUP = ATTN_Q_HEADS // ATTN_KV_HEADS
ATTN_Q_BLOCK = 128
ATTN_KV_TILE = 1024
ATTN_V_ROWS = ATTN_HEAD_DIM + 16
ATTN_AHEAD = 2
ATTN_SCORE_BUFFERS = 2 * ATTN_AHEAD
ROPE_THETA = 10000.0
ROPE_AXIS_DIM = ATTN_HEAD_DIM // 2

LANES = 128
MIB = 1024 * 1024
LOG2E = 1.4426950408889634


def _params(semantics, vmem_mib):
    return pltpu.CompilerParams(dimension_semantics=semantics, vmem_limit_bytes=vmem_mib * MIB)


def _rms(x, gain):
    return x * lax.rsqrt(jnp.mean(x * x, axis=-1, keepdims=True) + NORM_EPS) * gain


def _dot(a, b):
    return jnp.dot(a, b, preferred_element_type=F32)


def _dot_nt(a, b):
    return lax.dot_general(a, b, (((1,), (1,)), ((), ())), preferred_element_type=F32)


def _dot_tn(a, b):
    return lax.dot_general(a, b, (((0,), (0,)), ((), ())), preferred_element_type=F32)


def _resident(shape):
    return pl.BlockSpec(shape, lambda *_: (0,) * len(shape), pipeline_mode=pl.Buffered(1))


def _norm_matmul_kernel(h_ref, g_ref, w_ref, o_ref, ox_ref, u_ref, *, tn):
    n = o_ref.shape[1]
    u_ref[...] = _rms(h_ref[...], g_ref[...]).astype(BF16)
    ox_ref[...] = _dot(u_ref[...], w_ref[:, n:]).astype(ox_ref.dtype)
    for c in range(n // tn):
        sl = slice(c * tn, (c + 1) * tn)
        o_ref[:, sl] = _dot(u_ref[...], w_ref[:, sl]).astype(o_ref.dtype)


def norm_matmul(h, g, w, n, tm, tn):
    m, k = h.shape
    nx = w.shape[1] - n
    return pl.pallas_call(
        functools.partial(_norm_matmul_kernel, tn=tn),
        grid=(m // tm,),
        in_specs=[
            pl.BlockSpec((tm, k), lambda i: (i, 0)),
            _resident((1, k)),
            _resident(w.shape),
        ],
        out_specs=[
            pl.BlockSpec((tm, n), lambda i: (i, 0)),
            pl.BlockSpec((tm, nx), lambda i: (i, 0)),
        ],
        out_shape=[jax.ShapeDtypeStruct((m, n), F32), jax.ShapeDtypeStruct((m, nx), BF16)],
        scratch_shapes=[pltpu.VMEM((tm, k), BF16)],
        compiler_params=_params(("parallel",), 56),
        name="norm_matmul",
    )(h, g.reshape(1, k), w)


def _gla_kernel(qf_ref, kf_ref, vf_ref, lrf_ref, qb_ref, kb_ref, vb_ref, lrb_ref, wf_ref, bf_ref, wb_ref, bb_ref,
                trif_ref, trib_ref, of_ref, ob_ref, s_ref, cum_ref):
    c = GLA_CHUNK
    nchunk = GLA_BLOCK // c

    @pl.when(pl.program_id(0) == 0)
    def _():
        s_ref[...] = jnp.zeros_like(s_ref)

    def cumulative(d, lr_ref, w_ref, b_ref, tri_ref):
        z = _dot(lr_ref[...], w_ref[...]) + b_ref[...]
        la = (jnp.minimum(z, 0.0) - jnp.log(1.0 + jnp.exp(-jnp.abs(z)))) * (LOG2E / GLA_GATE_TAU)
        la_hi = la.astype(BF16)
        la_lo = (la - la_hi.astype(F32)).astype(BF16)
        tri = tri_ref[...]
        cum_ref[d] = _dot(tri, la_hi) + _dot(tri, la_lo)

    cumulative(0, lrf_ref, wf_ref, bf_ref, trif_ref)
    cumulative(1, lrb_ref, wb_ref, bb_ref, trib_ref)

    row = lax.broadcasted_iota(jnp.int32, (c, c), 0)
    col = lax.broadcasted_iota(jnp.int32, (c, c), 1)
    directions = (
        (qf_ref, kf_ref, vf_ref, of_ref, col <= row, c // 2 - 1, c - 1),
        (qb_ref, kb_ref, vb_ref, ob_ref, col > row, c // 2, 0),
    )
    for step in range(nchunk):
        for d, (q_ref, k_ref, v_ref, o_ref, mask, ref_row, last_row) in enumerate(directions):
            j = nchunk - 1 - step if d else step
            sl = slice(j * c, (j + 1) * c)
            for h in range(GLA_HEADS):
                hk = slice(h * GLA_HEAD_K, (h + 1) * GLA_HEAD_K)
                hv = slice(h * GLA_HEAD_V, (h + 1) * GLA_HEAD_V)
                cumc = cum_ref[d, sl, hk]
                refc = cumc[ref_row:ref_row + 1]
                lastc = cumc[last_row:last_row + 1]
                qc = q_ref[sl, hk] * (GLA_HEAD_K ** -0.5)
                kc = k_ref[sl, hk]
                vc = v_ref[sl, hv].astype(BF16)
                qt = (qc * jnp.exp2(cumc - refc)).astype(BF16)
                kt = (kc * jnp.exp2(refc - cumc)).astype(BF16)
                qd = (qc * jnp.exp2(cumc)).astype(BF16)
                ke = (kc * jnp.exp2(lastc - cumc)).astype(BF16)
                scores = jnp.where(mask, _dot_nt(qt, kt), 0.0).astype(BF16)
                state = s_ref[d * GLA_HEADS + h]
                o_ref[sl, hv] = _dot(scores, vc) + _dot_nt(qd, state.astype(BF16))
                s_ref[d * GLA_HEADS + h] = state * jnp.exp2(lastc) + _dot_tn(vc, ke)


def gla_scan(proj, lr, w_f, b_f, w_b, b_b):
    seq = proj.shape[0]
    t = GLA_BLOCK
    nb = seq // t
    fwd = lambda i: i
    bwd = lambda i: nb - 1 - i
    const = lambda i: (0, 0)
    v_blk = 2 * GLA_DK // GLA_DV

    def operands(blk):
        return [
            pl.BlockSpec((t, GLA_DK), lambda i: (blk(i), 0)),
            pl.BlockSpec((t, GLA_DK), lambda i: (blk(i), 1)),
            pl.BlockSpec((t, GLA_DV), lambda i: (blk(i), v_blk)),
            pl.BlockSpec((t, 2 * GLA_GATE_RANK), lambda i: (blk(i), 0)),
        ]

    return pl.pallas_call(
        _gla_kernel,
        grid=(nb,),
        in_specs=operands(fwd) + operands(bwd) + [
            pl.BlockSpec((2 * GLA_GATE_RANK, GLA_DK), const),
            pl.BlockSpec((1, GLA_DK), const),
            pl.BlockSpec((2 * GLA_GATE_RANK, GLA_DK), const),
            pl.BlockSpec((1, GLA_DK), const),
            pl.BlockSpec((t, t), const),
            pl.BlockSpec((t, t), const),
        ],
        out_specs=[
            pl.BlockSpec((t, GLA_DV), lambda i: (fwd(i), 0)),
            pl.BlockSpec((t, GLA_DV), lambda i: (bwd(i), 0)),
        ],
        out_shape=[jax.ShapeDtypeStruct((seq, GLA_DV), F32)] * 2,
        scratch_shapes=[
            pltpu.VMEM((2 * GLA_HEADS, GLA_HEAD_V, GLA_HEAD_K), F32),
            pltpu.VMEM((2, t, GLA_DK), F32),
        ],
        compiler_params=_params(("arbitrary",), 48),
        name="gla_scan",
    )(proj, proj, proj, lr, proj, proj, proj, lr,
      _gate_weights(w_f, 0), b_f.reshape(1, GLA_DK), _gate_weights(w_b, GLA_GATE_RANK), b_b.reshape(1, GLA_DK),
      _chunk_tri(False), _chunk_tri(True))


def _gla_post_kernel(of_ref, ob_ref, og_ref, gh_ref, w_ref, h_ref, gp_ref, o_ref, y_ref):
    for hd in range(GLA_HEADS):
        sl = slice(hd * GLA_HEAD_V, (hd + 1) * GLA_HEAD_V)
        o = _rms(of_ref[:, sl] + ob_ref[:, sl], gh_ref[...])
        og = og_ref[:, sl]
        y_ref[:, sl] = (o * (og / (1.0 + jnp.exp(-og)))).astype(BF16)
    mix = _dot(y_ref[...], w_ref[...])
    o_ref[...] = h_ref[...] + _rms(mix, gp_ref[...])


def gla_post(o_f, o_b, proj, g_head, w_out, h, g_post, tm=512):
    m, d = h.shape
    og_blk = (2 * GLA_DK + GLA_DV) // GLA_DV
    row = lambda i: (i, 0)
    return pl.pallas_call(
        _gla_post_kernel,
        grid=(m // tm,),
        in_specs=[
            pl.BlockSpec((tm, GLA_DV), row),
            pl.BlockSpec((tm, GLA_DV), row),
            pl.BlockSpec((tm, GLA_DV), lambda i: (i, og_blk)),
            _resident((1, GLA_HEAD_V)),
            _resident((GLA_DV, d)),
            pl.BlockSpec((tm, d), row),
            _resident((1, d)),
        ],
        out_specs=pl.BlockSpec((tm, d), row),
        out_shape=jax.ShapeDtypeStruct((m, d), F32),
        scratch_shapes=[pltpu.VMEM((tm, GLA_DV), BF16)],
        compiler_params=_params(("parallel",), 56),
        name="gla_post",
    )(o_f, o_b, proj, g_head.reshape(1, GLA_HEAD_V), w_out, h, g_post.reshape(1, d))


def _matmul_norm_res_kernel(a_ref, w_ref, h_ref, g_ref, o_ref, w_bf_ref):
    @pl.when(pl.program_id(0) == 0)
    def _():
        w_bf_ref[...] = w_ref[...].astype(BF16)

    o_ref[...] = h_ref[...] + _rms(_dot(a_ref[...], w_bf_ref[...]), g_ref[...])


def matmul_norm_res(a, w, h, g, tm=512):
    m, d = h.shape
    k = a.shape[1]
    row = lambda i: (i, 0)
    return pl.pallas_call(
        _matmul_norm_res_kernel,
        grid=(m // tm,),
        in_specs=[
            pl.BlockSpec((tm, k), row),
            _resident((k, d)),
            pl.BlockSpec((tm, d), row),
            _resident((1, d)),
        ],
        out_specs=pl.BlockSpec((tm, d), row),
        out_shape=jax.ShapeDtypeStruct((m, d), F32),
        scratch_shapes=[pltpu.VMEM((k, d), BF16)],
        compiler_params=_params(("arbitrary",), 56),
        name="matmul_norm_res",
    )(a, w, h, g.reshape(1, d))


def _mlp_kernel(h_ref, g1_ref, wu_ref, wd_ref, g2_ref, o_ref, u_ref):
    f = pl.program_id(1)
    last = pl.num_programs(1) - 1

    def partial_product():
        hid = jnp.maximum(_dot(u_ref[...], wu_ref[...].astype(BF16)), 0.0)
        return _dot((hid * hid).astype(BF16), wd_ref[...].astype(BF16))

    @pl.when(f == 0)
    def _():
        u_ref[...] = _rms(h_ref[...], g1_ref[...]).astype(BF16)
        o_ref[...] = partial_product()

    @pl.when(jnp.logical_and(f > 0, f < last))
    def _():
        o_ref[...] += partial_product()

    @pl.when(f == last)
    def _():
        o_ref[...] = h_ref[...] + _rms(o_ref[...] + partial_product(), g2_ref[...])


def mlp_block(h, g_pre, w_up, w_down, g_post, layer, tm=1024, tf=512):
    m, d = h.shape
    ff = w_up.shape[2]
    return pl.pallas_call(
        _mlp_kernel,
        grid=(m // tm, ff // tf),
        in_specs=[
            pl.BlockSpec((tm, d), lambda i, f: (i, 0)),
            _resident((1, d)),
            pl.BlockSpec((None, d, tf), lambda i, f: (layer, 0, f)),
            pl.BlockSpec((None, tf, d), lambda i, f: (layer, f, 0)),
            _resident((1, d)),
        ],
        out_specs=pl.BlockSpec((tm, d), lambda i, f: (i, 0)),
        out_shape=jax.ShapeDtypeStruct((m, d), F32),
        scratch_shapes=[pltpu.VMEM((tm, d), BF16)],
        compiler_params=_params(("parallel", "arbitrary"), 62),
        name="mlp_block",
    )(h, g_pre.reshape(1, d), w_up, w_down, g_post.reshape(1, d))


def _ple_kernel(h_ref, p_ref, wg_ref, wp_ref, g_ref, o_ref, wg_bf_ref, wp_bf_ref):
    @pl.when(pl.program_id(0) == 0)
    def _():
        wg_bf_ref[...] = wg_ref[...].astype(BF16)
        wp_bf_ref[...] = wp_ref[...].astype(BF16)

    x = h_ref[...]
    gate = 1.0 / (1.0 + jnp.exp(-_dot(x.astype(BF16), wg_bf_ref[...])))
    e = _dot(p_ref[...].astype(BF16), wp_bf_ref[...])
    o_ref[...] = x + _rms(gate * e, g_ref[...])


def ple_block(h, p, w_gate, w_proj, g, layer, tm=512):
    m, d = h.shape
    pd = p.shape[-1]
    row = lambda i: (i, 0)
    once = pl.Buffered(1)
    return pl.pallas_call(
        _ple_kernel,
        grid=(m // tm,),
        in_specs=[
            pl.BlockSpec((tm, d), row),
            pl.BlockSpec((None, None, tm, pd), lambda i: (layer, 0, i, 0)),
            pl.BlockSpec((None, d, d), lambda i: (layer, 0, 0), pipeline_mode=once),
            pl.BlockSpec((None, pd, d), lambda i: (layer, 0, 0), pipeline_mode=once),
            _resident((1, d)),
        ],
        out_specs=pl.BlockSpec((tm, d), row),
        out_shape=jax.ShapeDtypeStruct((m, d), F32),
        scratch_shapes=[pltpu.VMEM((d, d), BF16), pltpu.VMEM((pd, d), BF16)],
        compiler_params=_params(("arbitrary",), 56),
        name="ple_block",
    )(h, p, w_gate, w_proj, g.reshape(1, d))


def _rope_tables(seq):
    rows = seq // GRID_W
    t_row = np.repeat(np.arange(rows), GRID_W).astype(np.float64)
    t_col = np.tile(np.arange(GRID_W), rows).astype(np.float64)
    inv_freq = 1.0 / (ROPE_THETA ** (np.arange(0, ROPE_AXIS_DIM, 2, dtype=np.float64) / ROPE_AXIS_DIM))
    a_row = t_row[:, None] * inv_freq
    a_col = t_col[:, None] * inv_freq
    ang = np.concatenate([a_row, a_row, a_col, a_col], axis=1)
    half = ROPE_AXIS_DIM // 2
    sign = np.where((np.arange(ATTN_HEAD_DIM) % ROPE_AXIS_DIM) < half, -1.0, 1.0)
    return np.cos(ang).astype(np.float32), (np.sin(ang) * sign).astype(np.float32)


def _attn_proj_kernel(h_ref, g_ref, w_ref, cos_ref, sin_ref, cost_ref, sint_ref, gq_ref, gk_ref,
                      qt_ref, k_ref, vt_ref, u_ref):
    hd = ATTN_HEAD_DIM
    half = ROPE_AXIS_DIM // 2
    tm = h_ref.shape[0]
    per_dot = ATTN_KV_HEADS
    u_ref[...] = _rms(h_ref[...], g_ref[...]).astype(BF16)

    def project(first_head):
        return _dot(u_ref[...], w_ref[:, first_head * hd:(first_head + per_dot) * hd])

    cos_t = cost_ref[...]
    sin_t = sint_ref[...]
    gq = jnp.tile(gq_ref[...], (1, tm // LANES))
    for c in range(ATTN_Q_HEADS // per_dot):
        x = project(c * per_dot)
        for i in range(per_dot):
            xt = x[:, i * hd:(i + 1) * hd].T
            y = xt * lax.rsqrt(jnp.mean(xt * xt, axis=0, keepdims=True) + NORM_EPS) * gq
            partner = jnp.concatenate([y[half:2 * half], y[:half], y[3 * half:], y[2 * half:3 * half]], axis=0)
            head = c * per_dot + i
            qt_ref[head * hd:(head + 1) * hd, :] = (y * cos_t + partner * sin_t).astype(BF16)

    cos = cos_ref[...]
    sin = sin_ref[...]
    lane = lax.broadcasted_iota(jnp.int32, cos.shape, 1)
    first_half = (lane % ROPE_AXIS_DIM) < half
    xk = project(ATTN_Q_HEADS)
    xv = project(ATTN_Q_HEADS + ATTN_KV_HEADS)
    for i in range(ATTN_KV_HEADS):
        sl = slice(i * hd, (i + 1) * hd)
        y = _rms(xk[:, sl], gk_ref[...])
        partner = jnp.where(first_half, pltpu.roll(y, hd - half, 1), pltpu.roll(y, half, 1))
        k_ref[:, sl] = (y * cos + partner * sin).astype(BF16)
        vt_ref[i, :hd, :] = xv[:, sl].T.astype(BF16)
        vt_ref[i, hd:, :] = jnp.ones((ATTN_V_ROWS - hd, tm), BF16)


def attn_proj(h, g_pre, w, g_q, g_k, tm=512):
    seq, d = h.shape
    hd = ATTN_HEAD_DIM
    cos, sin = _rope_tables(seq)
    q_scale = (hd ** -0.5) * LOG2E
    cos_t, sin_t = jnp.asarray(cos.T * q_scale), jnp.asarray(sin.T * q_scale)
    cos, sin = jnp.asarray(cos), jnp.asarray(sin)
    gq_rows = jnp.broadcast_to(g_q.astype(F32)[:, None], (hd, LANES))
    row = lambda i: (i, 0)
    col = lambda i: (0, i)
    return pl.pallas_call(
        _attn_proj_kernel,
        grid=(seq // tm,),
        in_specs=[
            pl.BlockSpec((tm, d), row),
            _resident((1, d)),
            _resident(w.shape),
            pl.BlockSpec((tm, hd), row),
            pl.BlockSpec((tm, hd), row),
            pl.BlockSpec((hd, tm), col),
            pl.BlockSpec((hd, tm), col),
            _resident((hd, LANES)),
            _resident((1, hd)),
        ],
        out_specs=[
            pl.BlockSpec((ATTN_Q_HEADS * hd, tm), col),
            pl.BlockSpec((tm, ATTN_KV_HEADS * hd), row),
            pl.BlockSpec((ATTN_KV_HEADS, ATTN_V_ROWS, tm), lambda i: (0, 0, i)),
        ],
        out_shape=[
            jax.ShapeDtypeStruct((ATTN_Q_HEADS * hd, seq), BF16),
            jax.ShapeDtypeStruct((seq, ATTN_KV_HEADS * hd), BF16),
            jax.ShapeDtypeStruct((ATTN_KV_HEADS, ATTN_V_ROWS, seq), BF16),
        ],
        scratch_shapes=[pltpu.VMEM((tm, d), BF16)],
        compiler_params=_params(("parallel",), 48),
        name="attn_proj",
    )(h, g_pre.reshape(1, d), w, cos, sin, cos_t, sin_t, gq_rows, g_k.reshape(1, hd))


def _flash_kernel(qt_ref, k_ref, vt_ref, o_ref, st_ref, m_ref, acc_ref, *, tq):
    hd = ATTN_HEAD_DIM
    tk = ATTN_KV_TILE
    npair = k_ref.shape[0] // (2 * tk)
    nqb = qt_ref.shape[1] // tq

    def queries(qb):
        start = pl.multiple_of(qb * tq, tq)
        return jnp.concatenate([qt_ref[i * hd:(i + 1) * hd, pl.ds(start, tq)] for i in range(ATTN_GROUP)], axis=1)

    def scores(j, slot, qt):
        start = pl.multiple_of(j * tk, tk)
        st_ref[slot] = _dot(k_ref[pl.ds(start, tk), :], qt)

    def consume(j, slot):
        start = pl.multiple_of(j * tk, tk)
        st = st_ref[slot]
        m_prev = m_ref[...]
        m_next = jnp.maximum(m_prev, jnp.max(st, axis=0, keepdims=True))
        alpha = jnp.exp2(m_prev - m_next)
        pt = jnp.exp2(st - m_next)
        acc_ref[...] = alpha * acc_ref[...] + _dot(vt_ref[:, pl.ds(start, tk)], pt.astype(BF16))
        m_ref[...] = m_next

    nkb = 2 * npair
    first = queries(0)
    for j in range(ATTN_AHEAD):
        scores(j, j, first)

    def query_block(qb, carry):
        qt = queries(qb)
        qt_next = queries(jnp.minimum(qb + 1, nqb - 1))
        m_ref[...] = jnp.full_like(m_ref, -jnp.inf)
        acc_ref[...] = jnp.zeros_like(acc_ref)
        for j in range(nkb):
            ahead = j + ATTN_AHEAD
            if ahead < nkb:
                scores(ahead, ahead % ATTN_SCORE_BUFFERS, qt)
            else:
                scores(ahead - nkb, (ahead - nkb) % ATTN_SCORE_BUFFERS, qt_next)
            consume(j, j % ATTN_SCORE_BUFFERS)
        ot = acc_ref[:hd, :] / acc_ref[hd:hd + 1, :]
        rows = pl.ds(pl.multiple_of(qb * tq, tq), tq)
        for i in range(ATTN_GROUP):
            o_ref[rows, i * hd:(i + 1) * hd] = ot[:, i * tq:(i + 1) * tq].T.astype(o_ref.dtype)
        return carry

    def two_query_blocks(i, carry):
        query_block(2 * i, carry)
        return query_block(2 * i + 1, carry)

    lax.fori_loop(0, nqb // 2, two_query_blocks, 0)


def flash_attention(qt, k, vt, tq=256):
    seq = k.shape[0]
    hd = ATTN_HEAD_DIM
    gw = ATTN_GROUP * hd
    once = pl.Buffered(1)
    return pl.pallas_call(
        functools.partial(_flash_kernel, tq=tq),
        grid=(ATTN_KV_HEADS,),
        in_specs=[
            pl.BlockSpec((gw, seq), lambda g: (g, 0), pipeline_mode=once),
            pl.BlockSpec((seq, hd), lambda g: (0, g), pipeline_mode=once),
            pl.BlockSpec((None, ATTN_V_ROWS, seq), lambda g: (g, 0, 0), pipeline_mode=once),
        ],
        out_specs=pl.BlockSpec((seq, gw), lambda g: (0, g)),
        out_shape=jax.ShapeDtypeStruct((seq, ATTN_Q_HEADS * hd), BF16),
        scratch_shapes=[
            pltpu.VMEM((ATTN_SCORE_BUFFERS, ATTN_KV_TILE, ATTN_GROUP * tq), F32),
            pltpu.VMEM((1, ATTN_GROUP * tq), F32),
            pltpu.VMEM((ATTN_V_ROWS, ATTN_GROUP * tq), F32),
        ],
        compiler_params=_params(("parallel",), 62),
        name="flash_attention",
    )(qt, k, vt)


def _chunk_tri(reverse):
    idx = np.arange(GLA_BLOCK)
    same = (idx[:, None] // GLA_CHUNK) == (idx[None, :] // GLA_CHUNK)
    tri = (idx[None, :] >= idx[:, None]) if reverse else (idx[None, :] <= idx[:, None])
    return jnp.asarray((same & tri).astype(np.float32), dtype=BF16)


def _gate_weights(w_gk, first_row):
    w = jnp.zeros((2 * GLA_GATE_RANK, GLA_DK), F32).at[first_row:first_row + GLA_GATE_RANK].set(w_gk)
    return w.astype(BF16)


def gla_layer(h, g_pre, w_in, w_gk_f, b_gk_f, w_gk_b, b_gk_b, g_head, w_out, g_post):
    proj, lr = norm_matmul(h, g_pre, w_in.astype(BF16), GLA_MAIN, tm=256, tn=1024)
    o_f, o_b = gla_scan(proj, lr, w_gk_f, b_gk_f, w_gk_b, b_gk_b)
    return gla_post(o_f, o_b, proj, g_head, w_out.astype(BF16), h, g_post)


def gqa_layer(h, g_pre, w_in, g_q, g_k, w_out, g_post):
    qt, k, vt = attn_proj(h, g_pre, w_in.astype(BF16), g_q, g_k)
    o = flash_attention(qt, k, vt)
    seq, d = o.shape
    nblk = seq // ATTN_Q_BLOCK
    o = o.reshape(nblk, ATTN_Q_BLOCK, d).transpose(1, 0, 2).reshape(seq, d)
    return matmul_norm_res(o, w_out, h, g_post)


def kernel(x, p, g_pre_mix, g_post_mix, g_pre_mlp, g_post_mlp, g_ple, w_mlp_up, w_mlp_down,
           w_ple_proj, w_ple_gate, gla_w_in, gla_w_gk_fwd, gla_b_gk_fwd, gla_w_gk_bwd,
           gla_b_gk_bwd, gla_g_head, gla_w_out, attn_w_in, attn_g_q, attn_g_k, attn_w_out):
    assert x.shape == (1, SEQ, D_MODEL)
    h = x[0]
    for i in range(DEPTH):
        j = i // 2
        if i % 2 == 0:
            h = gla_layer(h, g_pre_mix[i], gla_w_in[j], gla_w_gk_fwd[j], gla_b_gk_fwd[j],
                          gla_w_gk_bwd[j], gla_b_gk_bwd[j], gla_g_head[j], gla_w_out[j], g_post_mix[i])
        else:
            h = gqa_layer(h, g_pre_mix[i], attn_w_in[j], attn_g_q[j], attn_g_k[j], attn_w_out[j],
                          g_post_mix[i])
        h = mlp_block(h, g_pre_mlp[i], w_mlp_up, w_mlp_down, g_post_mlp[i], i)
        h = ple_block(h, p, w_ple_gate, w_ple_proj, g_ple[i], i)
    return h[None]
```

```python
import functools

import numpy as np
import jax
import jax.numpy as jnp
from jax import lax
from jax.experimental import pallas as pl
from jax.experimental.pallas import tpu as pltpu

F32 = jnp.float32
BF16 = jnp.bfloat16

D_MODEL = 2048
SEQ = 8192
DEPTH = 2
GRID_W = 64
NORM_EPS = 1e-6

GLA_HEADS = 4
GLA_DK = D_MODEL // 2
GLA_DV = D_MODEL
GLA_HEAD_K = GLA_DK // GLA_HEADS
GLA_HEAD_V = GLA_DV // GLA_HEADS
GLA_GATE_RANK = 16
GLA_GATE_TAU = 16.0
GLA_CHUNK = 64
GLA_MAIN = 2 * GLA_DK + 2 * GLA_DV
GLA_BLOCK = 256

ATTN_HEAD_DIM = 128
ATTN_Q_HEADS = D_MODEL // ATTN_HEAD_DIM
ATTN_KV_HEADS = 4
ATTN_GROUP = ATTN_Q_HEADS // ATTN_KV_HEADS
ATTN_Q_BLOCK = 128
ATTN_KV_TILE = 1024
ATTN_V_ROWS = ATTN_HEAD_DIM + 16
ATTN_AHEAD = 1
ATTN_SCORE_BUFFERS = 2 * ATTN_AHEAD
ROPE_THETA = 10000.0
ROPE_AXIS_DIM = ATTN_HEAD_DIM // 2

LANES = 128
MIB = 1024 * 1024
VMEM_STREAM_MIB = 48
VMEM_RESIDENT_MIB = 56
VMEM_NEAR_FULL_MIB = 62
LOG2E = 1.4426950408889634


def _params(semantics, vmem_mib):
    return pltpu.CompilerParams(dimension_semantics=semantics, vmem_limit_bytes=vmem_mib * MIB)


def _rms(x, gain):
    return x * lax.rsqrt(jnp.mean(x * x, axis=-1, keepdims=True) + NORM_EPS) * gain


def _dot(a, b):
    return jnp.dot(a, b, preferred_element_type=F32)


def _dot_nt(a, b):
    return lax.dot_general(a, b, (((1,), (1,)), ((), ())), preferred_element_type=F32)


def _dot_tn(a, b):
    return lax.dot_general(a, b, (((0,), (0,)), ((), ())), preferred_element_type=F32)


def _resident(shape):
    return pl.BlockSpec(shape, lambda *_: (0,) * len(shape), pipeline_mode=pl.Buffered(1))


def _norm_matmul_kernel(h_ref, g_ref, w_ref, o_ref, ox_ref, u_ref, *, tn):
    n = o_ref.shape[1]
    u_ref[...] = _rms(h_ref[...], g_ref[...]).astype(BF16)
    ox_ref[...] = _dot(u_ref[...], w_ref[:, n:]).astype(ox_ref.dtype)
    for c in range(n // tn):
        sl = slice(c * tn, (c + 1) * tn)
        o_ref[:, sl] = _dot(u_ref[...], w_ref[:, sl]).astype(o_ref.dtype)


def norm_matmul(h, g, w, n, tm, tn):
    m, k = h.shape
    nx = w.shape[1] - n
    return pl.pallas_call(
        functools.partial(_norm_matmul_kernel, tn=tn),
        grid=(m // tm,),
        in_specs=[
            pl.BlockSpec((tm, k), lambda i: (i, 0)),
            _resident((1, k)),
            _resident(w.shape),
        ],
        out_specs=[
            pl.BlockSpec((tm, n), lambda i: (i, 0)),
            pl.BlockSpec((tm, nx), lambda i: (i, 0)),
        ],
        out_shape=[jax.ShapeDtypeStruct((m, n), F32), jax.ShapeDtypeStruct((m, nx), BF16)],
        scratch_shapes=[pltpu.VMEM((tm, k), BF16)],
        compiler_params=_params(("parallel",), VMEM_RESIDENT_MIB),
        name="norm_matmul",
    )(h, g.reshape(1, k), w)


def _gla_kernel(qf_ref, kf_ref, vf_ref, lrf_ref, qb_ref, kb_ref, vb_ref, lrb_ref, wf_ref, bf_ref, wb_ref, bb_ref,
                trif_ref, trib_ref, of_ref, ob_ref, s_ref, cum_ref):
    c = GLA_CHUNK
    nchunk = GLA_BLOCK // c

    @pl.when(pl.program_id(0) == 0)
    def _():
        s_ref[...] = jnp.zeros_like(s_ref)

    def cumulative(d, lr_ref, w_ref, b_ref, tri_ref):
        z = _dot(lr_ref[...], w_ref[...]) + b_ref[...]
        la = (jnp.minimum(z, 0.0) - jnp.log(1.0 + jnp.exp(-jnp.abs(z)))) * (LOG2E / GLA_GATE_TAU)
        la_hi = la.astype(BF16)
        la_lo = (la - la_hi.astype(F32)).astype(BF16)
        tri = tri_ref[...]
        cum_ref[d] = _dot(tri, la_hi) + _dot(tri, la_lo)

    cumulative(0, lrf_ref, wf_ref, bf_ref, trif_ref)
    cumulative(1, lrb_ref, wb_ref, bb_ref, trib_ref)

    row = lax.broadcasted_iota(jnp.int32, (c, c), 0)
    col = lax.broadcasted_iota(jnp.int32, (c, c), 1)
    directions = (
        (qf_ref, kf_ref, vf_ref, of_ref, col <= row, c // 2 - 1, c - 1),
        (qb_ref, kb_ref, vb_ref, ob_ref, col > row, c // 2, 0),
    )
    for step in range(nchunk):
        for d, (q_ref, k_ref, v_ref, o_ref, mask, ref_row, last_row) in enumerate(directions):
            j = nchunk - 1 - step if d else step
            sl = slice(j * c, (j + 1) * c)
            for h in range(GLA_HEADS):
                hk = slice(h * GLA_HEAD_K, (h + 1) * GLA_HEAD_K)
                hv = slice(h * GLA_HEAD_V, (h + 1) * GLA_HEAD_V)
                cumc = cum_ref[d, sl, hk]
                refc = cumc[ref_row:ref_row + 1]
                lastc = cumc[last_row:last_row + 1]
                qc = q_ref[sl, hk] * (GLA_HEAD_K ** -0.5)
                kc = k_ref[sl, hk]
                vc = v_ref[sl, hv].astype(BF16)
                qt = (qc * jnp.exp2(cumc - refc)).astype(BF16)
                kt = (kc * jnp.exp2(refc - cumc)).astype(BF16)
                qd = (qc * jnp.exp2(cumc)).astype(BF16)
                ke = (kc * jnp.exp2(lastc - cumc)).astype(BF16)
                scores = jnp.where(mask, _dot_nt(qt, kt), 0.0).astype(BF16)
                state = s_ref[d * GLA_HEADS + h]
                o_ref[sl, hv] = (_dot(scores, vc) + _dot_nt(qd, state.astype(BF16))).astype(o_ref.dtype)
                s_ref[d * GLA_HEADS + h] = state * jnp.exp2(lastc) + _dot_tn(vc, ke)


def gla_scan(proj, lr, w_f, b_f, w_b, b_b):
    seq = proj.shape[0]
    t = GLA_BLOCK
    nb = seq // t
    fwd = lambda i: i
    bwd = lambda i: nb - 1 - i
    const = lambda i: (0, 0)
    v_blk = 2 * GLA_DK // GLA_DV

    def operands(blk):
        return [
            pl.BlockSpec((t, GLA_DK), lambda i: (blk(i), 0)),
            pl.BlockSpec((t, GLA_DK), lambda i: (blk(i), 1)),
            pl.BlockSpec((t, GLA_DV), lambda i: (blk(i), v_blk)),
            pl.BlockSpec((t, 2 * GLA_GATE_RANK), lambda i: (blk(i), 0)),
        ]

    return pl.pallas_call(
        _gla_kernel,
        grid=(nb,),
        in_specs=operands(fwd) + operands(bwd) + [
            pl.BlockSpec((2 * GLA_GATE_RANK, GLA_DK), const),
            pl.BlockSpec((1, GLA_DK), const),
            pl.BlockSpec((2 * GLA_GATE_RANK, GLA_DK), const),
            pl.BlockSpec((1, GLA_DK), const),
            pl.BlockSpec((t, t), const),
            pl.BlockSpec((t, t), const),
        ],
        out_specs=[
            pl.BlockSpec((t, GLA_DV), lambda i: (fwd(i), 0)),
            pl.BlockSpec((t, GLA_DV), lambda i: (bwd(i), 0)),
        ],
        out_shape=[jax.ShapeDtypeStruct((seq, GLA_DV), BF16)] * 2,
        scratch_shapes=[
            pltpu.VMEM((2 * GLA_HEADS, GLA_HEAD_V, GLA_HEAD_K), F32),
            pltpu.VMEM((2, t, GLA_DK), F32),
        ],
        compiler_params=_params(("arbitrary",), VMEM_STREAM_MIB),
        name="gla_scan",
    )(proj, proj, proj, lr, proj, proj, proj, lr,
      _gate_weights(w_f, 0), b_f.reshape(1, GLA_DK), _gate_weights(w_b, GLA_GATE_RANK), b_b.reshape(1, GLA_DK),
      _chunk_tri(False), _chunk_tri(True))


def _gla_post_kernel(of_ref, ob_ref, og_ref, gh_ref, w_ref, h_ref, gp_ref, o_ref, y_ref):
    for hd in range(GLA_HEADS):
        sl = slice(hd * GLA_HEAD_V, (hd + 1) * GLA_HEAD_V)
        o = _rms(of_ref[:, sl].astype(F32) + ob_ref[:, sl].astype(F32), gh_ref[...])
        og = og_ref[:, sl]
        y_ref[:, sl] = (o * (og / (1.0 + jnp.exp(-og)))).astype(BF16)
    mix = _dot(y_ref[...], w_ref[...])
    o_ref[...] = h_ref[...] + _rms(mix, gp_ref[...])


def gla_post(o_f, o_b, proj, g_head, w_out, h, g_post, tm=512):
    m, d = h.shape
    og_blk = (2 * GLA_DK + GLA_DV) // GLA_DV
    row = lambda i: (i, 0)
    return pl.pallas_call(
        _gla_post_kernel,
        grid=(m // tm,),
        in_specs=[
            pl.BlockSpec((tm, GLA_DV), row),
            pl.BlockSpec((tm, GLA_DV), row),
            pl.BlockSpec((tm, GLA_DV), lambda i: (i, og_blk)),
            _resident((1, GLA_HEAD_V)),
            _resident((GLA_DV, d)),
            pl.BlockSpec((tm, d), row),
            _resident((1, d)),
        ],
        out_specs=pl.BlockSpec((tm, d), row),
        out_shape=jax.ShapeDtypeStruct((m, d), F32),
        scratch_shapes=[pltpu.VMEM((tm, GLA_DV), BF16)],
        compiler_params=_params(("parallel",), VMEM_RESIDENT_MIB),
        name="gla_post",
    )(o_f, o_b, proj, g_head.reshape(1, GLA_HEAD_V), w_out, h, g_post.reshape(1, d))


def _matmul_norm_res_kernel(a_ref, w_ref, h_ref, g_ref, o_ref, w_bf_ref):
    @pl.when(pl.program_id(0) == 0)
    def _():
        w_bf_ref[...] = w_ref[...].astype(BF16)

    o_ref[...] = h_ref[...] + _rms(_dot(a_ref[...], w_bf_ref[...]), g_ref[...])


def matmul_norm_res(a, w, h, g, tm=512):
    m, d = h.shape
    k = a.shape[1]
    row = lambda i: (i, 0)
    return pl.pallas_call(
        _matmul_norm_res_kernel,
        grid=(m // tm,),
        in_specs=[
            pl.BlockSpec((tm, k), row),
            _resident((k, d)),
            pl.BlockSpec((tm, d), row),
            _resident((1, d)),
        ],
        out_specs=pl.BlockSpec((tm, d), row),
        out_shape=jax.ShapeDtypeStruct((m, d), F32),
        scratch_shapes=[pltpu.VMEM((k, d), BF16)],
        compiler_params=_params(("arbitrary",), VMEM_RESIDENT_MIB),
        name="matmul_norm_res",
    )(a, w, h, g.reshape(1, d))


def _mlp_kernel(h_ref, g1_ref, wu_ref, wd_ref, g2_ref, o_ref, u_ref):
    f = pl.program_id(1)
    last = pl.num_programs(1) - 1

    def partial_product():
        hid = jnp.maximum(_dot(u_ref[...], wu_ref[...].astype(BF16)), 0.0)
        return _dot((hid * hid).astype(BF16), wd_ref[...].astype(BF16))

    @pl.when(f == 0)
    def _():
        u_ref[...] = _rms(h_ref[...], g1_ref[...]).astype(BF16)
        o_ref[...] = partial_product()

    @pl.when(jnp.logical_and(f > 0, f < last))
    def _():
        o_ref[...] += partial_product()

    @pl.when(f == last)
    def _():
        o_ref[...] = h_ref[...] + _rms(o_ref[...] + partial_product(), g2_ref[...])


def mlp_block(h, g_pre, w_up, w_down, g_post, layer, tm=1024, tf=512):
    m, d = h.shape
    ff = w_up.shape[2]
    return pl.pallas_call(
        _mlp_kernel,
        grid=(m // tm, ff // tf),
        in_specs=[
            pl.BlockSpec((tm, d), lambda i, f: (i, 0)),
            _resident((1, d)),
            pl.BlockSpec((None, d, tf), lambda i, f: (layer, 0, f)),
            pl.BlockSpec((None, tf, d), lambda i, f: (layer, f, 0)),
            _resident((1, d)),
        ],
        out_specs=pl.BlockSpec((tm, d), lambda i, f: (i, 0)),
        out_shape=jax.ShapeDtypeStruct((m, d), F32),
        scratch_shapes=[pltpu.VMEM((tm, d), BF16)],
        compiler_params=_params(("parallel", "arbitrary"), VMEM_NEAR_FULL_MIB),
        name="mlp_block",
    )(h, g_pre.reshape(1, d), w_up, w_down, g_post.reshape(1, d))


def _ple_kernel(h_ref, p_ref, wg_ref, wp_ref, g_ref, o_ref, wg_bf_ref, wp_bf_ref):
    @pl.when(pl.program_id(0) == 0)
    def _():
        wg_bf_ref[...] = wg_ref[...].astype(BF16)
        wp_bf_ref[...] = wp_ref[...].astype(BF16)

    x = h_ref[...]
    gate = 1.0 / (1.0 + jnp.exp(-_dot(x.astype(BF16), wg_bf_ref[...])))
    e = _dot(p_ref[...].astype(BF16), wp_bf_ref[...])
    o_ref[...] = x + _rms(gate * e, g_ref[...])


def ple_block(h, p, w_gate, w_proj, g, layer, tm=512):
    m, d = h.shape
    pd = p.shape[-1]
    row = lambda i: (i, 0)
    once = pl.Buffered(1)
    return pl.pallas_call(
        _ple_kernel,
        grid=(m // tm,),
        in_specs=[
            pl.BlockSpec((tm, d), row),
            pl.BlockSpec((None, None, tm, pd), lambda i: (layer, 0, i, 0)),
            pl.BlockSpec((None, d, d), lambda i: (layer, 0, 0), pipeline_mode=once),
            pl.BlockSpec((None, pd, d), lambda i: (layer, 0, 0), pipeline_mode=once),
            _resident((1, d)),
        ],
        out_specs=pl.BlockSpec((tm, d), row),
        out_shape=jax.ShapeDtypeStruct((m, d), F32),
        scratch_shapes=[pltpu.VMEM((d, d), BF16), pltpu.VMEM((pd, d), BF16)],
        compiler_params=_params(("arbitrary",), VMEM_RESIDENT_MIB),
        name="ple_block",
    )(h, p, w_gate, w_proj, g.reshape(1, d))


def _rope_tables(seq):
    rows = seq // GRID_W
    t_row = np.repeat(np.arange(rows), GRID_W).astype(np.float64)
    t_col = np.tile(np.arange(GRID_W), rows).astype(np.float64)
    inv_freq = 1.0 / (ROPE_THETA ** (np.arange(0, ROPE_AXIS_DIM, 2, dtype=np.float64) / ROPE_AXIS_DIM))
    a_row = t_row[:, None] * inv_freq
    a_col = t_col[:, None] * inv_freq
    ang = np.concatenate([a_row, a_row, a_col, a_col], axis=1)
    half = ROPE_AXIS_DIM // 2
    sign = np.where((np.arange(ATTN_HEAD_DIM) % ROPE_AXIS_DIM) < half, -1.0, 1.0)
    return np.cos(ang).astype(np.float32), (np.sin(ang) * sign).astype(np.float32)


def _attn_proj_kernel(h_ref, g_ref, w_ref, cos_ref, sin_ref, cost_ref, sint_ref, gq_ref, gk_ref,
                      qt_ref, k_ref, vt_ref, u_ref):
    hd = ATTN_HEAD_DIM
    half = ROPE_AXIS_DIM // 2
    tm = h_ref.shape[0]
    per_dot = ATTN_KV_HEADS
    u_ref[...] = _rms(h_ref[...], g_ref[...]).astype(BF16)

    def project(first_head):
        return _dot(u_ref[...], w_ref[:, first_head * hd:(first_head + per_dot) * hd])

    cos_t = cost_ref[...]
    sin_t = sint_ref[...]
    gq = jnp.tile(gq_ref[...], (1, tm // LANES))
    for c in range(ATTN_Q_HEADS // per_dot):
        x = project(c * per_dot)
        for i in range(per_dot):
            xt = x[:, i * hd:(i + 1) * hd].T
            y = xt * lax.rsqrt(jnp.mean(xt * xt, axis=0, keepdims=True) + NORM_EPS) * gq
            partner = jnp.concatenate([y[half:2 * half], y[:half], y[3 * half:], y[2 * half:3 * half]], axis=0)
            head = c * per_dot + i
            qt_ref[head * hd:(head + 1) * hd, :] = (y * cos_t + partner * sin_t).astype(BF16)

    cos = cos_ref[...]
    sin = sin_ref[...]
    lane = lax.broadcasted_iota(jnp.int32, cos.shape, 1)
    first_half = (lane % ROPE_AXIS_DIM) < half
    xk = project(ATTN_Q_HEADS)
    xv = project(ATTN_Q_HEADS + ATTN_KV_HEADS)
    for i in range(ATTN_KV_HEADS):
        sl = slice(i * hd, (i + 1) * hd)
        y = _rms(xk[:, sl], gk_ref[...])
        partner = jnp.where(first_half, pltpu.roll(y, hd - half, 1), pltpu.roll(y, half, 1))
        k_ref[:, sl] = (y * cos + partner * sin).astype(BF16)
        vt_ref[i, :hd, :] = xv[:, sl].T.astype(BF16)
        vt_ref[i, hd:, :] = jnp.ones((ATTN_V_ROWS - hd, tm), BF16)


def attn_proj(h, g_pre, w, g_q, g_k, tm=512):
    seq, d = h.shape
    hd = ATTN_HEAD_DIM
    cos, sin = _rope_tables(seq)
    q_scale = (hd ** -0.5) * LOG2E
    cos_t, sin_t = jnp.asarray(cos.T * q_scale), jnp.asarray(sin.T * q_scale)
    cos, sin = jnp.asarray(cos), jnp.asarray(sin)
    gq_rows = jnp.broadcast_to(g_q.astype(F32)[:, None], (hd, LANES))
    row = lambda i: (i, 0)
    col = lambda i: (0, i)
    return pl.pallas_call(
        _attn_proj_kernel,
        grid=(seq // tm,),
        in_specs=[
            pl.BlockSpec((tm, d), row),
            _resident((1, d)),
            _resident(w.shape),
            pl.BlockSpec((tm, hd), row),
            pl.BlockSpec((tm, hd), row),
            pl.BlockSpec((hd, tm), col),
            pl.BlockSpec((hd, tm), col),
            _resident((hd, LANES)),
            _resident((1, hd)),
        ],
        out_specs=[
            pl.BlockSpec((ATTN_Q_HEADS * hd, tm), col),
            pl.BlockSpec((tm, ATTN_KV_HEADS * hd), row),
            pl.BlockSpec((ATTN_KV_HEADS, ATTN_V_ROWS, tm), lambda i: (0, 0, i)),
        ],
        out_shape=[
            jax.ShapeDtypeStruct((ATTN_Q_HEADS * hd, seq), BF16),
            jax.ShapeDtypeStruct((seq, ATTN_KV_HEADS * hd), BF16),
            jax.ShapeDtypeStruct((ATTN_KV_HEADS, ATTN_V_ROWS, seq), BF16),
        ],
        scratch_shapes=[pltpu.VMEM((tm, d), BF16)],
        compiler_params=_params(("parallel",), VMEM_STREAM_MIB),
        name="attn_proj",
    )(h, g_pre.reshape(1, d), w, cos, sin, cos_t, sin_t, gq_rows, g_k.reshape(1, hd))


def _flash_kernel(qt_ref, k_ref, vt_ref, o_ref, st_ref, m_ref, acc_ref, *, tq):
    hd = ATTN_HEAD_DIM
    tk = ATTN_KV_TILE
    nkb = k_ref.shape[0] // tk
    nqb = qt_ref.shape[1] // tq
    assert nkb % ATTN_SCORE_BUFFERS == 0 and nqb % 2 == 0

    def queries(qb):
        start = pl.multiple_of(qb * tq, tq)
        return jnp.concatenate([qt_ref[i * hd:(i + 1) * hd, pl.ds(start, tq)] for i in range(ATTN_GROUP)], axis=1)

    def scores(j, slot, qt):
        start = pl.multiple_of(j * tk, tk)
        st_ref[slot] = _dot(k_ref[pl.ds(start, tk), :], qt)

    def consume(j, slot):
        start = pl.multiple_of(j * tk, tk)
        st = st_ref[slot]
        m_prev = m_ref[...]
        m_next = jnp.maximum(m_prev, jnp.max(st, axis=0, keepdims=True))
        alpha = jnp.exp2(m_prev - m_next)
        pt = jnp.exp2(st - m_next)
        acc_ref[...] = alpha * acc_ref[...] + _dot(vt_ref[:, pl.ds(start, tk)], pt.astype(BF16))
        m_ref[...] = m_next

    first = queries(0)
    for j in range(ATTN_AHEAD):
        scores(j, j, first)

    def query_block(qb, carry):
        qt = queries(qb)
        qt_next = queries(jnp.minimum(qb + 1, nqb - 1))
        m_ref[...] = jnp.full_like(m_ref, -jnp.inf)
        acc_ref[...] = jnp.zeros_like(acc_ref)
        for j in range(nkb):
            ahead = j + ATTN_AHEAD
            if ahead < nkb:
                scores(ahead, ahead % ATTN_SCORE_BUFFERS, qt)
            else:
                scores(ahead - nkb, (ahead - nkb) % ATTN_SCORE_BUFFERS, qt_next)
            consume(j, j % ATTN_SCORE_BUFFERS)
        ot = acc_ref[:hd, :] / acc_ref[hd:hd + 1, :]
        rows = pl.ds(pl.multiple_of(qb * tq, tq), tq)
        for i in range(ATTN_GROUP):
            o_ref[rows, i * hd:(i + 1) * hd] = ot[:, i * tq:(i + 1) * tq].T.astype(o_ref.dtype)
        return carry

    def two_query_blocks(i, carry):
        query_block(2 * i, carry)
        return query_block(2 * i + 1, carry)

    lax.fori_loop(0, nqb // 2, two_query_blocks, 0)


def flash_attention(qt, k, vt, tq=256):
    seq = k.shape[0]
    hd = ATTN_HEAD_DIM
    gw = ATTN_GROUP * hd
    once = pl.Buffered(1)
    return pl.pallas_call(
        functools.partial(_flash_kernel, tq=tq),
        grid=(ATTN_KV_HEADS,),
        in_specs=[
            pl.BlockSpec((gw, seq), lambda g: (g, 0), pipeline_mode=once),
            pl.BlockSpec((seq, hd), lambda g: (0, g), pipeline_mode=once),
            pl.BlockSpec((None, ATTN_V_ROWS, seq), lambda g: (g, 0, 0), pipeline_mode=once),
        ],
        out_specs=pl.BlockSpec((seq, gw), lambda g: (0, g)),
        out_shape=jax.ShapeDtypeStruct((seq, ATTN_Q_HEADS * hd), BF16),
        scratch_shapes=[
            pltpu.VMEM((ATTN_SCORE_BUFFERS, ATTN_KV_TILE, ATTN_GROUP * tq), F32),
            pltpu.VMEM((1, ATTN_GROUP * tq), F32),
            pltpu.VMEM((ATTN_V_ROWS, ATTN_GROUP * tq), F32),
        ],
        compiler_params=_params(("parallel",), VMEM_RESIDENT_MIB),
        name="flash_attention",
    )(qt, k, vt)


def _chunk_tri(reverse):
    idx = np.arange(GLA_BLOCK)
    same = (idx[:, None] // GLA_CHUNK) == (idx[None, :] // GLA_CHUNK)
    tri = (idx[None, :] >= idx[:, None]) if reverse else (idx[None, :] <= idx[:, None])
    return jnp.asarray((same & tri).astype(np.float32), dtype=BF16)


def _gate_weights(w_gk, first_row):
    w = jnp.zeros((2 * GLA_GATE_RANK, GLA_DK), F32).at[first_row:first_row + GLA_GATE_RANK].set(w_gk)
    return w.astype(BF16)


def gla_layer(h, g_pre, w_in, w_gk_f, b_gk_f, w_gk_b, b_gk_b, g_head, w_out, g_post):
    proj, lr = norm_matmul(h, g_pre, w_in.astype(BF16), GLA_MAIN, tm=256, tn=1024)
    o_f, o_b = gla_scan(proj, lr, w_gk_f, b_gk_f, w_gk_b, b_gk_b)
    return gla_post(o_f, o_b, proj, g_head, w_out.astype(BF16), h, g_post)


def gqa_layer(h, g_pre, w_in, g_q, g_k, w_out, g_post):
    qt, k, vt = attn_proj(h, g_pre, w_in.astype(BF16), g_q, g_k)
    o = flash_attention(qt, k, vt)
    seq, d = o.shape
    nblk = seq // ATTN_Q_BLOCK
    o = o.reshape(nblk, ATTN_Q_BLOCK, d).transpose(1, 0, 2).reshape(seq, d)
    return matmul_norm_res(o, w_out, h, g_post)


def kernel(x, p, g_pre_mix, g_post_mix, g_pre_mlp, g_post_mlp, g_ple, w_mlp_up, w_mlp_down,
           w_ple_proj, w_ple_gate, gla_w_in, gla_w_gk_fwd, gla_b_gk_fwd, gla_w_gk_bwd,
           gla_b_gk_bwd, gla_g_head, gla_w_out, attn_w_in, attn_g_q, attn_g_k, attn_w_out):
    assert x.shape == (1, SEQ, D_MODEL)
    h = x[0]
    for i in range(DEPTH):
        j = i // 2
        if i % 2 == 0:
            h = gla_layer(h, g_pre_mix[i], gla_w_in[j], gla_w_gk_fwd[j], gla_b_gk_fwd[j],
                          gla_w_gk_bwd[j], gla_b_gk_bwd[j], gla_g_head[j], gla_w_out[j], g_post_mix[i])
        else:
            h = gqa_layer(h, g_pre_mix[i], attn_w_in[j], attn_g_q[j], attn_g_k[j], attn_w_out[j],
                          g_post_mix[i])
        h = mlp_block(h, g_pre_mlp[i], w_mlp_up, w_mlp_down, g_post_mlp[i], i)
        h = ple_block(h, p, w_ple_gate, w_ple_proj, g_ple[i], i)
    return h[None]
```

```python
import functools

import numpy as np
import jax
import jax.numpy as jnp
from jax import lax
from jax.experimental import pallas as pl
from jax.experimental.pallas import tpu as pltpu

F32 = jnp.float32
BF16 = jnp.bfloat16

D_MODEL = 2048
SEQ = 8192
DEPTH = 2
GRID_W = 64
NORM_EPS = 1e-6

GLA_HEADS = 4
GLA_DK = D_MODEL // 2
GLA_DV = D_MODEL
GLA_HEAD_K = GLA_DK // GLA_HEADS
GLA_HEAD_V = GLA_DV // GLA_HEADS
GLA_GATE_RANK = 16
GLA_GATE_TAU = 16.0
GLA_CHUNK = 64
GLA_MAIN = 2 * GLA_DK + 2 * GLA_DV
GLA_BLOCK = 256

ATTN_HEAD_DIM = 128
ATTN_Q_HEADS = D_MODEL // ATTN_HEAD_DIM
ATTN_KV_HEADS = 4
ATTN_GROUP = ATTN_Q_HEADS // ATTN_KV_HEADS
ATTN_Q_BLOCK = 128
ATTN_KV_TILE = 1024
ATTN_V_ROWS = ATTN_HEAD_DIM + 16
ATTN_AHEAD = 1
ATTN_SCORE_BUFFERS = 2 * ATTN_AHEAD
ROPE_THETA = 10000.0
ROPE_AXIS_DIM = ATTN_HEAD_DIM // 2

LANES = 128
MIB = 1024 * 1024
VMEM_STREAM_MIB = 48
VMEM_RESIDENT_MIB = 56
VMEM_NEAR_FULL_MIB = 62
LOG2E = 1.4426950408889634


def _params(semantics, vmem_mib):
    return pltpu.CompilerParams(dimension_semantics=semantics, vmem_limit_bytes=vmem_mib * MIB)


def _rms(x, gain):
    return x * lax.rsqrt(jnp.mean(x * x, axis=-1, keepdims=True) + NORM_EPS) * gain


def _dot(a, b):
    return jnp.dot(a, b, preferred_element_type=F32)


def _dot_nt(a, b):
    return lax.dot_general(a, b, (((1,), (1,)), ((), ())), preferred_element_type=F32)


def _dot_tn(a, b):
    return lax.dot_general(a, b, (((0,), (0,)), ((), ())), preferred_element_type=F32)


def _resident(shape):
    return pl.BlockSpec(shape, lambda *_: (0,) * len(shape), pipeline_mode=pl.Buffered(1))


def _norm_matmul_kernel(h_ref, g_ref, w_ref, o_ref, ox_ref, u_ref, *, tn):
    n = o_ref.shape[1]
    u_ref[...] = _rms(h_ref[...], g_ref[...]).astype(BF16)
    ox_ref[...] = _dot(u_ref[...], w_ref[:, n:]).astype(ox_ref.dtype)
    for c in range(n // tn):
        sl = slice(c * tn, (c + 1) * tn)
        o_ref[:, sl] = _dot(u_ref[...], w_ref[:, sl]).astype(o_ref.dtype)


def norm_matmul(h, g, w, n, tm, tn):
    m, k = h.shape
    nx = w.shape[1] - n
    return pl.pallas_call(
        functools.partial(_norm_matmul_kernel, tn=tn),
        grid=(m // tm,),
        in_specs=[
            pl.BlockSpec((tm, k), lambda i: (i, 0)),
            _resident((1, k)),
            _resident(w.shape),
        ],
        out_specs=[
            pl.BlockSpec((tm, n), lambda i: (i, 0)),
            pl.BlockSpec((tm, nx), lambda i: (i, 0)),
        ],
        out_shape=[jax.ShapeDtypeStruct((m, n), F32), jax.ShapeDtypeStruct((m, nx), BF16)],
        scratch_shapes=[pltpu.VMEM((tm, k), BF16)],
        compiler_params=_params(("parallel",), VMEM_RESIDENT_MIB),
        name="norm_matmul",
    )(h, g.reshape(1, k), w)


def _gla_kernel(qf_ref, kf_ref, vf_ref, lrf_ref, qb_ref, kb_ref, vb_ref, lrb_ref, wf_ref, bf_ref, wb_ref, bb_ref,
                trif_ref, trib_ref, of_ref, ob_ref, s_ref, cum_ref):
    c = GLA_CHUNK
    nchunk = GLA_BLOCK // c

    @pl.when(pl.program_id(0) == 0)
    def _():
        s_ref[...] = jnp.zeros_like(s_ref)

    def cumulative(d, lr_ref, w_ref, b_ref, tri_ref):
        z = _dot(lr_ref[...], w_ref[...]) + b_ref[...]
        la = (jnp.minimum(z, 0.0) - jnp.log(1.0 + jnp.exp(-jnp.abs(z)))) * (LOG2E / GLA_GATE_TAU)
        la_hi = la.astype(BF16)
        la_lo = (la - la_hi.astype(F32)).astype(BF16)
        tri = tri_ref[...]
        cum_ref[d] = _dot(tri, la_hi) + _dot(tri, la_lo)

    cumulative(0, lrf_ref, wf_ref, bf_ref, trif_ref)
    cumulative(1, lrb_ref, wb_ref, bb_ref, trib_ref)

    row = lax.broadcasted_iota(jnp.int32, (c, c), 0)
    col = lax.broadcasted_iota(jnp.int32, (c, c), 1)
    directions = (
        (qf_ref, kf_ref, vf_ref, of_ref, col <= row, c // 2 - 1, c - 1),
        (qb_ref, kb_ref, vb_ref, ob_ref, col > row, c // 2, 0),
    )
    for step in range(nchunk):
        for d, (q_ref, k_ref, v_ref, o_ref, mask, ref_row, last_row) in enumerate(directions):
            j = nchunk - 1 - step if d else step
            sl = slice(j * c, (j + 1) * c)
            for h in range(GLA_HEADS):
                hk = slice(h * GLA_HEAD_K, (h + 1) * GLA_HEAD_K)
                hv = slice(h * GLA_HEAD_V, (h + 1) * GLA_HEAD_V)
                cumc = cum_ref[d, sl, hk]
                refc = cumc[ref_row:ref_row + 1]
                lastc = cumc[last_row:last_row + 1]
                qc = q_ref[sl, hk] * (GLA_HEAD_K ** -0.5)
                kc = k_ref[sl, hk]
                vc = v_ref[sl, hv].astype(BF16)
                qt = (qc * jnp.exp2(cumc - refc)).astype(BF16)
                kt = (kc * jnp.exp2(refc - cumc)).astype(BF16)
                qd = (qc * jnp.exp2(cumc)).astype(BF16)
                ke = (kc * jnp.exp2(lastc - cumc)).astype(BF16)
                scores = jnp.where(mask, _dot_nt(qt, kt), 0.0).astype(BF16)
                state = s_ref[d * GLA_HEADS + h]
                o_ref[sl, hv] = _dot(scores, vc) + _dot_nt(qd, state.astype(BF16))
                s_ref[d * GLA_HEADS + h] = state * jnp.exp2(lastc) + _dot_tn(vc, ke)


def gla_scan(proj, lr, w_f, b_f, w_b, b_b):
    seq = proj.shape[0]
    t = GLA_BLOCK
    nb = seq // t
    fwd = lambda i: i
    bwd = lambda i: nb - 1 - i
    const = lambda i: (0, 0)
    v_blk = 2 * GLA_DK // GLA_DV

    def operands(blk):
        return [
            pl.BlockSpec((t, GLA_DK), lambda i: (blk(i), 0)),
            pl.BlockSpec((t, GLA_DK), lambda i: (blk(i), 1)),
            pl.BlockSpec((t, GLA_DV), lambda i: (blk(i), v_blk)),
            pl.BlockSpec((t, 2 * GLA_GATE_RANK), lambda i: (blk(i), 0)),
        ]

    return pl.pallas_call(
        _gla_kernel,
        grid=(nb,),
        in_specs=operands(fwd) + operands(bwd) + [
            pl.BlockSpec((2 * GLA_GATE_RANK, GLA_DK), const),
            pl.BlockSpec((1, GLA_DK), const),
            pl.BlockSpec((2 * GLA_GATE_RANK, GLA_DK), const),
            pl.BlockSpec((1, GLA_DK), const),
            pl.BlockSpec((t, t), const),
            pl.BlockSpec((t, t), const),
        ],
        out_specs=[
            pl.BlockSpec((t, GLA_DV), lambda i: (fwd(i), 0)),
            pl.BlockSpec((t, GLA_DV), lambda i: (bwd(i), 0)),
        ],
        out_shape=[jax.ShapeDtypeStruct((seq, GLA_DV), F32)] * 2,
        scratch_shapes=[
            pltpu.VMEM((2 * GLA_HEADS, GLA_HEAD_V, GLA_HEAD_K), F32),
            pltpu.VMEM((2, t, GLA_DK), F32),
        ],
        compiler_params=_params(("arbitrary",), VMEM_STREAM_MIB),
        name="gla_scan",
    )(proj, proj, proj, lr, proj, proj, proj, lr,
      _gate_weights(w_f, 0), b_f.reshape(1, GLA_DK), _gate_weights(w_b, GLA_GATE_RANK), b_b.reshape(1, GLA_DK),
      _chunk_tri(False), _chunk_tri(True))


def _gla_post_kernel(of_ref, ob_ref, og_ref, gh_ref, w_ref, h_ref, gp_ref, o_ref, y_ref):
    for hd in range(GLA_HEADS):
        sl = slice(hd * GLA_HEAD_V, (hd + 1) * GLA_HEAD_V)
        o = _rms(of_ref[:, sl] + ob_ref[:, sl], gh_ref[...])
        og = og_ref[:, sl]
        y_ref[:, sl] = (o * (og / (1.0 + jnp.exp(-og)))).astype(BF16)
    mix = _dot(y_ref[...], w_ref[...])
    o_ref[...] = h_ref[...] + _rms(mix, gp_ref[...])


def gla_post(o_f, o_b, proj, g_head, w_out, h, g_post, tm=512):
    m, d = h.shape
    og_blk = (2 * GLA_DK + GLA_DV) // GLA_DV
    row = lambda i: (i, 0)
    return pl.pallas_call(
        _gla_post_kernel,
        grid=(m // tm,),
        in_specs=[
            pl.BlockSpec((tm, GLA_DV), row),
            pl.BlockSpec((tm, GLA_DV), row),
            pl.BlockSpec((tm, GLA_DV), lambda i: (i, og_blk)),
            _resident((1, GLA_HEAD_V)),
            _resident((GLA_DV, d)),
            pl.BlockSpec((tm, d), row),
            _resident((1, d)),
        ],
        out_specs=pl.BlockSpec((tm, d), row),
        out_shape=jax.ShapeDtypeStruct((m, d), F32),
        scratch_shapes=[pltpu.VMEM((tm, GLA_DV), BF16)],
        compiler_params=_params(("parallel",), VMEM_RESIDENT_MIB),
        name="gla_post",
    )(o_f, o_b, proj, g_head.reshape(1, GLA_HEAD_V), w_out, h, g_post.reshape(1, d))


def _matmul_norm_res_kernel(a_ref, w_ref, h_ref, g_ref, o_ref, w_bf_ref):
    @pl.when(pl.program_id(0) == 0)
    def _():
        w_bf_ref[...] = w_ref[...].astype(BF16)

    o_ref[...] = h_ref[...] + _rms(_dot(a_ref[...], w_bf_ref[...]), g_ref[...])


def matmul_norm_res(a, w, h, g, tm=512):
    m, d = h.shape
    k = a.shape[1]
    row = lambda i: (i, 0)
    return pl.pallas_call(
        _matmul_norm_res_kernel,
        grid=(m // tm,),
        in_specs=[
            pl.BlockSpec((tm, k), row),
            _resident((k, d)),
            pl.BlockSpec((tm, d), row),
            _resident((1, d)),
        ],
        out_specs=pl.BlockSpec((tm, d), row),
        out_shape=jax.ShapeDtypeStruct((m, d), F32),
        scratch_shapes=[pltpu.VMEM((k, d), BF16)],
        compiler_params=_params(("arbitrary",), VMEM_RESIDENT_MIB),
        name="matmul_norm_res",
    )(a, w, h, g.reshape(1, d))


def _mlp_kernel(h_ref, g1_ref, wu_ref, wd_ref, g2_ref, o_ref, u_ref):
    f = pl.program_id(1)
    last = pl.num_programs(1) - 1

    def partial_product():
        hid = jnp.maximum(_dot(u_ref[...], wu_ref[...].astype(BF16)), 0.0)
        return _dot((hid * hid).astype(BF16), wd_ref[...].astype(BF16))

    @pl.when(f == 0)
    def _():
        u_ref[...] = _rms(h_ref[...], g1_ref[...]).astype(BF16)
        o_ref[...] = partial_product()

    @pl.when(jnp.logical_and(f > 0, f < last))
    def _():
        o_ref[...] += partial_product()

    @pl.when(f == last)
    def _():
        o_ref[...] = h_ref[...] + _rms(o_ref[...] + partial_product(), g2_ref[...])


def mlp_block(h, g_pre, w_up, w_down, g_post, layer, tm=1024, tf=512):
    m, d = h.shape
    ff = w_up.shape[2]
    return pl.pallas_call(
        _mlp_kernel,
        grid=(m // tm, ff // tf),
        in_specs=[
            pl.BlockSpec((tm, d), lambda i, f: (i, 0)),
            _resident((1, d)),
            pl.BlockSpec((None, d, tf), lambda i, f: (layer, 0, f)),
            pl.BlockSpec((None, tf, d), lambda i, f: (layer, f, 0)),
            _resident((1, d)),
        ],
        out_specs=pl.BlockSpec((tm, d), lambda i, f: (i, 0)),
        out_shape=jax.ShapeDtypeStruct((m, d), F32),
        scratch_shapes=[pltpu.VMEM((tm, d), BF16)],
        compiler_params=_params(("parallel", "arbitrary"), VMEM_NEAR_FULL_MIB),
        name="mlp_block",
    )(h, g_pre.reshape(1, d), w_up, w_down, g_post.reshape(1, d))


def _ple_kernel(h_ref, p_ref, wg_ref, wp_ref, g_ref, o_ref, wg_bf_ref, wp_bf_ref):
    @pl.when(pl.program_id(0) == 0)
    def _():
        wg_bf_ref[...] = wg_ref[...].astype(BF16)
        wp_bf_ref[...] = wp_ref[...].astype(BF16)

    x = h_ref[...]
    gate = 1.0 / (1.0 + jnp.exp(-_dot(x.astype(BF16), wg_bf_ref[...])))
    e = _dot(p_ref[...].astype(BF16), wp_bf_ref[...])
    o_ref[...] = x + _rms(gate * e, g_ref[...])


def ple_block(h, p, w_gate, w_proj, g, layer, tm=512):
    m, d = h.shape
    pd = p.shape[-1]
    row = lambda i: (i, 0)
    once = pl.Buffered(1)
    return pl.pallas_call(
        _ple_kernel,
        grid=(m // tm,),
        in_specs=[
            pl.BlockSpec((tm, d), row),
            pl.BlockSpec((None, None, tm, pd), lambda i: (layer, 0, i, 0)),
            pl.BlockSpec((None, d, d), lambda i: (layer, 0, 0), pipeline_mode=once),
            pl.BlockSpec((None, pd, d), lambda i: (layer, 0, 0), pipeline_mode=once),
            _resident((1, d)),
        ],
        out_specs=pl.BlockSpec((tm, d), row),
        out_shape=jax.ShapeDtypeStruct((m, d), F32),
        scratch_shapes=[pltpu.VMEM((d, d), BF16), pltpu.VMEM((pd, d), BF16)],
        compiler_params=_params(("arbitrary",), VMEM_RESIDENT_MIB),
        name="ple_block",
    )(h, p, w_gate, w_proj, g.reshape(1, d))


def _rope_tables(seq):
    rows = seq // GRID_W
    t_row = np.repeat(np.arange(rows), GRID_W).astype(np.float64)
    t_col = np.tile(np.arange(GRID_W), rows).astype(np.float64)
    inv_freq = 1.0 / (ROPE_THETA ** (np.arange(0, ROPE_AXIS_DIM, 2, dtype=np.float64) / ROPE_AXIS_DIM))
    a_row = t_row[:, None] * inv_freq
    a_col = t_col[:, None] * inv_freq
    ang = np.concatenate([a_row, a_row, a_col, a_col], axis=1)
    half = ROPE_AXIS_DIM // 2
    sign = np.where((np.arange(ATTN_HEAD_DIM) % ROPE_AXIS_DIM) < half, -1.0, 1.0)
    return np.cos(ang).astype(np.float32), (np.sin(ang) * sign).astype(np.float32)


def _attn_proj_kernel(h_ref, g_ref, w_ref, cos_ref, sin_ref, cost_ref, sint_ref, gq_ref, gk_ref,
                      qt_ref, k_ref, vt_ref, u_ref):
    hd = ATTN_HEAD_DIM
    half = ROPE_AXIS_DIM // 2
    tm = h_ref.shape[0]
    per_dot = ATTN_KV_HEADS
    u_ref[...] = _rms(h_ref[...], g_ref[...]).astype(BF16)

    def project(first_head):
        return _dot(u_ref[...], w_ref[:, first_head * hd:(first_head + per_dot) * hd])

    cos_t = cost_ref[...]
    sin_t = sint_ref[...]
    gq = jnp.tile(gq_ref[...], (1, tm // LANES))
    for c in range(ATTN_Q_HEADS // per_dot):
        x = project(c * per_dot)
        for i in range(per_dot):
            xt = x[:, i * hd:(i + 1) * hd].T
            y = xt * lax.rsqrt(jnp.mean(xt * xt, axis=0, keepdims=True) + NORM_EPS) * gq
            partner = jnp.concatenate([y[half:2 * half], y[:half], y[3 * half:], y[2 * half:3 * half]], axis=0)
            head = c * per_dot + i
            qt_ref[head * hd:(head + 1) * hd, :] = (y * cos_t + partner * sin_t).astype(BF16)

    cos = cos_ref[...]
    sin = sin_ref[...]
    lane = lax.broadcasted_iota(jnp.int32, cos.shape, 1)
    first_half = (lane % ROPE_AXIS_DIM) < half
    xk = project(ATTN_Q_HEADS)
    xv = project(ATTN_Q_HEADS + ATTN_KV_HEADS)
    for i in range(ATTN_KV_HEADS):
        sl = slice(i * hd, (i + 1) * hd)
        y = _rms(xk[:, sl], gk_ref[...])
        partner = jnp.where(first_half, pltpu.roll(y, hd - half, 1), pltpu.roll(y, half, 1))
        k_ref[:, sl] = (y * cos + partner * sin).astype(BF16)
        vt_ref[i, :hd, :] = xv[:, sl].T.astype(BF16)
        vt_ref[i, hd:, :] = jnp.ones((ATTN_V_ROWS - hd, tm), BF16)


def attn_proj(h, g_pre, w, g_q, g_k, tm=512):
    seq, d = h.shape
    hd = ATTN_HEAD_DIM
    cos, sin = _rope_tables(seq)
    q_scale = (hd ** -0.5) * LOG2E
    cos_t, sin_t = jnp.asarray(cos.T * q_scale), jnp.asarray(sin.T * q_scale)
    cos, sin = jnp.asarray(cos), jnp.asarray(sin)
    gq_rows = jnp.broadcast_to(g_q.astype(F32)[:, None], (hd, LANES))
    row = lambda i: (i, 0)
    col = lambda i: (0, i)
    return pl.pallas_call(
        _attn_proj_kernel,
        grid=(seq // tm,),
        in_specs=[
            pl.BlockSpec((tm, d), row),
            _resident((1, d)),
            _resident(w.shape),
            pl.BlockSpec((tm, hd), row),
            pl.BlockSpec((tm, hd), row),
            pl.BlockSpec((hd, tm), col),
            pl.BlockSpec((hd, tm), col),
            _resident((hd, LANES)),
            _resident((1, hd)),
        ],
        out_specs=[
            pl.BlockSpec((ATTN_Q_HEADS * hd, tm), col),
            pl.BlockSpec((tm, ATTN_KV_HEADS * hd), row),
            pl.BlockSpec((ATTN_KV_HEADS, ATTN_V_ROWS, tm), lambda i: (0, 0, i)),
        ],
        out_shape=[
            jax.ShapeDtypeStruct((ATTN_Q_HEADS * hd, seq), BF16),
            jax.ShapeDtypeStruct((seq, ATTN_KV_HEADS * hd), BF16),
            jax.ShapeDtypeStruct((ATTN_KV_HEADS, ATTN_V_ROWS, seq), BF16),
        ],
        scratch_shapes=[pltpu.VMEM((tm, d), BF16)],
        compiler_params=_params(("parallel",), VMEM_STREAM_MIB),
        name="attn_proj",
    )(h, g_pre.reshape(1, d), w, cos, sin, cos_t, sin_t, gq_rows, g_k.reshape(1, hd))


def _flash_kernel(qt_ref, k_ref, vt_ref, o_ref, st_ref, m_ref, acc_ref, *, tq):
    hd = ATTN_HEAD_DIM
    tk = ATTN_KV_TILE
    nkb = k_ref.shape[0] // tk
    nqb = qt_ref.shape[1] // tq
    assert nkb % ATTN_SCORE_BUFFERS == 0 and nqb % 2 == 0

    def queries(qb):
        start = pl.multiple_of(qb * tq, tq)
        return jnp.concatenate([qt_ref[i * hd:(i + 1) * hd, pl.ds(start, tq)] for i in range(ATTN_GROUP)], axis=1)

    def scores(j, slot, qt):
        start = pl.multiple_of(j * tk, tk)
        st_ref[slot] = _dot(k_ref[pl.ds(start, tk), :], qt)

    def consume(j, slot):
        start = pl.multiple_of(j * tk, tk)
        st = st_ref[slot]
        m_prev = m_ref[...]
        m_next = jnp.maximum(m_prev, jnp.max(st, axis=0, keepdims=True))
        alpha = jnp.exp2(m_prev - m_next)
        pt = jnp.exp2(st - m_next)
        acc_ref[...] = alpha * acc_ref[...] + _dot(vt_ref[:, pl.ds(start, tk)], pt.astype(BF16))
        m_ref[...] = m_next

    first = queries(0)
    for j in range(ATTN_AHEAD):
        scores(j, j, first)

    def query_block(qb, carry):
        qt = queries(qb)
        qt_next = queries(jnp.minimum(qb + 1, nqb - 1))
        m_ref[...] = jnp.full_like(m_ref, -jnp.inf)
        acc_ref[...] = jnp.zeros_like(acc_ref)
        for j in range(nkb):
            ahead = j + ATTN_AHEAD
            if ahead < nkb:
                scores(ahead, ahead % ATTN_SCORE_BUFFERS, qt)
            else:
                scores(ahead - nkb, (ahead - nkb) % ATTN_SCORE_BUFFERS, qt_next)
            consume(j, j % ATTN_SCORE_BUFFERS)
        ot = acc_ref[:hd, :] / acc_ref[hd:hd + 1, :]
        rows = pl.ds(pl.multiple_of(qb * tq, tq), tq)
        for i in range(ATTN_GROUP):
            o_ref[rows, i * hd:(i + 1) * hd] = ot[:, i * tq:(i + 1) * tq].T.astype(o_ref.dtype)
        return carry

    def two_query_blocks(i, carry):
        query_block(2 * i, carry)
        return query_block(2 * i + 1, carry)

    lax.fori_loop(0, nqb // 2, two_query_blocks, 0)


def flash_attention(qt, k, vt, tq=256):
    seq = k.shape[0]
    hd = ATTN_HEAD_DIM
    gw = ATTN_GROUP * hd
    once = pl.Buffered(1)
    return pl.pallas_call(
        functools.partial(_flash_kernel, tq=tq),
        grid=(ATTN_KV_HEADS,),
        in_specs=[
            pl.BlockSpec((gw, seq), lambda g: (g, 0), pipeline_mode=once),
            pl.BlockSpec((seq, hd), lambda g: (0, g), pipeline_mode=once),
            pl.BlockSpec((None, ATTN_V_ROWS, seq), lambda g: (g, 0, 0), pipeline_mode=once),
        ],
        out_specs=pl.BlockSpec((seq, gw), lambda g: (0, g)),
        out_shape=jax.ShapeDtypeStruct((seq, ATTN_Q_HEADS * hd), BF16),
        scratch_shapes=[
            pltpu.VMEM((ATTN_SCORE_BUFFERS, ATTN_KV_TILE, ATTN_GROUP * tq), F32),
            pltpu.VMEM((1, ATTN_GROUP * tq), F32),
            pltpu.VMEM((ATTN_V_ROWS, ATTN_GROUP * tq), F32),
        ],
        compiler_params=_params(("parallel",), VMEM_RESIDENT_MIB),
        name="flash_attention",
    )(qt, k, vt)


def _chunk_tri(reverse):
    idx = np.arange(GLA_BLOCK)
    same = (idx[:, None] // GLA_CHUNK) == (idx[None, :] // GLA_CHUNK)
    tri = (idx[None, :] >= idx[:, None]) if reverse else (idx[None, :] <= idx[:, None])
    return jnp.asarray((same & tri).astype(np.float32), dtype=BF16)


def _gate_weights(w_gk, first_row):
    w = jnp.zeros((2 * GLA_GATE_RANK, GLA_DK), F32).at[first_row:first_row + GLA_GATE_RANK].set(w_gk)
    return w.astype(BF16)


def gla_layer(h, g_pre, w_in, w_gk_f, b_gk_f, w_gk_b, b_gk_b, g_head, w_out, g_post):
    proj, lr = norm_matmul(h, g_pre, w_in.astype(BF16), GLA_MAIN, tm=256, tn=1024)
    o_f, o_b = gla_scan(proj, lr, w_gk_f, b_gk_f, w_gk_b, b_gk_b)
    return gla_post(o_f, o_b, proj, g_head, w_out.astype(BF16), h, g_post)


def gqa_layer(h, g_pre, w_in, g_q, g_k, w_out, g_post):
    qt, k, vt = attn_proj(h, g_pre, w_in.astype(BF16), g_q, g_k)
    o = flash_attention(qt, k, vt)
    seq, d = o.shape
    nblk = seq // ATTN_Q_BLOCK
    o = o.reshape(nblk, ATTN_Q_BLOCK, d).transpose(1, 0, 2).reshape(seq, d)
    return matmul_norm_res(o, w_out, h, g_post)


def kernel(x, p, g_pre_mix, g_post_mix, g_pre_mlp, g_post_mlp, g_ple, w_mlp_up, w_mlp_down,
           w_ple_proj, w_ple_gate, gla_w_in, gla_w_gk_fwd, gla_b_gk_fwd, gla_w_gk_bwd,
           gla_b_gk_bwd, gla_g_head, gla_w_out, attn_w_in, attn_g_q, attn_g_k, attn_w_out):
    assert x.shape == (1, SEQ, D_MODEL)
    h = x[0]
    for i in range(DEPTH):
        j = i // 2
        if i % 2 == 0:
            h = gla_layer(h, g_pre_mix[i], gla_w_in[j], gla_w_gk_fwd[j], gla_b_gk_fwd[j],
                          gla_w_gk_bwd[j], gla_b_gk_bwd[j], gla_g_head[j], gla_w_out[j], g_post_mix[i])
        else:
            h = gqa_layer(h, g_pre_mix[i], attn_w_in[j], attn_g_q[j], attn_g_k[j], attn_w_out[j],
                          g_post_mix[i])
        h = mlp_block(h, g_pre_mlp[i], w_mlp_up, w_mlp_down, g_post_mlp[i], i)
        h = ple_block(h, p, w_ple_gate, w_ple_proj, g_ple[i], i)
    return h[None]
```
